```python
import math
import jax, jax.numpy as jnp
from jax import lax
import numpy as np

D_MODEL = 1024
BATCH = 1
SEQ = 16384
DEPTH = 2
DEC_BATCH = 32
DEC_SEQ = 16
PAST_LEN = 2048

CHUNK = 64
Q_BLOCK = 128
CONV_A_CH = 512
CONV_A_WIDTH = 31
GDN_HEADS = 4
GDN_DK = 128
GDN_DV = 128
GDN_CONV_WIDTH = 4
GDN_QKV_WIDTH = 2 * GDN_HEADS * GDN_DK + GDN_HEADS * GDN_DV
MLA_HEADS = 4
MLA_Q_LORA = 384
MLA_KV_LORA = 256
MLA_NOPE_DIM = 128
MLA_ROPE_DIM = 64
MLA_V_DIM = 128
ROPE_THETA = 10000.0
N_BRANCH = 3
BRANCH_WIDTH = 512
D_FF = 2816
FFN_CONV_WIDTH = 3
DN_ALPHA = (2 * DEPTH) ** 0.25
DN_BETA = (8 * DEPTH) ** -0.25
LN_EPS = 1e-5
RMS_EPS = 1e-6

IN_SIZES = (
    2 * CONV_A_CH,
    GDN_QKV_WIDTH,
    GDN_HEADS * GDN_DV,
    GDN_HEADS,
    GDN_HEADS,
    MLA_Q_LORA,
    MLA_KV_LORA,
    MLA_ROPE_DIM,
    N_BRANCH * D_MODEL,
)
IN_WIDTH = sum(IN_SIZES)

kernel_name = 'hybrid_streaming_encoder_step'


def split_columns(z, sizes):
    offsets = np.cumsum(np.array(sizes))[:-1].tolist()
    return jnp.split(z, offsets, axis=-1)


def layer_norm(x, g, b):
    xf = x.astype(jnp.float32)
    mu = jnp.mean(xf, axis=-1, keepdims=True)
    var = jnp.mean(jnp.square(xf - mu), axis=-1, keepdims=True)
    return ((xf - mu) * lax.rsqrt(var + LN_EPS) * g.astype(jnp.float32) + b.astype(jnp.float32)).astype(x.dtype)


def rms_norm(x, g):
    xf = x.astype(jnp.float32)
    return (xf * lax.rsqrt(jnp.mean(xf * xf, axis=-1, keepdims=True) + RMS_EPS) * g.astype(jnp.float32)).astype(x.dtype)


def l2_normalize(x):
    return x * lax.rsqrt(jnp.sum(x * x, axis=-1, keepdims=True) + RMS_EPS)


def causal_dwconv(x, hist, w):
    k = w.shape[0]
    xp = jnp.concatenate([hist.astype(x.dtype), x], axis=1)
    y = lax.conv_general_dilated(xp, w[:, None, :].astype(x.dtype), window_strides=(1,), padding='VALID',
                                 dimension_numbers=('NWC', 'WIO', 'NWC'), feature_group_count=x.shape[-1])
    return y, xp[:, xp.shape[1] - (k - 1):]


def rope_tables(pos):
    half = MLA_ROPE_DIM // 2
    inv_freq = ROPE_THETA ** (-jnp.arange(half, dtype=jnp.float32) / half)
    ang = pos.astype(jnp.float32)[:, None] * inv_freq[None, :]
    return jnp.cos(ang), jnp.sin(ang)


def apply_rope(x, cos, sin):
    x1, x2 = jnp.split(x.astype(jnp.float32), 2, axis=-1)
    return jnp.concatenate([x1 * cos - x2 * sin, x2 * cos + x1 * sin], axis=-1).astype(x.dtype)


def chunk_attention(q, k, v, q_pos, k_pos):
    b, tq, h, dq = q.shape
    dv = v.shape[-1]
    scale = dq ** -0.5
    k_chunk = k_pos // CHUNK

    def attend(qb, qp):
        s = jnp.einsum('bqhd,bkhd->bhqk', qb, k, preferred_element_type=jnp.float32) * scale
        mask = k_chunk[None, :] <= (qp // CHUNK)[:, None]
        p = jax.nn.softmax(jnp.where(mask, s, -jnp.inf), axis=-1)
        return jnp.einsum('bhqk,bkhd->bqhd', p.astype(v.dtype), v)

    if tq > Q_BLOCK and tq % Q_BLOCK == 0:
        nb = tq // Q_BLOCK
        qb = jnp.moveaxis(q.reshape(b, nb, Q_BLOCK, h, dq), 1, 0)
        qp = q_pos.reshape(nb, Q_BLOCK)
        o = lax.map(lambda a: attend(a[0], a[1]), (qb, qp))
        return jnp.moveaxis(o, 0, 1).reshape(b, tq, h, dv)
    return attend(q, q_pos)


def gated_delta_rule(q, k, v, g, beta, state):
    b, t, h, dk = q.shape
    dv = v.shape[-1]
    c = min(CHUNK, t)
    n = t // c

    def chunks(a):
        return jnp.moveaxis(a.reshape((b, n, c, h) + a.shape[3:]), (1, 3), (0, 2))

    q, k, v, g, beta = chunks(q), chunks(k), chunks(v), chunks(g), chunks(beta)
    g = jnp.cumsum(g, axis=-1)
    tri = jnp.tril(jnp.ones((c, c), dtype=bool))
    strict = jnp.tril(jnp.ones((c, c), dtype=bool), -1)
    diff = g[..., :, None] - g[..., None, :]
    decay = jnp.where(tri, jnp.exp(jnp.where(tri, diff, 0.0)), 0.0)
    kk = jnp.einsum('nbhid,nbhjd->nbhij', k, k)
    lmat = jnp.where(strict, beta[..., :, None] * kk * decay, 0.0)
    rhs = jnp.concatenate([v * beta[..., None], k * (beta * jnp.exp(g))[..., None]], axis=-1)
    sol = lax.linalg.triangular_solve(jnp.eye(c, dtype=jnp.float32) + lmat, rhs,
                                      left_side=True, lower=True, unit_diagonal=True)
    u, w = sol[..., :dv], sol[..., dv:]
    qk = jnp.where(tri, jnp.einsum('nbhid,nbhjd->nbhij', q, k) * decay, 0.0)
    q_dec = q * jnp.exp(g)[..., None]
    g_last = g[..., -1]
    k_dec = k * jnp.exp(g_last[..., None] - g)[..., None]

    def step(s, xs):
        u_c, w_c, qk_c, qd_c, kd_c, gl_c = xs
        v_new = u_c - jnp.einsum('bhck,bhkv->bhcv', w_c, s)
        o = jnp.einsum('bhck,bhkv->bhcv', qd_c, s) + jnp.einsum('bhij,bhjv->bhiv', qk_c, v_new)
        s = s * jnp.exp(gl_c)[..., None, None] + jnp.einsum('bhck,bhcv->bhkv', kd_c, v_new)
        return s, o

    state, o = lax.scan(step, state, (u, w, qk, q_dec, k_dec, g_last))
    o = jnp.moveaxis(o, (0, 2), (1, 3)).reshape(b, t, h, dv)
    return o, state


def conv_module(pre, hist, w, bias, ln_g, ln_b):
    a, gate = jnp.split(pre, 2, axis=-1)
    u = a * jax.nn.sigmoid(gate)
    y, hist = causal_dwconv(u, hist, w)
    return jax.nn.silu(layer_norm(y + bias, ln_g, ln_b)), hist


def gated_deltanet(qkv, z, beta_pre, dec_pre, hist, state, conv_w, a_log, dt_bias, norm_g):
    b, t, _ = qkv.shape
    qkv, hist = causal_dwconv(qkv, hist, conv_w)
    qkv = jax.nn.silu(qkv.astype(jnp.float32))
    nk = GDN_HEADS * GDN_DK
    q = l2_normalize(qkv[..., :nk].reshape(b, t, GDN_HEADS, GDN_DK)) * (GDN_DK ** -0.5)
    k = l2_normalize(qkv[..., nk:2 * nk].reshape(b, t, GDN_HEADS, GDN_DK))
    v = qkv[..., 2 * nk:].reshape(b, t, GDN_HEADS, GDN_DV)
    beta = jax.nn.sigmoid(beta_pre.astype(jnp.float32))
    g = -jnp.exp(a_log.astype(jnp.float32)) * jax.nn.softplus(dec_pre.astype(jnp.float32) + dt_bias.astype(jnp.float32))
    o, state = gated_delta_rule(q, k, v, g, beta, state.astype(jnp.float32))
    o = rms_norm(o, norm_g) * jax.nn.silu(z.astype(jnp.float32).reshape(b, t, GDN_HEADS, GDN_DV))
    return o.reshape(b, t, GDN_HEADS * GDN_DV), hist, state


def latent_attention(q_lat, kv_lat, k_pe, cache_lat, cache_kpe, q_norm_g, kv_norm_g, w_uq, w_ukv):
    b, t, _ = q_lat.shape
    past = cache_lat.shape[1]
    tk = past + t
    q_pos = past + jnp.arange(t, dtype=jnp.int32)
    k_pos = jnp.arange(tk, dtype=jnp.int32)
    cos, sin = rope_tables(q_pos)
    q = (rms_norm(q_lat, q_norm_g) @ w_uq).reshape(b, t, MLA_HEADS, MLA_NOPE_DIM + MLA_ROPE_DIM)
    q = jnp.concatenate([q[..., :MLA_NOPE_DIM],
                         apply_rope(q[..., MLA_NOPE_DIM:], cos[:, None, :], sin[:, None, :])], axis=-1)
    lat_new = rms_norm(kv_lat, kv_norm_g)
    kpe_new = apply_rope(k_pe, cos, sin)
    lat_all = jnp.concatenate([cache_lat.astype(lat_new.dtype), lat_new], axis=1)
    kpe_all = jnp.concatenate([cache_kpe.astype(kpe_new.dtype), kpe_new], axis=1)
    kv = (lat_all @ w_ukv).reshape(b, tk, MLA_HEADS, MLA_NOPE_DIM + MLA_V_DIM)
    k = jnp.concatenate([kv[..., :MLA_NOPE_DIM],
                         jnp.broadcast_to(kpe_all[:, :, None, :], (b, tk, MLA_HEADS, MLA_ROPE_DIM))], axis=-1)
    o = chunk_attention(q, k.astype(q.dtype), kv[..., MLA_NOPE_DIM:], q_pos, k_pos)
    return o.reshape(b, t, MLA_HEADS * MLA_V_DIM), lat_new, kpe_new


def run_layer(x, c, cache_lat, cache_kpe, hist_a, hist_b, state_b, hist_f, p):
    b, t, _ = x.shape
    mod = jax.nn.silu(c) @ p['w_ada'] + p['b_ada']
    sh1, sc1, g1, sh2, sc2, g2 = jnp.split(mod[:, None, :], 6, axis=-1)
    h = x * (1 + sc1) + sh1
    (pre_a, qkv_b, z_b, beta_b, dec_b, q_lat, kv_lat, k_pe, gate_pre) = split_columns(h @ p['w_in'], IN_SIZES)
    y_a, hist_a = conv_module(pre_a, hist_a, p['conv_a_w'], p['conv_a_b'], p['ln_a_g'], p['ln_a_b'])
    y_b, hist_b, state_b = gated_deltanet(qkv_b, z_b, beta_b, dec_b, hist_b, state_b, p['gdn_conv_w'],
                                          p['gdn_a_log'], p['gdn_dt_bias'], p['gdn_norm_g'])
    y_c, lat_new, kpe_new = latent_attention(q_lat, kv_lat, k_pe, cache_lat, cache_kpe, p['mla_q_norm_g'],
                                             p['mla_kv_norm_g'], p['mla_w_uq'], p['mla_w_ukv'])
    branches = jnp.stack([y_a.astype(x.dtype), y_b.astype(x.dtype), y_c.astype(x.dtype)], axis=0)
    proj = jnp.einsum('nbtc,ncd->btnd', branches, p['w_branch'])
    gates = jax.nn.sigmoid(gate_pre.astype(jnp.float32)).reshape(b, t, N_BRANCH, D_MODEL)
    merged = jnp.sum(gates * proj, axis=2).astype(x.dtype)
    x = layer_norm(DN_ALPHA * x + (1 + g1) * (merged @ p['w_out']), p['ln1_g'], p['ln1_b'])
    h = x * (1 + sc2) + sh2
    a, v = jnp.split(h @ p['w_up'], 2, axis=-1)
    a, hist_f = causal_dwconv(a, hist_f, p['ffn_conv_w'])
    y = (jax.nn.silu(a + p['ffn_conv_b']) * v) @ p['w_down']
    x = layer_norm(DN_ALPHA * x + (1 + g2) * y, p['ln2_g'], p['ln2_b'])
    return x, (lat_new, kpe_new, hist_a, hist_b, state_b, hist_f)


def setup_inputs(seed: int = 0) -> dict:
    key = jax.random.key(seed)
    ks = iter(jax.random.split(key, 64))

    def nrm(shape, scale):
        return scale * jax.random.normal(next(ks), shape, jnp.float32)

    def gain(shape):
        return 1.0 + nrm(shape, 0.02)

    log_dt = jax.random.uniform(next(ks), (DEPTH, GDN_HEADS), jnp.float32, math.log(1e-3), math.log(1e-1))
    dt = jnp.exp(log_dt)
    a_init = jax.random.uniform(next(ks), (DEPTH, GDN_HEADS), jnp.float32, 1.0, 16.0)
    return {
        'x_prompt': nrm((BATCH, SEQ, D_MODEL), 1.0),
        'x_sample': nrm((DEC_BATCH, DEC_SEQ, D_MODEL), 1.0),
        'cache_mla_latent': nrm((DEPTH, DEC_BATCH, PAST_LEN, MLA_KV_LORA), 1.0),
        'cache_mla_kpe': nrm((DEPTH, DEC_BATCH, PAST_LEN, MLA_ROPE_DIM), 1.0),
        'state_conv_a': nrm((DEPTH, DEC_BATCH, CONV_A_WIDTH - 1, CONV_A_CH), 0.5),
        'state_gdn_conv': nrm((DEPTH, DEC_BATCH, GDN_CONV_WIDTH - 1, GDN_QKV_WIDTH), 1.0),
        'state_gdn': nrm((DEPTH, DEC_BATCH, GDN_HEADS, GDN_DK, GDN_DV), 0.1),
        'state_ffn_conv': nrm((DEPTH, DEC_BATCH, FFN_CONV_WIDTH - 1, D_FF), 1.0),
        'c_prompt': nrm((BATCH, D_MODEL), 1.0),
        'c_sample': nrm((DEC_BATCH, D_MODEL), 1.0),
        'ln0_g': gain((D_MODEL,)),
        'ln0_b': nrm((D_MODEL,), 0.02),
        'w_ada': nrm((DEPTH, D_MODEL, 6 * D_MODEL), 0.2 * D_MODEL ** -0.5),
        'b_ada': nrm((DEPTH, 6 * D_MODEL), 0.01),
        'w_in': nrm((DEPTH, D_MODEL, IN_WIDTH), D_MODEL ** -0.5),
        'conv_a_w': nrm((DEPTH, CONV_A_WIDTH, CONV_A_CH), CONV_A_WIDTH ** -0.5),
        'conv_a_b': nrm((DEPTH, CONV_A_CH), 0.02),
        'ln_a_g': gain((DEPTH, CONV_A_CH)),
        'ln_a_b': nrm((DEPTH, CONV_A_CH), 0.02),
        'gdn_conv_w': nrm((DEPTH, GDN_CONV_WIDTH, GDN_QKV_WIDTH), GDN_CONV_WIDTH ** -0.5),
        'gdn_a_log': jnp.log(a_init),
        'gdn_dt_bias': dt + jnp.log(-jnp.expm1(-dt)),
        'gdn_norm_g': gain((DEPTH, GDN_DV)),
        'mla_q_norm_g': gain((DEPTH, MLA_Q_LORA)),
        'mla_kv_norm_g': gain((DEPTH, MLA_KV_LORA)),
        'mla_w_uq': nrm((DEPTH, MLA_Q_LORA, MLA_HEADS * (MLA_NOPE_DIM + MLA_ROPE_DIM)), MLA_Q_LORA ** -0.5),
        'mla_w_ukv': nrm((DEPTH, MLA_KV_LORA, MLA_HEADS * (MLA_NOPE_DIM + MLA_V_DIM)), MLA_KV_LORA ** -0.5),
        'w_branch': nrm((DEPTH, N_BRANCH, BRANCH_WIDTH, D_MODEL), BRANCH_WIDTH ** -0.5),
        'w_out': nrm((DEPTH, D_MODEL, D_MODEL), DN_BETA * D_MODEL ** -0.5),
        'ln1_g': gain((DEPTH, D_MODEL)),
        'ln1_b': nrm((DEPTH, D_MODEL), 0.02),
        'w_up': nrm((DEPTH, D_MODEL, 2 * D_FF), D_MODEL ** -0.5),
        'ffn_conv_w': nrm((DEPTH, FFN_CONV_WIDTH, D_FF), FFN_CONV_WIDTH ** -0.5),
        'ffn_conv_b': nrm((DEPTH, D_FF), 0.02),
        'w_down': nrm((DEPTH, D_FF, D_MODEL), DN_BETA * D_FF ** -0.5),
        'ln2_g': gain((DEPTH, D_MODEL)),
        'ln2_b': nrm((DEPTH, D_MODEL), 0.02),
    }


def reference(x_prompt, x_sample, cache_mla_latent, cache_mla_kpe, state_conv_a, state_gdn_conv, state_gdn,
              state_ffn_conv, c_prompt, c_sample, ln0_g, ln0_b, w_ada, b_ada, w_in, conv_a_w, conv_a_b,
              ln_a_g, ln_a_b, gdn_conv_w, gdn_a_log, gdn_dt_bias, gdn_norm_g, mla_q_norm_g, mla_kv_norm_g,
              mla_w_uq, mla_w_ukv, w_branch, w_out, ln1_g, ln1_b, w_up, ffn_conv_w, ffn_conv_b, w_down,
              ln2_g, ln2_b):
    def layer_params(l):
        return {'w_ada': w_ada[l], 'b_ada': b_ada[l], 'w_in': w_in[l], 'conv_a_w': conv_a_w[l],
                'conv_a_b': conv_a_b[l], 'ln_a_g': ln_a_g[l], 'ln_a_b': ln_a_b[l], 'gdn_conv_w': gdn_conv_w[l],
                'gdn_a_log': gdn_a_log[l], 'gdn_dt_bias': gdn_dt_bias[l], 'gdn_norm_g': gdn_norm_g[l],
                'mla_q_norm_g': mla_q_norm_g[l], 'mla_kv_norm_g': mla_kv_norm_g[l], 'mla_w_uq': mla_w_uq[l],
                'mla_w_ukv': mla_w_ukv[l], 'w_branch': w_branch[l], 'w_out': w_out[l], 'ln1_g': ln1_g[l],
                'ln1_b': ln1_b[l], 'w_up': w_up[l], 'ffn_conv_w': ffn_conv_w[l], 'ffn_conv_b': ffn_conv_b[l],
                'w_down': w_down[l], 'ln2_g': ln2_g[l], 'ln2_b': ln2_b[l]}

    bp = x_prompt.shape[0]
    dt = x_prompt.dtype
    xp = layer_norm(x_prompt, ln0_g, ln0_b)
    xs = layer_norm(x_sample, ln0_g, ln0_b)
    p_states, s_states = [], []
    for l in range(DEPTH):
        p = layer_params(l)
        xp, st_p = run_layer(xp, c_prompt,
                             jnp.zeros((bp, 0, MLA_KV_LORA), dt), jnp.zeros((bp, 0, MLA_ROPE_DIM), dt),
                             jnp.zeros((bp, CONV_A_WIDTH - 1, CONV_A_CH), dt),
                             jnp.zeros((bp, GDN_CONV_WIDTH - 1, GDN_QKV_WIDTH), dt),
                             jnp.zeros((bp, GDN_HEADS, GDN_DK, GDN_DV), jnp.float32),
                             jnp.zeros((bp, FFN_CONV_WIDTH - 1, D_FF), dt), p)
        xs, st_s = run_layer(xs, c_sample, cache_mla_latent[l], cache_mla_kpe[l], state_conv_a[l],
                             state_gdn_conv[l], state_gdn[l], state_ffn_conv[l], p)
        p_states.append(st_p)
        s_states.append(st_s)
    p_lat, p_kpe, p_conv_a, p_gdn_conv, p_gdn, p_ffn = [jnp.stack(z, axis=0) for z in zip(*p_states)]
    s_lat, s_kpe, s_conv_a, s_gdn_conv, s_gdn, s_ffn = [jnp.stack(z, axis=0) for z in zip(*s_states)]
    return (xp, xs, p_lat, p_kpe, p_conv_a, p_gdn_conv, p_gdn, p_ffn,
            s_lat, s_kpe, s_conv_a, s_gdn_conv, s_gdn, s_ffn)
```

```python
import functools
import math

import jax
import jax.numpy as jnp
import numpy as np
from jax import lax
from jax.experimental import pallas as pl
from jax.experimental.pallas import tpu as pltpu

F32 = jnp.float32
BF16 = jnp.bfloat16

D_MODEL = 1024
CHUNK = 64
CONV_A_CH = 512
CONV_A_WIDTH = 31
GDN_HEADS = 4
GDN_DK = 128
GDN_DV = 128
GDN_CONV_WIDTH = 4
GDN_QKV_WIDTH = 2 * GDN_HEADS * GDN_DK + GDN_HEADS * GDN_DV
MLA_HEADS = 4
MLA_Q_LORA = 384
MLA_KV_LORA = 256
MLA_NOPE_DIM = 128
MLA_ROPE_DIM = 64
MLA_V_DIM = 128
ROPE_THETA = 10000.0
N_BRANCH = 3
BRANCH_WIDTH = 512
D_FF = 2816
FFN_CONV_WIDTH = 3
LN_EPS = 1e-5
RMS_EPS = 1e-6

LANES = 128
SUBLANES = 8
VMEM_LIMIT = 48 * 1024 * 1024

GATE_OFF, GATE_W = 0, N_BRANCH * D_MODEL
QKV_OFF = GATE_OFF + GATE_W
Z_OFF = QKV_OFF + GDN_QKV_WIDTH
PREA_OFF = Z_OFF + GDN_HEADS * GDN_DV
QLAT_OFF = PREA_OFF + 2 * CONV_A_CH
SMALL_OFF = QLAT_OFF + MLA_Q_LORA
KVLAT_OFF = SMALL_OFF + LANES
FAT_W = KVLAT_OFF + MLA_KV_LORA
MLA_BLK_W = FAT_W - QLAT_OFF
BETA_LANE = MLA_ROPE_DIM
DEC_LANE = MLA_ROPE_DIM + GDN_HEADS
QK_PAD = 2 * LANES


def _cparams(sem):
    return pltpu.CompilerParams(dimension_semantics=sem, vmem_limit_bytes=VMEM_LIMIT)


def _sigmoid(x):
    return jax.nn.sigmoid(x)


def _silu(x):
    return x * jax.nn.sigmoid(x)


def _ln(x, g, b):
    mu = jnp.mean(x, axis=-1, keepdims=True)
    xc = x - mu
    var = jnp.mean(xc * xc, axis=-1, keepdims=True)
    return xc * lax.rsqrt(var + LN_EPS) * g + b


def _rms(x, g):
    return x * lax.rsqrt(jnp.mean(x * x, axis=-1, keepdims=True) + RMS_EPS) * g


def _dot(a, b):
    return jnp.dot(a, b, preferred_element_type=F32)


def _dot_nt(a, b):
    return lax.dot_general(a, b, (((1,), (1,)), ((), ())), preferred_element_type=F32)


def _dot_tn(a, b):
    return lax.dot_general(a, b, (((0,), (0,)), ((), ())), preferred_element_type=F32)


def _split3(x):
    hi = x.astype(BF16)
    r = x - hi.astype(F32)
    mid = r.astype(BF16)
    lo = (r - mid.astype(F32)).astype(BF16)
    return hi, mid, lo


def _mod_kernel(c_ref, w_ref, b_ref, o_ref):
    s = _silu(c_ref[...])
    o_ref[0] = _dot(s.astype(BF16), w_ref[0].astype(BF16)) + b_ref[0]


def _modulation(c_all, w_ada, b_ada):
    depth, d, n = w_ada.shape
    rows = c_all.shape[0]
    tn = 1536
    return pl.pallas_call(
        _mod_kernel,
        grid=(depth, n // tn),
        in_specs=[pl.BlockSpec((rows, d), lambda l, j: (0, 0)),
                  pl.BlockSpec((1, d, tn), lambda l, j: (l, 0, j)),
                  pl.BlockSpec((1, 1, tn), lambda l, j: (l, 0, j))],
        out_specs=pl.BlockSpec((1, rows, tn), lambda l, j: (l, 0, j)),
        out_shape=jax.ShapeDtypeStruct((depth, rows, n), F32),
        compiler_params=_cparams(("arbitrary", "arbitrary")),
        name="modulation",
    )(c_all, w_ada, b_ada.reshape(depth, 1, n))


def _inproj_kernel(x_ref, sc_ref, sh_ref, g0_ref, b0_ref, w_ref, o_ref, *, ln0):
    nb, tb, d = x_ref.shape
    x = x_ref[...]
    if ln0:
        x = _ln(x, g0_ref[...], b0_ref[...])
    h = x * (1.0 + sc_ref[...]) + sh_ref[...]
    o = _dot(h.reshape(nb * tb, d).astype(BF16), w_ref[...])
    o_ref[...] = o.reshape(nb, tb, o.shape[-1])


def _inproj(x, sc, sh, g0, b0, w_fat, *, nb, tb, ln0):
    bsz, t, d = x.shape
    tn = FAT_W // 3
    return pl.pallas_call(
        functools.partial(_inproj_kernel, ln0=ln0),
        grid=(FAT_W // tn, bsz // nb, t // tb),
        in_specs=[pl.BlockSpec((nb, tb, d), lambda n, b, i: (b, i, 0)),
                  pl.BlockSpec((nb, 1, d), lambda n, b, i: (b, 0, 0)),
                  pl.BlockSpec((nb, 1, d), lambda n, b, i: (b, 0, 0)),
                  pl.BlockSpec((1, d), lambda n, b, i: (0, 0)),
                  pl.BlockSpec((1, d), lambda n, b, i: (0, 0)),
                  pl.BlockSpec((d, tn), lambda n, b, i: (0, n))],
        out_specs=pl.BlockSpec((nb, tb, tn), lambda n, b, i: (b, i, n)),
        out_shape=jax.ShapeDtypeStruct((bsz, t, FAT_W), F32),
        compiler_params=_cparams(("arbitrary", "arbitrary", "arbitrary")),
        name="inproj",
    )(x, sc, sh, g0, b0, w_fat)


CONV_A_HALO = 32


def _glu(pre):
    return pre[..., :CONV_A_CH] * _sigmoid(pre[..., CONV_A_CH:])


def _conva_kernel(*refs, n_t, nbc, rc):
    if n_t > 1:
        pre_ref, halo_ref, hist_ref, w_ref, cb_ref, g_ref, b_ref, y_ref, ho_ref, xp_ref = refs
    else:
        pre_ref, hist_ref, w_ref, cb_ref, g_ref, b_ref, y_ref, ho_ref, xp_ref = refs
    nb, tb, _ = pre_ref.shape
    t = pl.program_id(1)
    hl = CONV_A_HALO

    @pl.when(t == 0)
    def _():
        xp_ref[:, 0:hl, :] = hist_ref[...]

    if n_t > 1:
        @pl.when(t > 0)
        def _():
            xp_ref[:, 0:hl, :] = _glu(halo_ref[...])

    xp_ref[:, hl:hl + tb, :] = _glu(pre_ref[...])
    off = hl - (CONV_A_WIDTH - 1)
    for b0 in range(0, nb, nbc):
        for r0 in range(0, tb, rc):
            acc = w_ref[0:1, :] * xp_ref[b0:b0 + nbc, r0 + off:r0 + off + rc, :]
            for k in range(1, CONV_A_WIDTH):
                acc = acc + w_ref[k:k + 1, :] * xp_ref[b0:b0 + nbc, r0 + off + k:r0 + off + k + rc, :]
            y = _silu(_ln(acc + cb_ref[...], g_ref[...], b_ref[...]))
            y_ref[b0:b0 + nbc, r0:r0 + rc, :] = y.astype(y_ref.dtype)
    ho_ref[...] = xp_ref[:, tb:tb + hl, :]


def _conv_module(fat, hist32, w, cb, g, b, *, nb, tb):
    bsz, t, _ = fat.shape
    n_t = t // tb
    pre_blk = PREA_OFF // (2 * CONV_A_CH)
    hl = CONV_A_HALO
    in_specs = [pl.BlockSpec((nb, tb, 2 * CONV_A_CH), lambda bi, i: (bi, i, pre_blk))]
    args = [fat]
    if n_t > 1:
        per = tb // hl
        in_specs.append(pl.BlockSpec((nb, hl, 2 * CONV_A_CH),
                                     lambda bi, i: (bi, jnp.maximum(i * per - 1, 0), pre_blk)))
        args.append(fat)
    in_specs += [pl.BlockSpec((nb, hl, CONV_A_CH), lambda bi, i: (bi, 0, 0)),
                 pl.BlockSpec((CONV_A_WIDTH, CONV_A_CH), lambda bi, i: (0, 0)),
                 pl.BlockSpec((1, CONV_A_CH), lambda bi, i: (0, 0)),
                 pl.BlockSpec((1, CONV_A_CH), lambda bi, i: (0, 0)),
                 pl.BlockSpec((1, CONV_A_CH), lambda bi, i: (0, 0))]
    args += [hist32, w, cb, g, b]
    if n_t > 1:
        nbc, rc = 1, 64
    else:
        nbc, rc = 4, tb
    return pl.pallas_call(
        functools.partial(_conva_kernel, n_t=n_t, nbc=nbc, rc=rc),
        grid=(bsz // nb, n_t),
        in_specs=in_specs,
        out_specs=[pl.BlockSpec((nb, tb, CONV_A_CH), lambda bi, i: (bi, i, 0)),
                   pl.BlockSpec((nb, hl, CONV_A_CH), lambda bi, i: (bi, 0, 0))],
        out_shape=[jax.ShapeDtypeStruct((bsz, t, CONV_A_CH), BF16),
                   jax.ShapeDtypeStruct((bsz, hl, CONV_A_CH), F32)],
        scratch_shapes=[pltpu.VMEM((nb, hl + tb, CONV_A_CH), F32)],
        compiler_params=_cparams(("arbitrary", "arbitrary")),
        name="conv_module",
    )(*args)


GDN_ROWS = 256


def _gdn_kernel(*refs, n_t, chunk):
    if n_t > 1:
        (qkv_ref, halo_ref, hist_ref, z_ref, small_ref, st_ref, cw_ref, alog_ref, dt_ref, ng_ref,
         y_ref, ho_ref, so_ref, xp_ref, s_ref, vn_ref, os_ref) = refs
    else:
        (qkv_ref, hist_ref, z_ref, small_ref, st_ref, cw_ref, alog_ref, dt_ref, ng_ref,
         y_ref, ho_ref, so_ref, xp_ref, s_ref, vn_ref, os_ref) = refs
    nb, tb, _ = qkv_ref.shape
    rows = nb * tb
    t = pl.program_id(1)
    hl = SUBLANES
    nk = GDN_HEADS * GDN_DK

    @pl.when(t == 0)
    def _():
        xp_ref[:, 0:hl, :] = hist_ref[...]
        s_ref[...] = st_ref[...]

    if n_t > 1:
        @pl.when(t > 0)
        def _():
            xp_ref[:, 0:hl, :] = halo_ref[...]

    xp_ref[:, hl:hl + tb, :] = qkv_ref[...]
    off = hl - (GDN_CONV_WIDTH - 1)
    acc = cw_ref[0:1, :] * xp_ref[:, off:off + tb, :]
    for k in range(1, GDN_CONV_WIDTH):
        acc = acc + cw_ref[k:k + 1, :] * xp_ref[:, off + k:off + k + tb, :]
    c = _silu(acc).reshape(rows, GDN_QKV_WIDTH)
    ho_ref[...] = xp_ref[:, tb:tb + hl, :]

    sm = small_ref[...].reshape(rows, LANES)
    beta_all = _sigmoid(sm)
    xg = sm + dt_ref[...]
    softplus = jnp.maximum(xg, 0.0) + jnp.log1p(jnp.exp(-jnp.abs(xg)))
    g_all = -jnp.exp(alog_ref[...]) * softplus

    shift = int(math.log2(chunk))
    ri = lax.broadcasted_iota(jnp.int32, (rows, rows), 0)
    ci = lax.broadcasted_iota(jnp.int32, (rows, rows), 1)
    same = (ri >> shift) == (ci >> shift)
    tri = same & (ci <= ri)
    strict = same & (ci < ri)
    tri_b = jnp.where(tri, 1.0, 0.0).astype(BF16)
    triu_b = jnp.where(same & (ri <= ci), 1.0, 0.0).astype(BF16)
    same_b = jnp.where(same, 1.0, 0.0).astype(BF16)
    g3 = _split3(g_all)
    gc_col = _dot(tri_b, g3[0]) + _dot(tri_b, g3[1]) + _dot(tri_b, g3[2])
    gc_row = _dot_tn(g3[0], triu_b) + _dot_tn(g3[1], triu_b) + _dot_tn(g3[2], triu_b)
    gl_col = _dot(same_b, g3[0]) + _dot(same_b, g3[1]) + _dot(same_b, g3[2])

    z = z_ref[...].reshape(rows, GDN_HEADS * GDN_DV)
    n_fac = shift
    for h in range(GDN_HEADS):
        qh = c[:, h * GDN_DK:(h + 1) * GDN_DK]
        kh = c[:, nk + h * GDN_DK:nk + (h + 1) * GDN_DK]
        vh = c[:, 2 * nk + h * GDN_DV:2 * nk + (h + 1) * GDN_DV]
        qh = qh * lax.rsqrt(jnp.sum(qh * qh, axis=-1, keepdims=True) + RMS_EPS) * (GDN_DK ** -0.5)
        kh = kh * lax.rsqrt(jnp.sum(kh * kh, axis=-1, keepdims=True) + RMS_EPS)
        beta = beta_all[:, BETA_LANE + h:BETA_LANE + h + 1]
        gc = gc_col[:, DEC_LANE + h:DEC_LANE + h + 1]
        gr = gc_row[DEC_LANE + h:DEC_LANE + h + 1, :]
        gl = gl_col[:, DEC_LANE + h:DEC_LANE + h + 1]
        decay = jnp.where(tri, jnp.exp(jnp.where(tri, gc - gr, 0.0)), 0.0)
        kb = kh.astype(BF16)
        kk = _dot_nt(kb, kb)
        a = jnp.where(strict, -(beta * kk * decay), 0.0)
        eg = jnp.exp(gc)
        sol = jnp.concatenate([vh * beta, kh * (beta * eg)], axis=1)
        for it in range(n_fac):
            ab = a.astype(BF16)
            sol = sol + _dot(ab, sol.astype(BF16))
            if it + 1 < n_fac:
                a = _dot(ab, ab)
        u = sol[:, :GDN_DV]
        wb = sol[:, GDN_DV:].astype(BF16)
        qkm = jnp.where(tri, _dot_nt(qh.astype(BF16), kb) * decay, 0.0)
        qd = (qh * eg).astype(BF16)
        kd = (kh * jnp.exp(gl - gc)).astype(BF16)
        egl = jnp.exp(gl)
        for j in range(rows // chunk):
            bi = (j * chunk) // tb
            r0 = j * chunk
            sc = s_ref[bi, h]
            sb = sc.astype(BF16)
            vnew = u[r0:r0 + chunk] - _dot(wb[r0:r0 + chunk], sb)
            os_ref[r0:r0 + chunk, :] = _dot(qd[r0:r0 + chunk], sb)
            vn_ref[r0:r0 + chunk, :] = vnew
            s_ref[bi, h] = sc * egl[r0:r0 + 1, :] + _dot_tn(kd[r0:r0 + chunk], vnew.astype(BF16))
        o = os_ref[...] + _dot(qkm.astype(BF16), vn_ref[...].astype(BF16))
        on = _rms(o, ng_ref[...])
        zh = z[:, h * GDN_DV:(h + 1) * GDN_DV]
        y_ref[:, :, h * GDN_DV:(h + 1) * GDN_DV] = (on * _silu(zh)).reshape(nb, tb, GDN_DV).astype(y_ref.dtype)
    so_ref[...] = s_ref[...]


def _gated_deltanet(fat, hist8, state, cw, alog_row, dt_row, ng, *, nb, tb):
    bsz, t, _ = fat.shape
    n_t = t // tb
    chunk = min(CHUNK, t)
    hl = SUBLANES
    qkv_blk = QKV_OFF // GDN_QKV_WIDTH
    z_blk = Z_OFF // (GDN_HEADS * GDN_DV)
    small_blk = SMALL_OFF // LANES
    in_specs = [pl.BlockSpec((nb, tb, GDN_QKV_WIDTH), lambda bi, i: (bi, i, qkv_blk))]
    args = [fat]
    if n_t > 1:
        per = tb // hl
        in_specs.append(pl.BlockSpec((nb, hl, GDN_QKV_WIDTH), lambda bi, i: (bi, jnp.maximum(i * per - 1, 0), qkv_blk)))
        args.append(fat)
    in_specs += [pl.BlockSpec((nb, hl, GDN_QKV_WIDTH), lambda bi, i: (bi, 0, 0)),
                 pl.BlockSpec((nb, tb, GDN_HEADS * GDN_DV), lambda bi, i: (bi, i, z_blk)),
                 pl.BlockSpec((nb, tb, LANES), lambda bi, i: (bi, i, small_blk)),
                 pl.BlockSpec((nb, GDN_HEADS, GDN_DK, GDN_DV), lambda bi, i: (bi, 0, 0, 0)),
                 pl.BlockSpec((GDN_CONV_WIDTH, GDN_QKV_WIDTH), lambda bi, i: (0, 0)),
                 pl.BlockSpec((1, LANES), lambda bi, i: (0, 0)),
                 pl.BlockSpec((1, LANES), lambda bi, i: (0, 0)),
                 pl.BlockSpec((1, GDN_DV), lambda bi, i: (0, 0))]
    args += [hist8, fat, fat, state, cw, alog_row, dt_row, ng]
    rows = nb * tb
    return pl.pallas_call(
        functools.partial(_gdn_kernel, n_t=n_t, chunk=chunk),
        grid=(bsz // nb, n_t),
        in_specs=in_specs,
        out_specs=[pl.BlockSpec((nb, tb, GDN_HEADS * GDN_DV), lambda bi, i: (bi, i, 0)),
                   pl.BlockSpec((nb, hl, GDN_QKV_WIDTH), lambda bi, i: (bi, 0, 0)),
                   pl.BlockSpec((nb, GDN_HEADS, GDN_DK, GDN_DV), lambda bi, i: (bi, 0, 0, 0))],
        out_shape=[jax.ShapeDtypeStruct((bsz, t, GDN_HEADS * GDN_DV), BF16),
                   jax.ShapeDtypeStruct((bsz, hl, GDN_QKV_WIDTH), F32),
                   jax.ShapeDtypeStruct((bsz, GDN_HEADS, GDN_DK, GDN_DV), F32)],
        scratch_shapes=[pltpu.VMEM((nb, hl + tb, GDN_QKV_WIDTH), F32),
                        pltpu.VMEM((nb, GDN_HEADS, GDN_DK, GDN_DV), F32),
                        pltpu.VMEM((rows, GDN_DV), F32),
                        pltpu.VMEM((rows, GDN_DV), F32)],
        compiler_params=_cparams(("arbitrary", "arbitrary")),
        name="gated_deltanet",
    )(*args)


def _mla_prep_kernel(*refs, emit_kv):
    if emit_kv:
        (blk_ref, cs_ref, sn_ref, qg_ref, kg_ref, wq_ref, wqs_ref, wkv_ref,
         q_ref, lat_ref, kpe_ref, k_ref, v_ref) = refs
    else:
        (blk_ref, cs_ref, sn_ref, qg_ref, kg_ref, wq_ref, wqs_ref,
         q_ref, lat_ref, kpe_ref) = refs
    nb, tb, _ = blk_ref.shape
    rows = nb * tb
    blk = blk_ref[...]
    cs = cs_ref[...]
    sn = sn_ref[...]
    scale = (MLA_NOPE_DIM + MLA_ROPE_DIM) ** -0.5

    qn = _rms(blk[..., :MLA_Q_LORA], qg_ref[...]).reshape(rows, MLA_Q_LORA).astype(BF16)
    qf = _dot(qn, wq_ref[...]).reshape(nb, tb, MLA_HEADS * QK_PAD)
    qs = _dot(qn, wqs_ref[...]).reshape(nb, tb, MLA_HEADS * LANES)
    for h in range(MLA_HEADS):
        nope = qf[..., h * QK_PAD:h * QK_PAD + MLA_NOPE_DIM]
        rope = qf[..., h * QK_PAD + MLA_NOPE_DIM:(h + 1) * QK_PAD] * cs + qs[..., h * LANES:(h + 1) * LANES] * sn
        q_ref[h, :, :, 0:MLA_NOPE_DIM] = (nope * scale).astype(q_ref.dtype)
        q_ref[h, :, :, MLA_NOPE_DIM:QK_PAD] = (rope * scale).astype(q_ref.dtype)

    lat = _rms(blk[..., MLA_Q_LORA + LANES:], kg_ref[...])
    lat_ref[...] = lat

    small = blk[..., MLA_Q_LORA:MLA_Q_LORA + LANES].reshape(rows, LANES)
    half = MLA_ROPE_DIM // 2
    lane = lax.broadcasted_iota(jnp.int32, (rows, LANES), 1)
    swapped = jnp.where(lane < half, pltpu.roll(small, LANES - half, 1), pltpu.roll(small, half, 1))
    kpe = small.reshape(nb, tb, LANES) * cs + swapped.reshape(nb, tb, LANES) * sn
    kpe_ref[...] = kpe[..., :MLA_ROPE_DIM]

    if emit_kv:
        kv = _dot(lat.reshape(rows, MLA_KV_LORA).astype(BF16), wkv_ref[...]).reshape(nb, tb, MLA_HEADS * 2 * LANES)
        for h in range(MLA_HEADS):
            k_ref[h, :, :, 0:MLA_NOPE_DIM] = kv[..., h * 2 * LANES:h * 2 * LANES + MLA_NOPE_DIM].astype(k_ref.dtype)
            k_ref[h, :, :, MLA_NOPE_DIM:QK_PAD] = kpe.astype(k_ref.dtype)
            v_ref[h] = kv[..., h * 2 * LANES + MLA_NOPE_DIM:(h + 1) * 2 * LANES].astype(v_ref.dtype)


def _mla_prep(fat, cs, sn, qg, kg, wq, wqs, wkv, *, nb, tb, emit_kv):
    bsz, t, _ = fat.shape
    blk = QLAT_OFF // MLA_BLK_W
    hq = MLA_HEADS
    in_specs = [pl.BlockSpec((nb, tb, MLA_BLK_W), lambda bi, i: (bi, i, blk)),
                pl.BlockSpec((tb, LANES), lambda bi, i: (i, 0)),
                pl.BlockSpec((tb, LANES), lambda bi, i: (i, 0)),
                pl.BlockSpec((1, MLA_Q_LORA), lambda bi, i: (0, 0)),
                pl.BlockSpec((1, MLA_KV_LORA), lambda bi, i: (0, 0)),
                pl.BlockSpec(wq.shape, lambda bi, i: (0, 0)),
                pl.BlockSpec(wqs.shape, lambda bi, i: (0, 0))]
    args = [fat, cs, sn, qg, kg, wq, wqs]
    out_specs = [pl.BlockSpec((hq, nb, tb, QK_PAD), lambda bi, i: (0, bi, i, 0)),
                 pl.BlockSpec((nb, tb, MLA_KV_LORA), lambda bi, i: (bi, i, 0)),
                 pl.BlockSpec((nb, tb, MLA_ROPE_DIM), lambda bi, i: (bi, i, 0))]
    out_shape = [jax.ShapeDtypeStruct((hq, bsz, t, QK_PAD), BF16),
                 jax.ShapeDtypeStruct((bsz, t, MLA_KV_LORA), F32),
                 jax.ShapeDtypeStruct((bsz, t, MLA_ROPE_DIM), F32)]
    if emit_kv:
        in_specs.append(pl.BlockSpec(wkv.shape, lambda bi, i: (0, 0)))
        args.append(wkv)
        out_specs += [pl.BlockSpec((hq, nb, tb, QK_PAD), lambda bi, i: (0, bi, i, 0)),
                      pl.BlockSpec((hq, nb, tb, MLA_V_DIM), lambda bi, i: (0, bi, i, 0))]
        out_shape += [jax.ShapeDtypeStruct((hq, bsz, t, QK_PAD), BF16),
                      jax.ShapeDtypeStruct((hq, bsz, t, MLA_V_DIM), BF16)]
    return pl.pallas_call(
        functools.partial(_mla_prep_kernel, emit_kv=emit_kv),
        grid=(bsz // nb, t // tb),
        in_specs=in_specs, out_specs=out_specs, out_shape=out_shape,
        compiler_params=_cparams(("arbitrary", "arbitrary")),
        name="mla_prep",
    )(*args)


def _chunk_mask(q0, k0, tq, tk):
    qc = (q0 + lax.broadcasted_iota(jnp.int32, (tq, tk), 0)) // CHUNK
    kc = (k0 + lax.broadcasted_iota(jnp.int32, (tq, tk), 1)) // CHUNK
    return kc <= qc


def _attn_prompt_kernel(q_ref, k_ref, v_ref, o_ref, m_ref, l_ref, acc_ref, *, tq):
    qi = pl.program_id(2)
    q = q_ref[0, 0]
    m_ref[...] = jnp.full(m_ref.shape, -jnp.inf, F32)
    l_ref[...] = jnp.zeros(l_ref.shape, F32)
    acc_ref[...] = jnp.zeros(acc_ref.shape, F32)

    def block(ki, masked):
        k0 = pl.multiple_of(ki * tq, tq)
        kb = k_ref[0, 0, pl.ds(k0, tq), :]
        vb = v_ref[0, 0, pl.ds(k0, tq), :]
        s = _dot_nt(q, kb)
        if masked:
            s = jnp.where(_chunk_mask(0, 0, tq, tq), s, -jnp.inf)
        m_old = m_ref[...]
        m_new = jnp.maximum(m_old, jnp.max(s, axis=-1, keepdims=True))
        p = jnp.exp(s - m_new)
        alpha = jnp.exp(m_old - m_new)
        l_ref[...] = alpha * l_ref[...] + jnp.sum(p, axis=-1, keepdims=True)
        acc_ref[...] = alpha * acc_ref[...] + _dot(p.astype(BF16), vb)
        m_ref[...] = m_new

    def body(ki, carry):
        block(ki, False)
        return carry

    lax.fori_loop(0, qi, body, 0)
    block(qi, True)
    o_ref[0] = (acc_ref[...] / l_ref[...]).astype(o_ref.dtype)


def _attn_prompt(q, k, v, *, tq):
    hq, bsz, t, _ = q.shape
    return pl.pallas_call(
        functools.partial(_attn_prompt_kernel, tq=tq),
        grid=(bsz, hq, t // tq),
        in_specs=[pl.BlockSpec((1, 1, tq, QK_PAD), lambda b, h, i: (h, b, i, 0)),
                  pl.BlockSpec((1, 1, t, QK_PAD), lambda b, h, i: (h, b, 0, 0)),
                  pl.BlockSpec((1, 1, t, MLA_V_DIM), lambda b, h, i: (h, b, 0, 0))],
        out_specs=pl.BlockSpec((1, tq, MLA_V_DIM), lambda b, h, i: (b, i, h)),
        out_shape=jax.ShapeDtypeStruct((bsz, t, MLA_HEADS * MLA_V_DIM), BF16),
        scratch_shapes=[pltpu.VMEM((tq, 1), F32), pltpu.VMEM((tq, 1), F32), pltpu.VMEM((tq, MLA_V_DIM), F32)],
        compiler_params=_cparams(("arbitrary", "arbitrary", "arbitrary")),
        name="attn_prompt",
    )(q, k, v)


def _attn_sample_kernel(q_ref, latc_ref, kpec_ref, latn_ref, kpen_ref, wuk_ref, wuv_ref, o_ref):
    tn = latn_ref.shape[1]
    past = latc_ref.shape[1]
    hq = MLA_HEADS
    qabs = jnp.concatenate(
        [_dot(q_ref[h, 0, :, 0:MLA_NOPE_DIM], wuk_ref[h]) for h in range(hq)], axis=0).astype(BF16)
    qrope = jnp.concatenate([q_ref[h, 0, :, MLA_NOPE_DIM:MLA_NOPE_DIM + MLA_ROPE_DIM] for h in range(hq)], axis=0)
    latc = latc_ref[0].astype(BF16)
    latn = latn_ref[0].astype(BF16)
    sc = _dot_nt(qabs, latc) + _dot_nt(qrope, kpec_ref[0].astype(BF16))
    sn = _dot_nt(qabs, latn) + _dot_nt(qrope, kpen_ref[0].astype(BF16))
    rows = hq * tn
    qpos = past + lax.broadcasted_iota(jnp.int32, (rows, 1), 0) % tn
    kc_pos = lax.broadcasted_iota(jnp.int32, (rows, past), 1)
    kn_pos = past + lax.broadcasted_iota(jnp.int32, (rows, tn), 1)
    sc = jnp.where(kc_pos // CHUNK <= qpos // CHUNK, sc, -jnp.inf)
    sn = jnp.where(kn_pos // CHUNK <= qpos // CHUNK, sn, -jnp.inf)
    m = jnp.maximum(jnp.max(sc, axis=-1, keepdims=True), jnp.max(sn, axis=-1, keepdims=True))
    pc = jnp.exp(sc - m)
    pn = jnp.exp(sn - m)
    l = jnp.sum(pc, axis=-1, keepdims=True) + jnp.sum(pn, axis=-1, keepdims=True)
    olat = (_dot(pc.astype(BF16), latc) + _dot(pn.astype(BF16), latn)) / l
    for h in range(hq):
        oh = _dot(olat[h * tn:(h + 1) * tn].astype(BF16), wuv_ref[h])
        o_ref[0, :, h * MLA_V_DIM:(h + 1) * MLA_V_DIM] = oh.astype(o_ref.dtype)


def _attn_sample(q, cache_lat, cache_kpe, lat_new, kpe_new, wuk_t, wuv):
    hq, bsz, tn, _ = q.shape
    past = cache_lat.shape[1]
    return pl.pallas_call(
        _attn_sample_kernel,
        grid=(bsz,),
        in_specs=[pl.BlockSpec((hq, 1, tn, QK_PAD), lambda b: (0, b, 0, 0)),
                  pl.BlockSpec((1, past, MLA_KV_LORA), lambda b: (b, 0, 0)),
                  pl.BlockSpec((1, past, MLA_ROPE_DIM), lambda b: (b, 0, 0)),
                  pl.BlockSpec((1, tn, MLA_KV_LORA), lambda b: (b, 0, 0)),
                  pl.BlockSpec((1, tn, MLA_ROPE_DIM), lambda b: (b, 0, 0)),
                  pl.BlockSpec(wuk_t.shape, lambda b: (0, 0, 0)),
                  pl.BlockSpec(wuv.shape, lambda b: (0, 0, 0))],
        out_specs=pl.BlockSpec((1, tn, MLA_HEADS * MLA_V_DIM), lambda b: (b, 0, 0)),
        out_shape=jax.ShapeDtypeStruct((bsz, tn, MLA_HEADS * MLA_V_DIM), BF16),
        compiler_params=_cparams(("arbitrary",)),
        name="attn_sample",
    )(q, cache_lat, cache_kpe, lat_new, kpe_new, wuk_t, wuv)


def _merge_kernel(ya_ref, yb_ref, yc_ref, gate_ref, x_ref, g1_ref, g0_ref, b0_ref, wb_ref, wo_ref,
                  lg_ref, lb_ref, o_ref, *, ln0, alpha):
    nb, tb, d = x_ref.shape
    rows = nb * tb
    x = x_ref[...]
    if ln0:
        x = _ln(x, g0_ref[...], b0_ref[...])
    merged = None
    for i, y_ref in enumerate((ya_ref, yb_ref, yc_ref)):
        proj = _dot(y_ref[...].reshape(rows, BRANCH_WIDTH), wb_ref[i])
        gate = _sigmoid(gate_ref[:, :, i * d:(i + 1) * d]).reshape(rows, d)
        merged = gate * proj if merged is None else merged + gate * proj
    out = _dot(merged.astype(BF16), wo_ref[...]).reshape(nb, tb, d)
    o_ref[...] = _ln(alpha * x + (1.0 + g1_ref[...]) * out, lg_ref[...], lb_ref[...])


def _merge(ya, yb, yc, fat, x, g1, g0, b0, wb, wo, lg, lb, *, nb, tb, ln0, alpha):
    bsz, t, d = x.shape
    row = lambda bi, i: (bi, i, 0)
    const2 = lambda bi, i: (0, 0)
    return pl.pallas_call(
        functools.partial(_merge_kernel, ln0=ln0, alpha=alpha),
        grid=(bsz // nb, t // tb),
        in_specs=[pl.BlockSpec((nb, tb, BRANCH_WIDTH), row),
                  pl.BlockSpec((nb, tb, BRANCH_WIDTH), row),
                  pl.BlockSpec((nb, tb, BRANCH_WIDTH), row),
                  pl.BlockSpec((nb, tb, GATE_W), row),
                  pl.BlockSpec((nb, tb, d), row),
                  pl.BlockSpec((nb, 1, d), lambda bi, i: (bi, 0, 0)),
                  pl.BlockSpec((1, d), const2),
                  pl.BlockSpec((1, d), const2),
                  pl.BlockSpec(wb.shape, lambda bi, i: (0, 0, 0)),
                  pl.BlockSpec(wo.shape, const2),
                  pl.BlockSpec((1, d), const2),
                  pl.BlockSpec((1, d), const2)],
        out_specs=pl.BlockSpec((nb, tb, d), row),
        out_shape=jax.ShapeDtypeStruct((bsz, t, d), F32),
        compiler_params=_cparams(("arbitrary", "arbitrary")),
        name="merge",
    )(ya, yb, yc, fat, x, g1, g0, b0, wb, wo, lg, lb)


FFN_BLOCK = 1408


def _ffn_kernel(x_ref, sc_ref, sh_ref, g2_ref, wa_ref, wv_ref, wd_ref, cw_ref, cb_ref, lg_ref, lb_ref, hist_ref,
                o_ref, ho_ref, h_ref, acc_ref, carry_ref, ap_ref, *, alpha):
    nb, tb, d = x_ref.shape
    rows = nb * tb
    t = pl.program_id(1)
    f = pl.program_id(2)
    n_f = pl.num_programs(2)
    hl = SUBLANES
    fb = wa_ref.shape[1]

    @pl.when(f == 0)
    def _():
        h = x_ref[...] * (1.0 + sc_ref[...]) + sh_ref[...]
        h_ref[...] = h.reshape(rows, d).astype(BF16)

    @pl.when(t == 0)
    def _():
        carry_ref[f] = hist_ref[...]

    h = h_ref[...]
    a = _dot(h, wa_ref[...]).reshape(nb, tb, fb)
    v = _dot(h, wv_ref[...]).reshape(nb, tb, fb)
    ap_ref[:, 0:hl, :] = carry_ref[f]
    ap_ref[:, hl:hl + tb, :] = a
    off = hl - (FFN_CONV_WIDTH - 1)
    conv = cw_ref[FFN_CONV_WIDTH - 1:FFN_CONV_WIDTH, :] * a
    for k in range(FFN_CONV_WIDTH - 1):
        conv = conv + cw_ref[k:k + 1, :] * ap_ref[:, off + k:off + k + tb, :]
    act = _silu(conv + cb_ref[...]) * v
    contrib = _dot(act.reshape(rows, fb).astype(BF16), wd_ref[...])
    last = ap_ref[:, tb:tb + hl, :]
    carry_ref[f] = last
    ho_ref[0] = last

    @pl.when(f == 0)
    def _():
        acc_ref[...] = contrib

    @pl.when(f > 0)
    def _():
        acc_ref[...] = acc_ref[...] + contrib

    @pl.when(f == n_f - 1)
    def _():
        y = acc_ref[...].reshape(nb, tb, d)
        o_ref[...] = _ln(alpha * x_ref[...] + (1.0 + g2_ref[...]) * y, lg_ref[...], lb_ref[...])


def _ffn(x, sc, sh, g2, wa, wv, wd, cw, cb, lg, lb, hist8, *, nb, tb, alpha):
    bsz, t, d = x.shape
    dff = wa.shape[1]
    fb = FFN_BLOCK
    n_f = dff // fb
    hl = SUBLANES
    rows = nb * tb
    row = lambda bi, i, f: (bi, i, 0)
    mod = lambda bi, i, f: (bi, 0, 0)
    const2 = lambda bi, i, f: (0, 0)
    return pl.pallas_call(
        functools.partial(_ffn_kernel, alpha=alpha),
        grid=(bsz // nb, t // tb, n_f),
        in_specs=[pl.BlockSpec((nb, tb, d), row),
                  pl.BlockSpec((nb, 1, d), mod),
                  pl.BlockSpec((nb, 1, d), mod),
                  pl.BlockSpec((nb, 1, d), mod),
                  pl.BlockSpec((d, fb), lambda bi, i, f: (0, f)),
                  pl.BlockSpec((d, fb), lambda bi, i, f: (0, f)),
                  pl.BlockSpec((fb, d), lambda bi, i, f: (f, 0)),
                  pl.BlockSpec((FFN_CONV_WIDTH, fb), lambda bi, i, f: (0, f)),
                  pl.BlockSpec((1, fb), lambda bi, i, f: (0, f)),
                  pl.BlockSpec((1, d), const2),
                  pl.BlockSpec((1, d), const2),
                  pl.BlockSpec((nb, hl, fb), lambda bi, i, f: (bi, 0, f))],
        out_specs=[pl.BlockSpec((nb, tb, d), row),
                   pl.BlockSpec((1, nb, hl, fb), lambda bi, i, f: (i, bi, 0, f))],
        out_shape=[jax.ShapeDtypeStruct((bsz, t, d), F32),
                   jax.ShapeDtypeStruct((t // tb, bsz, hl, dff), F32)],
        scratch_shapes=[pltpu.VMEM((rows, d), BF16),
                        pltpu.VMEM((rows, d), F32),
                        pltpu.VMEM((n_f, nb, hl, fb), F32),
                        pltpu.VMEM((nb, hl + tb, fb), F32)],
        compiler_params=_cparams(("arbitrary", "arbitrary", "arbitrary")),
        name="conv_ffn",
    )(x, sc, sh, g2, wa, wv, wd, cw, cb, lg, lb, hist8)


def _front_pad(hist, rows):
    b, r, c = hist.shape
    return jnp.concatenate([jnp.zeros((b, rows - r, c), hist.dtype), hist], axis=1)


def _rope_rows(pos):
    half = MLA_ROPE_DIM // 2
    inv_freq = ROPE_THETA ** (-jnp.arange(half, dtype=F32) / half)
    ang = pos.astype(F32)[:, None] * inv_freq[None, :]
    cos, sin = jnp.cos(ang), jnp.sin(ang)
    pad = jnp.zeros((pos.shape[0], LANES - MLA_ROPE_DIM), F32)
    return jnp.concatenate([cos, cos, pad], axis=1), jnp.concatenate([-sin, sin, pad], axis=1)


def _lane_row(vals, lane0):
    return jnp.zeros((1, LANES), F32).at[0, lane0:lane0 + vals.shape[0]].set(vals.astype(F32))


def _layer_weights(l, w_in, mla_w_uq, mla_w_ukv, w_branch, w_out, w_up, w_down):
    d = D_MODEL
    o = np.cumsum([0, 2 * CONV_A_CH, GDN_QKV_WIDTH, GDN_HEADS * GDN_DV, GDN_HEADS, GDN_HEADS,
                   MLA_Q_LORA, MLA_KV_LORA, MLA_ROPE_DIM, N_BRANCH * D_MODEL]).tolist()
    w = w_in[l]
    seg = lambda i: w[:, o[i]:o[i + 1]]
    pad = jnp.zeros((d, LANES - MLA_ROPE_DIM - 2 * GDN_HEADS), w.dtype)
    w_fat = jnp.concatenate([seg(8), seg(1), seg(2), seg(0), seg(5), seg(7), seg(3), seg(4), pad, seg(6)],
                            axis=1).astype(BF16)
    hd = MLA_NOPE_DIM + MLA_ROPE_DIM
    half = MLA_ROPE_DIM // 2
    wq = mla_w_uq[l]
    zq = jnp.zeros((MLA_Q_LORA, QK_PAD - hd), wq.dtype)
    zs = jnp.zeros((MLA_Q_LORA, LANES - MLA_ROPE_DIM), wq.dtype)
    wq_cols, wqs_cols = [], []
    for h in range(MLA_HEADS):
        wq_cols += [wq[:, h * hd:(h + 1) * hd], zq]
        r0 = h * hd + MLA_NOPE_DIM
        wqs_cols += [wq[:, r0 + half:r0 + 2 * half], wq[:, r0:r0 + half], zs]
    wkv = mla_w_ukv[l]
    kvw = MLA_NOPE_DIM + MLA_V_DIM
    wuk_t = jnp.stack([wkv[:, h * kvw:h * kvw + MLA_NOPE_DIM].T for h in range(MLA_HEADS)]).astype(BF16)
    wuv = jnp.stack([wkv[:, h * kvw + MLA_NOPE_DIM:(h + 1) * kvw] for h in range(MLA_HEADS)]).astype(BF16)
    return dict(
        w_fat=w_fat,
        wq=jnp.concatenate(wq_cols, axis=1).astype(BF16),
        wqs=jnp.concatenate(wqs_cols, axis=1).astype(BF16),
        wkv=wkv.astype(BF16), wuk_t=wuk_t, wuv=wuv,
        wb=w_branch[l].astype(BF16), wo=w_out[l].astype(BF16),
        wa=w_up[l][:, :D_FF].astype(BF16), wv=w_up[l][:, D_FF:].astype(BF16), wd=w_down[l].astype(BF16),
    )


def _run_layer(x, mods, st, wts, p, *, ln0, g0, b0, alpha, cs, sn, tiles, prompt):
    bsz, t, d = x.shape
    sh1, sc1, g1, sh2, sc2, g2 = mods
    nb, tb, gdn_nb, gdn_tb = tiles
    if st is None:
        hist_a = jnp.zeros((bsz, CONV_A_WIDTH - 1, CONV_A_CH), F32)
        hist_b = jnp.zeros((bsz, GDN_CONV_WIDTH - 1, GDN_QKV_WIDTH), F32)
        state_b = jnp.zeros((bsz, GDN_HEADS, GDN_DK, GDN_DV), F32)
        hist_f = jnp.zeros((bsz, FFN_CONV_WIDTH - 1, D_FF), F32)
    else:
        cache_lat, cache_kpe, hist_a, hist_b, state_b, hist_f = st
    row = lambda v: v.reshape(1, -1)

    fat = _inproj(x, sc1, sh1, g0, b0, wts['w_fat'], nb=nb, tb=tb, ln0=ln0)
    y_a, ha = _conv_module(fat, _front_pad(hist_a, CONV_A_HALO), p['conv_a_w'], row(p['conv_a_b']),
                           row(p['ln_a_g']), row(p['ln_a_b']), nb=nb, tb=tb)
    y_b, hb, sb = _gated_deltanet(fat, _front_pad(hist_b, SUBLANES), state_b, p['gdn_conv_w'],
                                  _lane_row(p['gdn_a_log'], DEC_LANE), _lane_row(p['gdn_dt_bias'], DEC_LANE),
                                  row(p['gdn_norm_g']), nb=gdn_nb, tb=gdn_tb)
    if prompt:
        q, lat, kpe, k, v = _mla_prep(fat, cs, sn, row(p['mla_q_norm_g']), row(p['mla_kv_norm_g']),
                                      wts['wq'], wts['wqs'], wts['wkv'], nb=nb, tb=tb, emit_kv=True)
        y_c = _attn_prompt(q, k, v, tq=tb)
    else:
        q, lat, kpe = _mla_prep(fat, cs, sn, row(p['mla_q_norm_g']), row(p['mla_kv_norm_g']),
                                wts['wq'], wts['wqs'], None, nb=nb, tb=tb, emit_kv=False)
        y_c = _attn_sample(q, cache_lat, cache_kpe, lat, kpe, wts['wuk_t'], wts['wuv'])
    x1 = _merge(y_a, y_b, y_c, fat, x, g1, g0, b0, wts['wb'], wts['wo'], row(p['ln1_g']), row(p['ln1_b']),
                nb=nb, tb=tb, ln0=ln0, alpha=alpha)
    x2, hf = _ffn(x1, sc2, sh2, g2, wts['wa'], wts['wv'], wts['wd'], p['ffn_conv_w'], row(p['ffn_conv_b']),
                  row(p['ln2_g']), row(p['ln2_b']), _front_pad(hist_f, SUBLANES), nb=nb, tb=tb, alpha=alpha)
    states = (lat, kpe, ha[:, CONV_A_HALO - (CONV_A_WIDTH - 1):], hb[:, SUBLANES - (GDN_CONV_WIDTH - 1):],
              sb, hf[-1, :, SUBLANES - (FFN_CONV_WIDTH - 1):])
    return x2, states


def kernel(x_prompt, x_sample, cache_mla_latent, cache_mla_kpe, state_conv_a, state_gdn_conv, state_gdn, state_ffn_conv, c_prompt, c_sample, ln0_g, ln0_b, w_ada, b_ada, w_in, conv_a_w, conv_a_b, ln_a_g, ln_a_b, gdn_conv_w, gdn_a_log, gdn_dt_bias, gdn_norm_g, mla_q_norm_g, mla_kv_norm_g, mla_w_uq, mla_w_ukv, w_branch, w_out, ln1_g, ln1_b, w_up, ffn_conv_w, ffn_conv_b, w_down, ln2_g, ln2_b):
    depth = w_ada.shape[0]
    bp, seq, d = x_prompt.shape
    bs, dec_seq, _ = x_sample.shape
    past = cache_mla_latent.shape[2]
    alpha = (2 * depth) ** 0.25

    n_c = bp + bs
    c_rows = -(-n_c // SUBLANES) * SUBLANES
    c_all = jnp.concatenate([c_prompt, c_sample, jnp.zeros((c_rows - n_c, d), F32)], axis=0)
    mod = _modulation(c_all, w_ada, b_ada)

    cs_p, sn_p = _rope_rows(jnp.arange(seq, dtype=jnp.int32))
    cs_s, sn_s = _rope_rows(past + jnp.arange(dec_seq, dtype=jnp.int32))
    g0, b0 = ln0_g.reshape(1, d), ln0_b.reshape(1, d)

    tile_p = min(512, seq)
    tiles_p = (1, tile_p, 1, min(GDN_ROWS, seq))
    tiles_s = (bs, dec_seq, GDN_ROWS // dec_seq, dec_seq)

    xp, xs = x_prompt, x_sample
    p_states, s_states = [], []
    for l in range(depth):
        p = {'conv_a_w': conv_a_w[l], 'conv_a_b': conv_a_b[l], 'ln_a_g': ln_a_g[l], 'ln_a_b': ln_a_b[l],
             'gdn_conv_w': gdn_conv_w[l], 'gdn_a_log': gdn_a_log[l], 'gdn_dt_bias': gdn_dt_bias[l],
             'gdn_norm_g': gdn_norm_g[l], 'mla_q_norm_g': mla_q_norm_g[l], 'mla_kv_norm_g': mla_kv_norm_g[l],
             'ln1_g': ln1_g[l], 'ln1_b': ln1_b[l], 'ffn_conv_w': ffn_conv_w[l], 'ffn_conv_b': ffn_conv_b[l],
             'ln2_g': ln2_g[l], 'ln2_b': ln2_b[l]}
        wts = _layer_weights(l, w_in, mla_w_uq, mla_w_ukv, w_branch, w_out, w_up, w_down)
        mods_p = [m.reshape(bp, 1, d) for m in jnp.split(mod[l, :bp], 6, axis=-1)]
        mods_s = [m.reshape(bs, 1, d) for m in jnp.split(mod[l, bp:n_c], 6, axis=-1)]
        xp, st_p = _run_layer(xp, mods_p, None, wts, p, ln0=(l == 0), g0=g0, b0=b0, alpha=alpha,
                              cs=cs_p, sn=sn_p, tiles=tiles_p, prompt=True)
        st_in = (cache_mla_latent[l], cache_mla_kpe[l], state_conv_a[l], state_gdn_conv[l], state_gdn[l],
                 state_ffn_conv[l])
        xs, st_s = _run_layer(xs, mods_s, st_in, wts, p, ln0=(l == 0), g0=g0, b0=b0, alpha=alpha,
                              cs=cs_s, sn=sn_s, tiles=tiles_s, prompt=False)
        p_states.append(st_p)
        s_states.append(st_s)
    p_out = [jnp.stack(z, axis=0) for z in zip(*p_states)]
    s_out = [jnp.stack(z, axis=0) for z in zip(*s_states)]
    return (xp, xs, *p_out, *s_out)
```

```python
import functools
import math

import jax
import jax.numpy as jnp
import numpy as np
from jax import lax
from jax.experimental import pallas as pl
from jax.experimental.pallas import tpu as pltpu

F32 = jnp.float32
BF16 = jnp.bfloat16

D_MODEL = 1024
CHUNK = 64
CONV_A_CH = 512
CONV_A_WIDTH = 31
GDN_HEADS = 4
GDN_DK = 128
GDN_DV = 128
GDN_CONV_WIDTH = 4
GDN_QKV_WIDTH = 2 * GDN_HEADS * GDN_DK + GDN_HEADS * GDN_DV
MLA_HEADS = 4
MLA_Q_LORA = 384
MLA_KV_LORA = 256
MLA_NOPE_DIM = 128
MLA_ROPE_DIM = 64
MLA_V_DIM = 128
ROPE_THETA = 10000.0
N_BRANCH = 3
BRANCH_WIDTH = 512
D_FF = 2816
FFN_CONV_WIDTH = 3
LN_EPS = 1e-5
RMS_EPS = 1e-6

LANES = 128
SUBLANES = 8
VMEM_LIMIT = 48 * 1024 * 1024

GATE_OFF, GATE_W = 0, N_BRANCH * D_MODEL
QKV_OFF = GATE_OFF + GATE_W
Z_OFF = QKV_OFF + GDN_QKV_WIDTH
PREA_OFF = Z_OFF + GDN_HEADS * GDN_DV
QLAT_OFF = PREA_OFF + 2 * CONV_A_CH
SMALL_OFF = QLAT_OFF + MLA_Q_LORA
KVLAT_OFF = SMALL_OFF + LANES
FAT_W = KVLAT_OFF + MLA_KV_LORA
MLA_BLK_W = FAT_W - QLAT_OFF
BETA_LANE = MLA_ROPE_DIM
DEC_LANE = MLA_ROPE_DIM + GDN_HEADS
QK_PAD = 2 * LANES
V_EXT = 2 * LANES
ATTN_SCALE = (MLA_NOPE_DIM + MLA_ROPE_DIM) ** -0.5


def _cparams(sem):
    return pltpu.CompilerParams(dimension_semantics=sem, vmem_limit_bytes=VMEM_LIMIT)


def _sigmoid(x):
    return jax.nn.sigmoid(x)


def _silu(x):
    return x * jax.nn.sigmoid(x)


def _ln(x, g, b):
    mu = jnp.mean(x, axis=-1, keepdims=True)
    xc = x - mu
    var = jnp.mean(xc * xc, axis=-1, keepdims=True)
    return xc * lax.rsqrt(var + LN_EPS) * g + b


def _rms(x, g):
    return x * lax.rsqrt(jnp.mean(x * x, axis=-1, keepdims=True) + RMS_EPS) * g


def _dot(a, b):
    return jnp.dot(a, b, preferred_element_type=F32)


def _dot_nt(a, b):
    return lax.dot_general(a, b, (((1,), (1,)), ((), ())), preferred_element_type=F32)


def _dot_tn(a, b):
    return lax.dot_general(a, b, (((0,), (0,)), ((), ())), preferred_element_type=F32)


def _dot_split(a, b):
    ah = a.astype(BF16)
    al = (a - ah.astype(F32)).astype(BF16)
    bh = b.astype(BF16)
    bl = (b - bh.astype(F32)).astype(BF16)
    return _dot(ah, bh) + _dot(ah, bl) + _dot(al, bh)


def _split3(x):
    hi = x.astype(BF16)
    r = x - hi.astype(F32)
    mid = r.astype(BF16)
    lo = (r - mid.astype(F32)).astype(BF16)
    return hi, mid, lo


def _mod_kernel(c_ref, w_ref, b_ref, o_ref):
    s = _silu(c_ref[...])
    o_ref[0] = _dot(s.astype(BF16), w_ref[0].astype(BF16)) + b_ref[0]


def _modulation(c_all, w_ada, b_ada):
    depth, d, n = w_ada.shape
    rows = c_all.shape[0]
    tn = 1536
    return pl.pallas_call(
        _mod_kernel,
        grid=(depth, n // tn),
        in_specs=[pl.BlockSpec((rows, d), lambda l, j: (0, 0)),
                  pl.BlockSpec((1, d, tn), lambda l, j: (l, 0, j)),
                  pl.BlockSpec((1, 1, tn), lambda l, j: (l, 0, j))],
        out_specs=pl.BlockSpec((1, rows, tn), lambda l, j: (l, 0, j)),
        out_shape=jax.ShapeDtypeStruct((depth, rows, n), F32),
        compiler_params=_cparams(("arbitrary", "arbitrary")),
        name="modulation",
    )(c_all, w_ada, b_ada.reshape(depth, 1, n))


def _inproj_kernel(x_ref, sc_ref, sh_ref, g0_ref, b0_ref, w_ref, o_ref, *, ln0):
    nb, tb, d = x_ref.shape
    x = x_ref[...]
    if ln0:
        x = _ln(x, g0_ref[...], b0_ref[...])
    h = x * (1.0 + sc_ref[...]) + sh_ref[...]
    o = _dot(h.reshape(nb * tb, d).astype(BF16), w_ref[...])
    o_ref[...] = o.reshape(nb, tb, o.shape[-1])


def _inproj(x, sc, sh, g0, b0, w_fat, *, nb, tb, ln0):
    bsz, t, d = x.shape
    tn = FAT_W // 3
    return pl.pallas_call(
        functools.partial(_inproj_kernel, ln0=ln0),
        grid=(FAT_W // tn, bsz // nb, t // tb),
        in_specs=[pl.BlockSpec((nb, tb, d), lambda n, b, i: (b, i, 0)),
                  pl.BlockSpec((nb, 1, d), lambda n, b, i: (b, 0, 0)),
                  pl.BlockSpec((nb, 1, d), lambda n, b, i: (b, 0, 0)),
                  pl.BlockSpec((1, d), lambda n, b, i: (0, 0)),
                  pl.BlockSpec((1, d), lambda n, b, i: (0, 0)),
                  pl.BlockSpec((d, tn), lambda n, b, i: (0, n))],
        out_specs=pl.BlockSpec((nb, tb, tn), lambda n, b, i: (b, i, n)),
        out_shape=jax.ShapeDtypeStruct((bsz, t, FAT_W), F32),
        compiler_params=_cparams(("arbitrary", "arbitrary", "arbitrary")),
        name="inproj",
    )(x, sc, sh, g0, b0, w_fat)


CONV_A_HALO = 32


def _glu(pre):
    return pre[..., :CONV_A_CH] * _sigmoid(pre[..., CONV_A_CH:])


def _conva_kernel(*refs, n_t, nbc, rc):
    if n_t > 1:
        pre_ref, halo_ref, hist_ref, w_ref, cb_ref, g_ref, b_ref, y_ref, ho_ref, xp_ref = refs
    else:
        pre_ref, hist_ref, w_ref, cb_ref, g_ref, b_ref, y_ref, ho_ref, xp_ref = refs
    nb, tb, _ = pre_ref.shape
    t = pl.program_id(1)
    hl = CONV_A_HALO

    @pl.when(t == 0)
    def _():
        xp_ref[:, 0:hl, :] = hist_ref[...]

    if n_t > 1:
        @pl.when(t > 0)
        def _():
            xp_ref[:, 0:hl, :] = _glu(halo_ref[...])

    xp_ref[:, hl:hl + tb, :] = _glu(pre_ref[...])
    off = hl - (CONV_A_WIDTH - 1)
    for b0 in range(0, nb, nbc):
        for r0 in range(0, tb, rc):
            acc = w_ref[0:1, :] * xp_ref[b0:b0 + nbc, r0 + off:r0 + off + rc, :]
            for k in range(1, CONV_A_WIDTH):
                acc = acc + w_ref[k:k + 1, :] * xp_ref[b0:b0 + nbc, r0 + off + k:r0 + off + k + rc, :]
            y = _silu(_ln(acc + cb_ref[...], g_ref[...], b_ref[...]))
            y_ref[b0:b0 + nbc, r0:r0 + rc, :] = y.astype(y_ref.dtype)
    ho_ref[...] = xp_ref[:, tb:tb + hl, :]


def _conv_module(fat, hist32, w, cb, g, b, *, nb, tb):
    bsz, t, _ = fat.shape
    n_t = t // tb
    pre_blk = PREA_OFF // (2 * CONV_A_CH)
    hl = CONV_A_HALO
    in_specs = [pl.BlockSpec((nb, tb, 2 * CONV_A_CH), lambda bi, i: (bi, i, pre_blk))]
    args = [fat]
    if n_t > 1:
        per = tb // hl
        in_specs.append(pl.BlockSpec((nb, hl, 2 * CONV_A_CH),
                                     lambda bi, i: (bi, jnp.maximum(i * per - 1, 0), pre_blk)))
        args.append(fat)
    in_specs += [pl.BlockSpec((nb, hl, CONV_A_CH), lambda bi, i: (bi, 0, 0)),
                 pl.BlockSpec((CONV_A_WIDTH, CONV_A_CH), lambda bi, i: (0, 0)),
                 pl.BlockSpec((1, CONV_A_CH), lambda bi, i: (0, 0)),
                 pl.BlockSpec((1, CONV_A_CH), lambda bi, i: (0, 0)),
                 pl.BlockSpec((1, CONV_A_CH), lambda bi, i: (0, 0))]
    args += [hist32, w, cb, g, b]
    if n_t > 1:
        nbc, rc = 1, 64
    else:
        nbc, rc = 4, tb
    return pl.pallas_call(
        functools.partial(_conva_kernel, n_t=n_t, nbc=nbc, rc=rc),
        grid=(bsz // nb, n_t),
        in_specs=in_specs,
        out_specs=[pl.BlockSpec((nb, tb, CONV_A_CH), lambda bi, i: (bi, i, 0)),
                   pl.BlockSpec((nb, hl, CONV_A_CH), lambda bi, i: (bi, 0, 0))],
        out_shape=[jax.ShapeDtypeStruct((bsz, t, CONV_A_CH), BF16),
                   jax.ShapeDtypeStruct((bsz, hl, CONV_A_CH), F32)],
        scratch_shapes=[pltpu.VMEM((nb, hl + tb, CONV_A_CH), F32)],
        compiler_params=_cparams(("arbitrary", "arbitrary")),
        name="conv_module",
    )(*args)


GDN_ROWS = 256


def _gdn_kernel(*refs, n_t, chunk):
    if n_t > 1:
        (qkv_ref, halo_ref, hist_ref, z_ref, small_ref, st_ref, cw_ref, alog_ref, dt_ref, ng_ref,
         y_ref, ho_ref, so_ref, xp_ref, s_ref, vn_ref, os_ref) = refs
    else:
        (qkv_ref, hist_ref, z_ref, small_ref, st_ref, cw_ref, alog_ref, dt_ref, ng_ref,
         y_ref, ho_ref, so_ref, xp_ref, s_ref, vn_ref, os_ref) = refs
    nb, tb, _ = qkv_ref.shape
    rows = nb * tb
    t = pl.program_id(1)
    hl = SUBLANES
    nk = GDN_HEADS * GDN_DK

    @pl.when(t == 0)
    def _():
        xp_ref[:, 0:hl, :] = hist_ref[...]
        s_ref[...] = st_ref[...]

    if n_t > 1:
        @pl.when(t > 0)
        def _():
            xp_ref[:, 0:hl, :] = halo_ref[...]

    xp_ref[:, hl:hl + tb, :] = qkv_ref[...]
    off = hl - (GDN_CONV_WIDTH - 1)
    acc = cw_ref[0:1, :] * xp_ref[:, off:off + tb, :]
    for k in range(1, GDN_CONV_WIDTH):
        acc = acc + cw_ref[k:k + 1, :] * xp_ref[:, off + k:off + k + tb, :]
    c = _silu(acc).reshape(rows, GDN_QKV_WIDTH)
    ho_ref[...] = xp_ref[:, tb:tb + hl, :]

    sm = small_ref[...].reshape(rows, LANES)
    beta_all = _sigmoid(sm)
    xg = sm + dt_ref[...]
    softplus = jnp.maximum(xg, 0.0) + jnp.log1p(jnp.exp(-jnp.abs(xg)))
    g_all = -jnp.exp(alog_ref[...]) * softplus

    shift = int(math.log2(chunk))
    ri = lax.broadcasted_iota(jnp.int32, (rows, rows), 0)
    ci = lax.broadcasted_iota(jnp.int32, (rows, rows), 1)
    same = (ri >> shift) == (ci >> shift)
    tri = same & (ci <= ri)
    strict = same & (ci < ri)
    tri_b = jnp.where(tri, 1.0, 0.0).astype(BF16)
    triu_b = jnp.where(same & (ri <= ci), 1.0, 0.0).astype(BF16)
    same_b = jnp.where(same, 1.0, 0.0).astype(BF16)
    g3 = _split3(g_all)
    gc_col = _dot(tri_b, g3[0]) + _dot(tri_b, g3[1]) + _dot(tri_b, g3[2])
    gc_row = _dot_tn(g3[0], triu_b) + _dot_tn(g3[1], triu_b) + _dot_tn(g3[2], triu_b)
    gl_col = _dot(same_b, g3[0]) + _dot(same_b, g3[1]) + _dot(same_b, g3[2])

    z = z_ref[...].reshape(rows, GDN_HEADS * GDN_DV)
    n_fac = shift
    for h in range(GDN_HEADS):
        qh = c[:, h * GDN_DK:(h + 1) * GDN_DK]
        kh = c[:, nk + h * GDN_DK:nk + (h + 1) * GDN_DK]
        vh = c[:, 2 * nk + h * GDN_DV:2 * nk + (h + 1) * GDN_DV]
        qh = qh * lax.rsqrt(jnp.sum(qh * qh, axis=-1, keepdims=True) + RMS_EPS) * (GDN_DK ** -0.5)
        kh = kh * lax.rsqrt(jnp.sum(kh * kh, axis=-1, keepdims=True) + RMS_EPS)
        beta = beta_all[:, BETA_LANE + h:BETA_LANE + h + 1]
        gc = gc_col[:, DEC_LANE + h:DEC_LANE + h + 1]
        gr = gc_row[DEC_LANE + h:DEC_LANE + h + 1, :]
        gl = gl_col[:, DEC_LANE + h:DEC_LANE + h + 1]
        decay = jnp.where(tri, jnp.exp(jnp.where(tri, gc - gr, 0.0)), 0.0)
        kb = kh.astype(BF16)
        kk = _dot_nt(kb, kb)
        a = jnp.where(strict, -(beta * kk * decay), 0.0)
        eg = jnp.exp(gc)
        rhs = jnp.concatenate([vh * beta, kh * (beta * eg)], axis=1)
        a0 = a
        m = a
        for _ in range(1, n_fac):
            ab = a.astype(BF16)
            a = _dot(ab, ab)
            m = m + a + _dot(m.astype(BF16), a.astype(BF16))
        mb = m.astype(BF16)
        x0 = rhs + _dot(mb, rhs.astype(BF16))
        res = rhs - x0 + _dot_split(a0, x0)
        sol = x0 + res + _dot(mb, res.astype(BF16))
        u = sol[:, :GDN_DV]
        wb = sol[:, GDN_DV:].astype(BF16)
        qkm = jnp.where(tri, _dot_nt(qh.astype(BF16), kb) * decay, 0.0)
        qd = (qh * eg).astype(BF16)
        kd = (kh * jnp.exp(gl - gc)).astype(BF16)
        egl = jnp.exp(gl)
        for j in range(rows // chunk):
            bi = (j * chunk) // tb
            r0 = j * chunk
            sc = s_ref[bi, h]
            sb = sc.astype(BF16)
            vnew = u[r0:r0 + chunk] - _dot(wb[r0:r0 + chunk], sb)
            os_ref[r0:r0 + chunk, :] = _dot(qd[r0:r0 + chunk], sb)
            vn_ref[r0:r0 + chunk, :] = vnew
            s_ref[bi, h] = sc * egl[r0:r0 + 1, :] + _dot_tn(kd[r0:r0 + chunk], vnew.astype(BF16))
        o = os_ref[...] + _dot(qkm.astype(BF16), vn_ref[...].astype(BF16))
        on = _rms(o, ng_ref[...])
        zh = z[:, h * GDN_DV:(h + 1) * GDN_DV]
        y_ref[:, :, h * GDN_DV:(h + 1) * GDN_DV] = (on * _silu(zh)).reshape(nb, tb, GDN_DV).astype(y_ref.dtype)
    so_ref[...] = s_ref[...]


def _gated_deltanet(fat, hist8, state, cw, alog_row, dt_row, ng, *, nb, tb):
    bsz, t, _ = fat.shape
    n_t = t // tb
    chunk = min(CHUNK, t)
    hl = SUBLANES
    qkv_blk = QKV_OFF // GDN_QKV_WIDTH
    z_blk = Z_OFF // (GDN_HEADS * GDN_DV)
    small_blk = SMALL_OFF // LANES
    in_specs = [pl.BlockSpec((nb, tb, GDN_QKV_WIDTH), lambda bi, i: (bi, i, qkv_blk))]
    args = [fat]
    if n_t > 1:
        per = tb // hl
        in_specs.append(pl.BlockSpec((nb, hl, GDN_QKV_WIDTH), lambda bi, i: (bi, jnp.maximum(i * per - 1, 0), qkv_blk)))
        args.append(fat)
    in_specs += [pl.BlockSpec((nb, hl, GDN_QKV_WIDTH), lambda bi, i: (bi, 0, 0)),
                 pl.BlockSpec((nb, tb, GDN_HEADS * GDN_DV), lambda bi, i: (bi, i, z_blk)),
                 pl.BlockSpec((nb, tb, LANES), lambda bi, i: (bi, i, small_blk)),
                 pl.BlockSpec((nb, GDN_HEADS, GDN_DK, GDN_DV), lambda bi, i: (bi, 0, 0, 0)),
                 pl.BlockSpec((GDN_CONV_WIDTH, GDN_QKV_WIDTH), lambda bi, i: (0, 0)),
                 pl.BlockSpec((1, LANES), lambda bi, i: (0, 0)),
                 pl.BlockSpec((1, LANES), lambda bi, i: (0, 0)),
                 pl.BlockSpec((1, GDN_DV), lambda bi, i: (0, 0))]
    args += [hist8, fat, fat, state, cw, alog_row, dt_row, ng]
    rows = nb * tb
    return pl.pallas_call(
        functools.partial(_gdn_kernel, n_t=n_t, chunk=chunk),
        grid=(bsz // nb, n_t),
        in_specs=in_specs,
        out_specs=[pl.BlockSpec((nb, tb, GDN_HEADS * GDN_DV), lambda bi, i: (bi, i, 0)),
                   pl.BlockSpec((nb, hl, GDN_QKV_WIDTH), lambda bi, i: (bi, 0, 0)),
                   pl.BlockSpec((nb, GDN_HEADS, GDN_DK, GDN_DV), lambda bi, i: (bi, 0, 0, 0))],
        out_shape=[jax.ShapeDtypeStruct((bsz, t, GDN_HEADS * GDN_DV), BF16),
                   jax.ShapeDtypeStruct((bsz, hl, GDN_QKV_WIDTH), F32),
                   jax.ShapeDtypeStruct((bsz, GDN_HEADS, GDN_DK, GDN_DV), F32)],
        scratch_shapes=[pltpu.VMEM((nb, hl + tb, GDN_QKV_WIDTH), F32),
                        pltpu.VMEM((nb, GDN_HEADS, GDN_DK, GDN_DV), F32),
                        pltpu.VMEM((rows, GDN_DV), F32),
                        pltpu.VMEM((rows, GDN_DV), F32)],
        compiler_params=_cparams(("arbitrary", "arbitrary")),
        name="gated_deltanet",
    )(*args)


def _mla_prep_kernel(*refs, emit_kv, scale):
    if emit_kv:
        (blk_ref, cs_ref, sn_ref, qg_ref, kg_ref, wq_ref, wqs_ref, wkv_ref,
         q_ref, lat_ref, kpe_ref, k_ref, v_ref) = refs
    else:
        (blk_ref, cs_ref, sn_ref, qg_ref, kg_ref, wq_ref, wqs_ref,
         q_ref, lat_ref, kpe_ref) = refs
    nb, tb, _ = blk_ref.shape
    rows = nb * tb
    blk = blk_ref[...]
    cs = cs_ref[...]
    sn = sn_ref[...]
    qn = _rms(blk[..., :MLA_Q_LORA], qg_ref[...]).reshape(rows, MLA_Q_LORA).astype(BF16)
    qf = _dot(qn, wq_ref[...]).reshape(nb, tb, MLA_HEADS * QK_PAD)
    qs = _dot(qn, wqs_ref[...]).reshape(nb, tb, MLA_HEADS * LANES)
    for h in range(MLA_HEADS):
        nope = qf[..., h * QK_PAD:h * QK_PAD + MLA_NOPE_DIM]
        rope = qf[..., h * QK_PAD + MLA_NOPE_DIM:(h + 1) * QK_PAD] * cs + qs[..., h * LANES:(h + 1) * LANES] * sn
        q_ref[h, :, :, 0:MLA_NOPE_DIM] = (nope * scale).astype(q_ref.dtype)
        q_ref[h, :, :, MLA_NOPE_DIM:QK_PAD] = (rope * scale).astype(q_ref.dtype)

    lat = _rms(blk[..., MLA_Q_LORA + LANES:], kg_ref[...])
    lat_ref[...] = lat

    small = blk[..., MLA_Q_LORA:MLA_Q_LORA + LANES].reshape(rows, LANES)
    half = MLA_ROPE_DIM // 2
    lane = lax.broadcasted_iota(jnp.int32, (rows, LANES), 1)
    swapped = jnp.where(lane < half, pltpu.roll(small, LANES - half, 1), pltpu.roll(small, half, 1))
    kpe = small.reshape(nb, tb, LANES) * cs + swapped.reshape(nb, tb, LANES) * sn
    kpe_ref[...] = kpe[..., :MLA_ROPE_DIM]

    if emit_kv:
        ones_col = jnp.where(lax.broadcasted_iota(jnp.int32, (nb, tb, V_EXT - MLA_V_DIM), 2) == 0,
                             1.0, 0.0).astype(v_ref.dtype)
        kv = _dot(lat.reshape(rows, MLA_KV_LORA).astype(BF16), wkv_ref[...]).reshape(nb, tb, MLA_HEADS * 2 * LANES)
        for h in range(MLA_HEADS):
            k_ref[h, :, :, 0:MLA_NOPE_DIM] = kv[..., h * 2 * LANES:h * 2 * LANES + MLA_NOPE_DIM].astype(k_ref.dtype)
            k_ref[h, :, :, MLA_NOPE_DIM:QK_PAD] = kpe.astype(k_ref.dtype)
            v_ref[h, :, :, 0:MLA_V_DIM] = kv[..., h * 2 * LANES + MLA_NOPE_DIM:(h + 1) * 2 * LANES].astype(v_ref.dtype)
            v_ref[h, :, :, MLA_V_DIM:V_EXT] = ones_col


def _mla_prep(fat, cs, sn, qg, kg, wq, wqs, wkv, *, nb, tb, emit_kv, scale):
    bsz, t, _ = fat.shape
    blk = QLAT_OFF // MLA_BLK_W
    hq = MLA_HEADS
    in_specs = [pl.BlockSpec((nb, tb, MLA_BLK_W), lambda bi, i: (bi, i, blk)),
                pl.BlockSpec((tb, LANES), lambda bi, i: (i, 0)),
                pl.BlockSpec((tb, LANES), lambda bi, i: (i, 0)),
                pl.BlockSpec((1, MLA_Q_LORA), lambda bi, i: (0, 0)),
                pl.BlockSpec((1, MLA_KV_LORA), lambda bi, i: (0, 0)),
                pl.BlockSpec(wq.shape, lambda bi, i: (0, 0)),
                pl.BlockSpec(wqs.shape, lambda bi, i: (0, 0))]
    args = [fat, cs, sn, qg, kg, wq, wqs]
    out_specs = [pl.BlockSpec((hq, nb, tb, QK_PAD), lambda bi, i: (0, bi, i, 0)),
                 pl.BlockSpec((nb, tb, MLA_KV_LORA), lambda bi, i: (bi, i, 0)),
                 pl.BlockSpec((nb, tb, MLA_ROPE_DIM), lambda bi, i: (bi, i, 0))]
    out_shape = [jax.ShapeDtypeStruct((hq, bsz, t, QK_PAD), BF16),
                 jax.ShapeDtypeStruct((bsz, t, MLA_KV_LORA), F32),
                 jax.ShapeDtypeStruct((bsz, t, MLA_ROPE_DIM), F32)]
    if emit_kv:
        in_specs.append(pl.BlockSpec(wkv.shape, lambda bi, i: (0, 0)))
        args.append(wkv)
        out_specs += [pl.BlockSpec((hq, nb, tb, QK_PAD), lambda bi, i: (0, bi, i, 0)),
                      pl.BlockSpec((hq, nb, tb, V_EXT), lambda bi, i: (0, bi, i, 0))]
        out_shape += [jax.ShapeDtypeStruct((hq, bsz, t, QK_PAD), BF16),
                      jax.ShapeDtypeStruct((hq, bsz, t, V_EXT), BF16)]
    return pl.pallas_call(
        functools.partial(_mla_prep_kernel, emit_kv=emit_kv, scale=scale),
        grid=(bsz // nb, t // tb),
        in_specs=in_specs, out_specs=out_specs, out_shape=out_shape,
        compiler_params=_cparams(("arbitrary", "arbitrary")),
        name="mla_prep",
    )(*args)


def _chunk_mask(q0, k0, tq, tk):
    qc = (q0 + lax.broadcasted_iota(jnp.int32, (tq, tk), 0)) // CHUNK
    kc = (k0 + lax.broadcasted_iota(jnp.int32, (tq, tk), 1)) // CHUNK
    return kc <= qc


def _attn_prompt_kernel(q_ref, k_ref, v_ref, o_ref, m_ref, acc_ref, *, tq, nsub):
    j = pl.program_id(2)
    m_ref[...] = jnp.full(m_ref.shape, -jnp.inf, F32)
    acc_ref[...] = jnp.zeros(acc_ref.shape, F32)

    def update(c, ki, masked):
        k0 = pl.multiple_of(ki * tq, tq)
        q = q_ref[0, 0, c * tq:(c + 1) * tq, :]
        s = _dot_nt(q, k_ref[0, 0, pl.ds(k0, tq), :])
        if masked:
            s = jnp.where(_chunk_mask(0, 0, tq, tq), s, -jnp.inf)
        m_old = m_ref[c]
        m_new = jnp.maximum(m_old, jnp.max(s, axis=-1, keepdims=True))
        p = jnp.exp2(s - m_new)
        acc_ref[c] = jnp.exp2(m_old - m_new) * acc_ref[c] + _dot(p.astype(BF16), v_ref[0, 0, pl.ds(k0, tq), :])
        m_ref[c] = m_new

    def body(ki, carry):
        for c in range(nsub):
            update(c, ki, False)
        return carry

    lax.fori_loop(0, nsub * j, body, 0)
    for d in range(nsub):
        for c in range(d, nsub):
            update(c, nsub * j + d, c == d)
    for c in range(nsub):
        acc = acc_ref[c]
        o_ref[0, c * tq:(c + 1) * tq, :] = (acc[:, 0:MLA_V_DIM] / acc[:, MLA_V_DIM:MLA_V_DIM + 1]).astype(o_ref.dtype)


def _attn_prompt(q, k, v, *, tq, nsub):
    hq, bsz, t, _ = q.shape
    tqs = tq * nsub
    return pl.pallas_call(
        functools.partial(_attn_prompt_kernel, tq=tq, nsub=nsub),
        grid=(bsz, hq, t // tqs),
        in_specs=[pl.BlockSpec((1, 1, tqs, QK_PAD), lambda b, h, i: (h, b, i, 0)),
                  pl.BlockSpec((1, 1, t, QK_PAD), lambda b, h, i: (h, b, 0, 0)),
                  pl.BlockSpec((1, 1, t, V_EXT), lambda b, h, i: (h, b, 0, 0))],
        out_specs=pl.BlockSpec((1, tqs, MLA_V_DIM), lambda b, h, i: (b, i, h)),
        out_shape=jax.ShapeDtypeStruct((bsz, t, MLA_HEADS * MLA_V_DIM), BF16),
        scratch_shapes=[pltpu.VMEM((nsub, tq, 1), F32), pltpu.VMEM((nsub, tq, V_EXT), F32)],
        compiler_params=_cparams(("arbitrary", "arbitrary", "arbitrary")),
        name="attn_prompt",
    )(q, k, v)


def _attn_sample_kernel(q_ref, latc_ref, kpec_ref, latn_ref, kpen_ref, wuk_ref, wuv_ref, o_ref):
    tn = latn_ref.shape[1]
    past = latc_ref.shape[1]
    hq = MLA_HEADS
    qabs = jnp.concatenate(
        [_dot(q_ref[h, 0, :, 0:MLA_NOPE_DIM], wuk_ref[h]) for h in range(hq)], axis=0).astype(BF16)
    qrope = jnp.concatenate([q_ref[h, 0, :, MLA_NOPE_DIM:MLA_NOPE_DIM + MLA_ROPE_DIM] for h in range(hq)], axis=0)
    latc = latc_ref[0].astype(BF16)
    latn = latn_ref[0].astype(BF16)
    sc = _dot_nt(qabs, latc) + _dot_nt(qrope, kpec_ref[0].astype(BF16))
    sn = _dot_nt(qabs, latn) + _dot_nt(qrope, kpen_ref[0].astype(BF16))
    rows = hq * tn
    qpos = past + lax.broadcasted_iota(jnp.int32, (rows, 1), 0) % tn
    kc_pos = lax.broadcasted_iota(jnp.int32, (rows, past), 1)
    kn_pos = past + lax.broadcasted_iota(jnp.int32, (rows, tn), 1)
    sc = jnp.where(kc_pos // CHUNK <= qpos // CHUNK, sc, -jnp.inf)
    sn = jnp.where(kn_pos // CHUNK <= qpos // CHUNK, sn, -jnp.inf)
    m = jnp.maximum(jnp.max(sc, axis=-1, keepdims=True), jnp.max(sn, axis=-1, keepdims=True))
    pc = jnp.exp(sc - m)
    pn = jnp.exp(sn - m)
    l = jnp.sum(pc, axis=-1, keepdims=True) + jnp.sum(pn, axis=-1, keepdims=True)
    olat = (_dot(pc.astype(BF16), latc) + _dot(pn.astype(BF16), latn)) / l
    for h in range(hq):
        oh = _dot(olat[h * tn:(h + 1) * tn].astype(BF16), wuv_ref[h])
        o_ref[0, :, h * MLA_V_DIM:(h + 1) * MLA_V_DIM] = oh.astype(o_ref.dtype)


def _attn_sample(q, cache_lat, cache_kpe, lat_new, kpe_new, wuk_t, wuv):
    hq, bsz, tn, _ = q.shape
    past = cache_lat.shape[1]
    return pl.pallas_call(
        _attn_sample_kernel,
        grid=(bsz,),
        in_specs=[pl.BlockSpec((hq, 1, tn, QK_PAD), lambda b: (0, b, 0, 0)),
                  pl.BlockSpec((1, past, MLA_KV_LORA), lambda b: (b, 0, 0)),
                  pl.BlockSpec((1, past, MLA_ROPE_DIM), lambda b: (b, 0, 0)),
                  pl.BlockSpec((1, tn, MLA_KV_LORA), lambda b: (b, 0, 0)),
                  pl.BlockSpec((1, tn, MLA_ROPE_DIM), lambda b: (b, 0, 0)),
                  pl.BlockSpec(wuk_t.shape, lambda b: (0, 0, 0)),
                  pl.BlockSpec(wuv.shape, lambda b: (0, 0, 0))],
        out_specs=pl.BlockSpec((1, tn, MLA_HEADS * MLA_V_DIM), lambda b: (b, 0, 0)),
        out_shape=jax.ShapeDtypeStruct((bsz, tn, MLA_HEADS * MLA_V_DIM), BF16),
        compiler_params=_cparams(("arbitrary",)),
        name="attn_sample",
    )(q, cache_lat, cache_kpe, lat_new, kpe_new, wuk_t, wuv)


def _merge_kernel(ya_ref, yb_ref, yc_ref, gate_ref, x_ref, g1_ref, g0_ref, b0_ref, wb_ref, wo_ref,
                  lg_ref, lb_ref, o_ref, *, ln0, alpha):
    nb, tb, d = x_ref.shape
    rows = nb * tb
    x = x_ref[...]
    if ln0:
        x = _ln(x, g0_ref[...], b0_ref[...])
    merged = None
    for i, y_ref in enumerate((ya_ref, yb_ref, yc_ref)):
        proj = _dot(y_ref[...].reshape(rows, BRANCH_WIDTH), wb_ref[i])
        gate = _sigmoid(gate_ref[:, :, i * d:(i + 1) * d]).reshape(rows, d)
        merged = gate * proj if merged is None else merged + gate * proj
    out = _dot(merged.astype(BF16), wo_ref[...]).reshape(nb, tb, d)
    o_ref[...] = _ln(alpha * x + (1.0 + g1_ref[...]) * out, lg_ref[...], lb_ref[...])


def _merge(ya, yb, yc, fat, x, g1, g0, b0, wb, wo, lg, lb, *, nb, tb, ln0, alpha):
    bsz, t, d = x.shape
    row = lambda bi, i: (bi, i, 0)
    const2 = lambda bi, i: (0, 0)
    return pl.pallas_call(
        functools.partial(_merge_kernel, ln0=ln0, alpha=alpha),
        grid=(bsz // nb, t // tb),
        in_specs=[pl.BlockSpec((nb, tb, BRANCH_WIDTH), row),
                  pl.BlockSpec((nb, tb, BRANCH_WIDTH), row),
                  pl.BlockSpec((nb, tb, BRANCH_WIDTH), row),
                  pl.BlockSpec((nb, tb, GATE_W), row),
                  pl.BlockSpec((nb, tb, d), row),
                  pl.BlockSpec((nb, 1, d), lambda bi, i: (bi, 0, 0)),
                  pl.BlockSpec((1, d), const2),
                  pl.BlockSpec((1, d), const2),
                  pl.BlockSpec(wb.shape, lambda bi, i: (0, 0, 0)),
                  pl.BlockSpec(wo.shape, const2),
                  pl.BlockSpec((1, d), const2),
                  pl.BlockSpec((1, d), const2)],
        out_specs=pl.BlockSpec((nb, tb, d), row),
        out_shape=jax.ShapeDtypeStruct((bsz, t, d), F32),
        compiler_params=_cparams(("arbitrary", "arbitrary")),
        name="merge",
    )(ya, yb, yc, fat, x, g1, g0, b0, wb, wo, lg, lb)


FFN_BLOCK = 1408


def _ffn_kernel(x_ref, sc_ref, sh_ref, g2_ref, wa_ref, wv_ref, wd_ref, cw_ref, cb_ref, lg_ref, lb_ref, hist_ref,
                o_ref, ho_ref, h_ref, acc_ref, carry_ref, ap_ref, *, alpha):
    nb, tb, d = x_ref.shape
    rows = nb * tb
    t = pl.program_id(1)
    f = pl.program_id(2)
    n_f = pl.num_programs(2)
    hl = SUBLANES
    fb = wa_ref.shape[1]

    @pl.when(f == 0)
    def _():
        h = x_ref[...] * (1.0 + sc_ref[...]) + sh_ref[...]
        h_ref[...] = h.reshape(rows, d).astype(BF16)

    @pl.when(t == 0)
    def _():
        carry_ref[f] = hist_ref[...]

    h = h_ref[...]
    a = _dot(h, wa_ref[...]).reshape(nb, tb, fb)
    v = _dot(h, wv_ref[...]).reshape(nb, tb, fb)
    ap_ref[:, 0:hl, :] = carry_ref[f]
    ap_ref[:, hl:hl + tb, :] = a
    off = hl - (FFN_CONV_WIDTH - 1)
    conv = cw_ref[FFN_CONV_WIDTH - 1:FFN_CONV_WIDTH, :] * a
    for k in range(FFN_CONV_WIDTH - 1):
        conv = conv + cw_ref[k:k + 1, :] * ap_ref[:, off + k:off + k + tb, :]
    act = _silu(conv + cb_ref[...]) * v
    contrib = _dot(act.reshape(rows, fb).astype(BF16), wd_ref[...])
    last = ap_ref[:, tb:tb + hl, :]
    carry_ref[f] = last
    ho_ref[0] = last

    @pl.when(f == 0)
    def _():
        acc_ref[...] = contrib

    @pl.when(f > 0)
    def _():
        acc_ref[...] = acc_ref[...] + contrib

    @pl.when(f == n_f - 1)
    def _():
        y = acc_ref[...].reshape(nb, tb, d)
        o_ref[...] = _ln(alpha * x_ref[...] + (1.0 + g2_ref[...]) * y, lg_ref[...], lb_ref[...])


def _ffn(x, sc, sh, g2, wa, wv, wd, cw, cb, lg, lb, hist8, *, nb, tb, alpha):
    bsz, t, d = x.shape
    dff = wa.shape[1]
    fb = FFN_BLOCK
    n_f = dff // fb
    hl = SUBLANES
    rows = nb * tb
    row = lambda bi, i, f: (bi, i, 0)
    mod = lambda bi, i, f: (bi, 0, 0)
    const2 = lambda bi, i, f: (0, 0)
    return pl.pallas_call(
        functools.partial(_ffn_kernel, alpha=alpha),
        grid=(bsz // nb, t // tb, n_f),
        in_specs=[pl.BlockSpec((nb, tb, d), row),
                  pl.BlockSpec((nb, 1, d), mod),
                  pl.BlockSpec((nb, 1, d), mod),
                  pl.BlockSpec((nb, 1, d), mod),
                  pl.BlockSpec((d, fb), lambda bi, i, f: (0, f)),
                  pl.BlockSpec((d, fb), lambda bi, i, f: (0, f)),
                  pl.BlockSpec((fb, d), lambda bi, i, f: (f, 0)),
                  pl.BlockSpec((FFN_CONV_WIDTH, fb), lambda bi, i, f: (0, f)),
                  pl.BlockSpec((1, fb), lambda bi, i, f: (0, f)),
                  pl.BlockSpec((1, d), const2),
                  pl.BlockSpec((1, d), const2),
                  pl.BlockSpec((nb, hl, fb), lambda bi, i, f: (bi, 0, f))],
        out_specs=[pl.BlockSpec((nb, tb, d), row),
                   pl.BlockSpec((1, nb, hl, fb), lambda bi, i, f: (i, bi, 0, f))],
        out_shape=[jax.ShapeDtypeStruct((bsz, t, d), F32),
                   jax.ShapeDtypeStruct((t // tb, bsz, hl, dff), F32)],
        scratch_shapes=[pltpu.VMEM((rows, d), BF16),
                        pltpu.VMEM((rows, d), F32),
                        pltpu.VMEM((n_f, nb, hl, fb), F32),
                        pltpu.VMEM((nb, hl + tb, fb), F32)],
        compiler_params=_cparams(("arbitrary", "arbitrary", "arbitrary")),
        name="conv_ffn",
    )(x, sc, sh, g2, wa, wv, wd, cw, cb, lg, lb, hist8)


def _front_pad(hist, rows):
    b, r, c = hist.shape
    return jnp.concatenate([jnp.zeros((b, rows - r, c), hist.dtype), hist], axis=1)


def _rope_rows(pos):
    half = MLA_ROPE_DIM // 2
    inv_freq = ROPE_THETA ** (-jnp.arange(half, dtype=F32) / half)
    ang = pos.astype(F32)[:, None] * inv_freq[None, :]
    cos, sin = jnp.cos(ang), jnp.sin(ang)
    pad = jnp.zeros((pos.shape[0], LANES - MLA_ROPE_DIM), F32)
    return jnp.concatenate([cos, cos, pad], axis=1), jnp.concatenate([-sin, sin, pad], axis=1)


def _lane_row(vals, lane0):
    return jnp.zeros((1, LANES), F32).at[0, lane0:lane0 + vals.shape[0]].set(vals.astype(F32))


def _layer_weights(l, w_in, mla_w_uq, mla_w_ukv, w_branch, w_out, w_up, w_down):
    d = D_MODEL
    o = np.cumsum([0, 2 * CONV_A_CH, GDN_QKV_WIDTH, GDN_HEADS * GDN_DV, GDN_HEADS, GDN_HEADS,
                   MLA_Q_LORA, MLA_KV_LORA, MLA_ROPE_DIM, N_BRANCH * D_MODEL]).tolist()
    w = w_in[l]
    seg = lambda i: w[:, o[i]:o[i + 1]]
    pad = jnp.zeros((d, LANES - MLA_ROPE_DIM - 2 * GDN_HEADS), w.dtype)
    w_fat = jnp.concatenate([seg(8), seg(1), seg(2), seg(0), seg(5), seg(7), seg(3), seg(4), pad, seg(6)],
                            axis=1).astype(BF16)
    hd = MLA_NOPE_DIM + MLA_ROPE_DIM
    half = MLA_ROPE_DIM // 2
    wq = mla_w_uq[l]
    zq = jnp.zeros((MLA_Q_LORA, QK_PAD - hd), wq.dtype)
    zs = jnp.zeros((MLA_Q_LORA, LANES - MLA_ROPE_DIM), wq.dtype)
    wq_cols, wqs_cols = [], []
    for h in range(MLA_HEADS):
        wq_cols += [wq[:, h * hd:(h + 1) * hd], zq]
        r0 = h * hd + MLA_NOPE_DIM
        wqs_cols += [wq[:, r0 + half:r0 + 2 * half], wq[:, r0:r0 + half], zs]
    wkv = mla_w_ukv[l]
    kvw = MLA_NOPE_DIM + MLA_V_DIM
    wuk_t = jnp.stack([wkv[:, h * kvw:h * kvw + MLA_NOPE_DIM].T for h in range(MLA_HEADS)]).astype(BF16)
    wuv = jnp.stack([wkv[:, h * kvw + MLA_NOPE_DIM:(h + 1) * kvw] for h in range(MLA_HEADS)]).astype(BF16)
    return dict(
        w_fat=w_fat,
        wq=jnp.concatenate(wq_cols, axis=1).astype(BF16),
        wqs=jnp.concatenate(wqs_cols, axis=1).astype(BF16),
        wkv=wkv.astype(BF16), wuk_t=wuk_t, wuv=wuv,
        wb=w_branch[l].astype(BF16), wo=w_out[l].astype(BF16),
        wa=w_up[l][:, :D_FF].astype(BF16), wv=w_up[l][:, D_FF:].astype(BF16), wd=w_down[l].astype(BF16),
    )


def _run_layer(x, mods, st, wts, p, *, ln0, g0, b0, alpha, cs, sn, tiles, prompt):
    bsz, t, d = x.shape
    sh1, sc1, g1, sh2, sc2, g2 = mods
    nb, tb, gdn_nb, gdn_tb = tiles
    if st is None:
        hist_a = jnp.zeros((bsz, CONV_A_WIDTH - 1, CONV_A_CH), F32)
        hist_b = jnp.zeros((bsz, GDN_CONV_WIDTH - 1, GDN_QKV_WIDTH), F32)
        state_b = jnp.zeros((bsz, GDN_HEADS, GDN_DK, GDN_DV), F32)
        hist_f = jnp.zeros((bsz, FFN_CONV_WIDTH - 1, D_FF), F32)
    else:
        cache_lat, cache_kpe, hist_a, hist_b, state_b, hist_f = st
    row = lambda v: v.reshape(1, -1)

    fat = _inproj(x, sc1, sh1, g0, b0, wts['w_fat'], nb=nb, tb=tb, ln0=ln0)
    y_a, ha = _conv_module(fat, _front_pad(hist_a, CONV_A_HALO), p['conv_a_w'], row(p['conv_a_b']),
                           row(p['ln_a_g']), row(p['ln_a_b']), nb=nb, tb=tb)
    y_b, hb, sb = _gated_deltanet(fat, _front_pad(hist_b, SUBLANES), state_b, p['gdn_conv_w'],
                                  _lane_row(p['gdn_a_log'], DEC_LANE), _lane_row(p['gdn_dt_bias'], DEC_LANE),
                                  row(p['gdn_norm_g']), nb=gdn_nb, tb=gdn_tb)
    if prompt:
        q, lat, kpe, k, v = _mla_prep(fat, cs, sn, row(p['mla_q_norm_g']), row(p['mla_kv_norm_g']),
                                      wts['wq'], wts['wqs'], wts['wkv'], nb=nb, tb=tb, emit_kv=True,
                                      scale=ATTN_SCALE * math.log2(math.e))
        nsub = 2 if t % (2 * tb) == 0 else 1
        y_c = _attn_prompt(q, k, v, tq=tb, nsub=nsub)
    else:
        q, lat, kpe = _mla_prep(fat, cs, sn, row(p['mla_q_norm_g']), row(p['mla_kv_norm_g']),
                                wts['wq'], wts['wqs'], None, nb=nb, tb=tb, emit_kv=False, scale=ATTN_SCALE)
        y_c = _attn_sample(q, cache_lat, cache_kpe, lat, kpe, wts['wuk_t'], wts['wuv'])
    x1 = _merge(y_a, y_b, y_c, fat, x, g1, g0, b0, wts['wb'], wts['wo'], row(p['ln1_g']), row(p['ln1_b']),
                nb=nb, tb=tb, ln0=ln0, alpha=alpha)
    x2, hf = _ffn(x1, sc2, sh2, g2, wts['wa'], wts['wv'], wts['wd'], p['ffn_conv_w'], row(p['ffn_conv_b']),
                  row(p['ln2_g']), row(p['ln2_b']), _front_pad(hist_f, SUBLANES), nb=nb, tb=tb, alpha=alpha)
    states = (lat, kpe, ha[:, CONV_A_HALO - (CONV_A_WIDTH - 1):], hb[:, SUBLANES - (GDN_CONV_WIDTH - 1):],
              sb, hf[-1, :, SUBLANES - (FFN_CONV_WIDTH - 1):])
    return x2, states


def kernel(x_prompt, x_sample, cache_mla_latent, cache_mla_kpe, state_conv_a, state_gdn_conv, state_gdn, state_ffn_conv, c_prompt, c_sample, ln0_g, ln0_b, w_ada, b_ada, w_in, conv_a_w, conv_a_b, ln_a_g, ln_a_b, gdn_conv_w, gdn_a_log, gdn_dt_bias, gdn_norm_g, mla_q_norm_g, mla_kv_norm_g, mla_w_uq, mla_w_ukv, w_branch, w_out, ln1_g, ln1_b, w_up, ffn_conv_w, ffn_conv_b, w_down, ln2_g, ln2_b):
    depth = w_ada.shape[0]
    bp, seq, d = x_prompt.shape
    bs, dec_seq, _ = x_sample.shape
    past = cache_mla_latent.shape[2]
    alpha = (2 * depth) ** 0.25

    n_c = bp + bs
    c_rows = -(-n_c // SUBLANES) * SUBLANES
    c_all = jnp.concatenate([c_prompt, c_sample, jnp.zeros((c_rows - n_c, d), F32)], axis=0)
    mod = _modulation(c_all, w_ada, b_ada)

    cs_p, sn_p = _rope_rows(jnp.arange(seq, dtype=jnp.int32))
    cs_s, sn_s = _rope_rows(past + jnp.arange(dec_seq, dtype=jnp.int32))
    g0, b0 = ln0_g.reshape(1, d), ln0_b.reshape(1, d)

    tile_p = min(512, seq)
    tiles_p = (1, tile_p, 1, min(GDN_ROWS, seq))
    tiles_s = (bs, dec_seq, GDN_ROWS // dec_seq, dec_seq)

    xp, xs = x_prompt, x_sample
    p_states, s_states = [], []
    for l in range(depth):
        p = {'conv_a_w': conv_a_w[l], 'conv_a_b': conv_a_b[l], 'ln_a_g': ln_a_g[l], 'ln_a_b': ln_a_b[l],
             'gdn_conv_w': gdn_conv_w[l], 'gdn_a_log': gdn_a_log[l], 'gdn_dt_bias': gdn_dt_bias[l],
             'gdn_norm_g': gdn_norm_g[l], 'mla_q_norm_g': mla_q_norm_g[l], 'mla_kv_norm_g': mla_kv_norm_g[l],
             'ln1_g': ln1_g[l], 'ln1_b': ln1_b[l], 'ffn_conv_w': ffn_conv_w[l], 'ffn_conv_b': ffn_conv_b[l],
             'ln2_g': ln2_g[l], 'ln2_b': ln2_b[l]}
        wts = _layer_weights(l, w_in, mla_w_uq, mla_w_ukv, w_branch, w_out, w_up, w_down)
        mods_p = [m.reshape(bp, 1, d) for m in jnp.split(mod[l, :bp], 6, axis=-1)]
        mods_s = [m.reshape(bs, 1, d) for m in jnp.split(mod[l, bp:n_c], 6, axis=-1)]
        xp, st_p = _run_layer(xp, mods_p, None, wts, p, ln0=(l == 0), g0=g0, b0=b0, alpha=alpha,
                              cs=cs_p, sn=sn_p, tiles=tiles_p, prompt=True)
        st_in = (cache_mla_latent[l], cache_mla_kpe[l], state_conv_a[l], state_gdn_conv[l], state_gdn[l],
                 state_ffn_conv[l])
        xs, st_s = _run_layer(xs, mods_s, st_in, wts, p, ln0=(l == 0), g0=g0, b0=b0, alpha=alpha,
                              cs=cs_s, sn=sn_s, tiles=tiles_s, prompt=False)
        p_states.append(st_p)
        s_states.append(st_s)
    p_out = [jnp.stack(z, axis=0) for z in zip(*p_states)]
    s_out = [jnp.stack(z, axis=0) for z in zip(*s_states)]
    return (xp, xs, *p_out, *s_out)
```

```python
import functools
import math

import jax
import jax.numpy as jnp
import numpy as np
from jax import lax
from jax.experimental import pallas as pl
from jax.experimental.pallas import tpu as pltpu

F32 = jnp.float32
BF16 = jnp.bfloat16

D_MODEL = 1024
CHUNK = 64
CONV_A_CH = 512
CONV_A_WIDTH = 31
GDN_HEADS = 4
GDN_DK = 128
GDN_DV = 128
GDN_CONV_WIDTH = 4
GDN_QKV_WIDTH = 2 * GDN_HEADS * GDN_DK + GDN_HEADS * GDN_DV
MLA_HEADS = 4
MLA_Q_LORA = 384
MLA_KV_LORA = 256
MLA_NOPE_DIM = 128
MLA_ROPE_DIM = 64
MLA_V_DIM = 128
ROPE_THETA = 10000.0
N_BRANCH = 3
BRANCH_WIDTH = 512
D_FF = 2816
FFN_CONV_WIDTH = 3
LN_EPS = 1e-5
RMS_EPS = 1e-6

LANES = 128
SUBLANES = 8
VMEM_LIMIT = 48 * 1024 * 1024

GATE_OFF, GATE_W = 0, N_BRANCH * D_MODEL
QKV_OFF = GATE_OFF + GATE_W
Z_OFF = QKV_OFF + GDN_QKV_WIDTH
PREA_OFF = Z_OFF + GDN_HEADS * GDN_DV
QLAT_OFF = PREA_OFF + 2 * CONV_A_CH
SMALL_OFF = QLAT_OFF + MLA_Q_LORA
KVLAT_OFF = SMALL_OFF + LANES
FAT_W = KVLAT_OFF + MLA_KV_LORA
MLA_BLK_W = FAT_W - QLAT_OFF
BETA_LANE = MLA_ROPE_DIM
DEC_LANE = MLA_ROPE_DIM + GDN_HEADS
QK_PAD = 2 * LANES
ATTN_KB = 256
ATTN_CHAINS = 8
VT_ROWS = MLA_V_DIM + 16
ATTN_SCALE = (MLA_NOPE_DIM + MLA_ROPE_DIM) ** -0.5


def _cparams(sem):
    return pltpu.CompilerParams(dimension_semantics=sem, vmem_limit_bytes=VMEM_LIMIT)


def _sigmoid(x):
    return jax.nn.sigmoid(x)


def _silu(x):
    return x * jax.nn.sigmoid(x)


def _ln(x, g, b):
    mu = jnp.mean(x, axis=-1, keepdims=True)
    xc = x - mu
    var = jnp.mean(xc * xc, axis=-1, keepdims=True)
    return xc * lax.rsqrt(var + LN_EPS) * g + b


def _rms(x, g):
    return x * lax.rsqrt(jnp.mean(x * x, axis=-1, keepdims=True) + RMS_EPS) * g


def _dot(a, b):
    return jnp.dot(a, b, preferred_element_type=F32)


def _dot_nt(a, b):
    return lax.dot_general(a, b, (((1,), (1,)), ((), ())), preferred_element_type=F32)


def _dot_tn(a, b):
    return lax.dot_general(a, b, (((0,), (0,)), ((), ())), preferred_element_type=F32)


def _dot_split(a, b):
    ah = a.astype(BF16)
    al = (a - ah.astype(F32)).astype(BF16)
    bh = b.astype(BF16)
    bl = (b - bh.astype(F32)).astype(BF16)
    return _dot(ah, bh) + _dot(ah, bl) + _dot(al, bh)


def _split3(x):
    hi = x.astype(BF16)
    r = x - hi.astype(F32)
    mid = r.astype(BF16)
    lo = (r - mid.astype(F32)).astype(BF16)
    return hi, mid, lo


def _mod_kernel(c_ref, w_ref, b_ref, o_ref):
    s = _silu(c_ref[...])
    o_ref[0] = _dot(s.astype(BF16), w_ref[0].astype(BF16)) + b_ref[0]


def _modulation(c_all, w_ada, b_ada):
    depth, d, n = w_ada.shape
    rows = c_all.shape[0]
    tn = 1536
    return pl.pallas_call(
        _mod_kernel,
        grid=(depth, n // tn),
        in_specs=[pl.BlockSpec((rows, d), lambda l, j: (0, 0)),
                  pl.BlockSpec((1, d, tn), lambda l, j: (l, 0, j)),
                  pl.BlockSpec((1, 1, tn), lambda l, j: (l, 0, j))],
        out_specs=pl.BlockSpec((1, rows, tn), lambda l, j: (l, 0, j)),
        out_shape=jax.ShapeDtypeStruct((depth, rows, n), F32),
        compiler_params=_cparams(("arbitrary", "arbitrary")),
        name="modulation",
    )(c_all, w_ada, b_ada.reshape(depth, 1, n))


def _inproj_kernel(x_ref, sc_ref, sh_ref, g0_ref, b0_ref, w_ref, o_ref, *, ln0):
    nb, tb, d = x_ref.shape
    x = x_ref[...]
    if ln0:
        x = _ln(x, g0_ref[...], b0_ref[...])
    h = x * (1.0 + sc_ref[...]) + sh_ref[...]
    o = _dot(h.reshape(nb * tb, d).astype(BF16), w_ref[...])
    o_ref[...] = o.reshape(nb, tb, o.shape[-1])


def _inproj(x, sc, sh, g0, b0, w_fat, *, nb, tb, ln0):
    bsz, t, d = x.shape
    tn = FAT_W // 3
    return pl.pallas_call(
        functools.partial(_inproj_kernel, ln0=ln0),
        grid=(FAT_W // tn, bsz // nb, t // tb),
        in_specs=[pl.BlockSpec((nb, tb, d), lambda n, b, i: (b, i, 0)),
                  pl.BlockSpec((nb, 1, d), lambda n, b, i: (b, 0, 0)),
                  pl.BlockSpec((nb, 1, d), lambda n, b, i: (b, 0, 0)),
                  pl.BlockSpec((1, d), lambda n, b, i: (0, 0)),
                  pl.BlockSpec((1, d), lambda n, b, i: (0, 0)),
                  pl.BlockSpec((d, tn), lambda n, b, i: (0, n))],
        out_specs=pl.BlockSpec((nb, tb, tn), lambda n, b, i: (b, i, n)),
        out_shape=jax.ShapeDtypeStruct((bsz, t, FAT_W), F32),
        compiler_params=_cparams(("arbitrary", "arbitrary", "arbitrary")),
        name="inproj",
    )(x, sc, sh, g0, b0, w_fat)


CONV_A_HALO = 32


def _glu(pre):
    return pre[..., :CONV_A_CH] * _sigmoid(pre[..., CONV_A_CH:])


def _conva_kernel(*refs, n_t, nbc, rc):
    if n_t > 1:
        pre_ref, halo_ref, hist_ref, w_ref, cb_ref, g_ref, b_ref, y_ref, ho_ref, xp_ref = refs
    else:
        pre_ref, hist_ref, w_ref, cb_ref, g_ref, b_ref, y_ref, ho_ref, xp_ref = refs
    nb, tb, _ = pre_ref.shape
    t = pl.program_id(1)
    hl = CONV_A_HALO

    @pl.when(t == 0)
    def _():
        xp_ref[:, 0:hl, :] = hist_ref[...]

    if n_t > 1:
        @pl.when(t > 0)
        def _():
            xp_ref[:, 0:hl, :] = _glu(halo_ref[...])

    xp_ref[:, hl:hl + tb, :] = _glu(pre_ref[...])
    off = hl - (CONV_A_WIDTH - 1)
    for b0 in range(0, nb, nbc):
        for r0 in range(0, tb, rc):
            acc = w_ref[0:1, :] * xp_ref[b0:b0 + nbc, r0 + off:r0 + off + rc, :]
            for k in range(1, CONV_A_WIDTH):
                acc = acc + w_ref[k:k + 1, :] * xp_ref[b0:b0 + nbc, r0 + off + k:r0 + off + k + rc, :]
            y = _silu(_ln(acc + cb_ref[...], g_ref[...], b_ref[...]))
            y_ref[b0:b0 + nbc, r0:r0 + rc, :] = y.astype(y_ref.dtype)
    ho_ref[...] = xp_ref[:, tb:tb + hl, :]


def _conv_module(fat, hist32, w, cb, g, b, *, nb, tb):
    bsz, t, _ = fat.shape
    n_t = t // tb
    pre_blk = PREA_OFF // (2 * CONV_A_CH)
    hl = CONV_A_HALO
    in_specs = [pl.BlockSpec((nb, tb, 2 * CONV_A_CH), lambda bi, i: (bi, i, pre_blk))]
    args = [fat]
    if n_t > 1:
        per = tb // hl
        in_specs.append(pl.BlockSpec((nb, hl, 2 * CONV_A_CH),
                                     lambda bi, i: (bi, jnp.maximum(i * per - 1, 0), pre_blk)))
        args.append(fat)
    in_specs += [pl.BlockSpec((nb, hl, CONV_A_CH), lambda bi, i: (bi, 0, 0)),
                 pl.BlockSpec((CONV_A_WIDTH, CONV_A_CH), lambda bi, i: (0, 0)),
                 pl.BlockSpec((1, CONV_A_CH), lambda bi, i: (0, 0)),
                 pl.BlockSpec((1, CONV_A_CH), lambda bi, i: (0, 0)),
                 pl.BlockSpec((1, CONV_A_CH), lambda bi, i: (0, 0))]
    args += [hist32, w, cb, g, b]
    if n_t > 1:
        nbc, rc = 1, 64
    else:
        nbc, rc = 4, tb
    return pl.pallas_call(
        functools.partial(_conva_kernel, n_t=n_t, nbc=nbc, rc=rc),
        grid=(bsz // nb, n_t),
        in_specs=in_specs,
        out_specs=[pl.BlockSpec((nb, tb, CONV_A_CH), lambda bi, i: (bi, i, 0)),
                   pl.BlockSpec((nb, hl, CONV_A_CH), lambda bi, i: (bi, 0, 0))],
        out_shape=[jax.ShapeDtypeStruct((bsz, t, CONV_A_CH), BF16),
                   jax.ShapeDtypeStruct((bsz, hl, CONV_A_CH), F32)],
        scratch_shapes=[pltpu.VMEM((nb, hl + tb, CONV_A_CH), F32)],
        compiler_params=_cparams(("arbitrary", "arbitrary")),
        name="conv_module",
    )(*args)


GDN_ROWS = 256


def _gdn_kernel(*refs, n_t, chunk):
    if n_t > 1:
        (qkv_ref, halo_ref, hist_ref, z_ref, small_ref, st_ref, cw_ref, alog_ref, dt_ref, ng_ref,
         y_ref, ho_ref, so_ref, xp_ref, s_ref, vn_ref, os_ref) = refs
    else:
        (qkv_ref, hist_ref, z_ref, small_ref, st_ref, cw_ref, alog_ref, dt_ref, ng_ref,
         y_ref, ho_ref, so_ref, xp_ref, s_ref, vn_ref, os_ref) = refs
    nb, tb, _ = qkv_ref.shape
    rows = nb * tb
    t = pl.program_id(1)
    hl = SUBLANES
    nk = GDN_HEADS * GDN_DK

    @pl.when(t == 0)
    def _():
        xp_ref[:, 0:hl, :] = hist_ref[...]
        s_ref[...] = st_ref[...]

    if n_t > 1:
        @pl.when(t > 0)
        def _():
            xp_ref[:, 0:hl, :] = halo_ref[...]

    xp_ref[:, hl:hl + tb, :] = qkv_ref[...]
    off = hl - (GDN_CONV_WIDTH - 1)
    acc = cw_ref[0:1, :] * xp_ref[:, off:off + tb, :]
    for k in range(1, GDN_CONV_WIDTH):
        acc = acc + cw_ref[k:k + 1, :] * xp_ref[:, off + k:off + k + tb, :]
    c = _silu(acc).reshape(rows, GDN_QKV_WIDTH)
    ho_ref[...] = xp_ref[:, tb:tb + hl, :]

    sm = small_ref[...].reshape(rows, LANES)
    beta_all = _sigmoid(sm)
    xg = sm + dt_ref[...]
    softplus = jnp.maximum(xg, 0.0) + jnp.log1p(jnp.exp(-jnp.abs(xg)))
    g_all = -jnp.exp(alog_ref[...]) * softplus

    shift = int(math.log2(chunk))
    ri = lax.broadcasted_iota(jnp.int32, (rows, rows), 0)
    ci = lax.broadcasted_iota(jnp.int32, (rows, rows), 1)
    same = (ri >> shift) == (ci >> shift)
    tri = same & (ci <= ri)
    strict = same & (ci < ri)
    tri_b = jnp.where(tri, 1.0, 0.0).astype(BF16)
    triu_b = jnp.where(same & (ri <= ci), 1.0, 0.0).astype(BF16)
    same_b = jnp.where(same, 1.0, 0.0).astype(BF16)
    g3 = _split3(g_all)
    gc_col = _dot(tri_b, g3[0]) + _dot(tri_b, g3[1]) + _dot(tri_b, g3[2])
    gc_row = _dot_tn(g3[0], triu_b) + _dot_tn(g3[1], triu_b) + _dot_tn(g3[2], triu_b)
    gl_col = _dot(same_b, g3[0]) + _dot(same_b, g3[1]) + _dot(same_b, g3[2])

    z = z_ref[...].reshape(rows, GDN_HEADS * GDN_DV)
    n_fac = shift
    for h in range(GDN_HEADS):
        qh = c[:, h * GDN_DK:(h + 1) * GDN_DK]
        kh = c[:, nk + h * GDN_DK:nk + (h + 1) * GDN_DK]
        vh = c[:, 2 * nk + h * GDN_DV:2 * nk + (h + 1) * GDN_DV]
        qh = qh * lax.rsqrt(jnp.sum(qh * qh, axis=-1, keepdims=True) + RMS_EPS) * (GDN_DK ** -0.5)
        kh = kh * lax.rsqrt(jnp.sum(kh * kh, axis=-1, keepdims=True) + RMS_EPS)
        beta = beta_all[:, BETA_LANE + h:BETA_LANE + h + 1]
        gc = gc_col[:, DEC_LANE + h:DEC_LANE + h + 1]
        gr = gc_row[DEC_LANE + h:DEC_LANE + h + 1, :]
        gl = gl_col[:, DEC_LANE + h:DEC_LANE + h + 1]
        decay = jnp.where(tri, jnp.exp(jnp.where(tri, gc - gr, 0.0)), 0.0)
        kb = kh.astype(BF16)
        kk = _dot_nt(kb, kb)
        a = jnp.where(strict, -(beta * kk * decay), 0.0)
        eg = jnp.exp(gc)
        rhs = jnp.concatenate([vh * beta, kh * (beta * eg)], axis=1)
        a0 = a
        m = a
        for _ in range(1, n_fac):
            ab = a.astype(BF16)
            a = _dot(ab, ab)
            m = m + a + _dot(m.astype(BF16), a.astype(BF16))
        mb = m.astype(BF16)
        x0 = rhs + _dot(mb, rhs.astype(BF16))
        res = rhs - x0 + _dot_split(a0, x0)
        sol = x0 + res + _dot(mb, res.astype(BF16))
        u = sol[:, :GDN_DV]
        wb = sol[:, GDN_DV:].astype(BF16)
        qkm = jnp.where(tri, _dot_nt(qh.astype(BF16), kb) * decay, 0.0)
        qd = (qh * eg).astype(BF16)
        kd = (kh * jnp.exp(gl - gc)).astype(BF16)
        egl = jnp.exp(gl)
        for j in range(rows // chunk):
            bi = (j * chunk) // tb
            r0 = j * chunk
            sc = s_ref[bi, h]
            sb = sc.astype(BF16)
            vnew = u[r0:r0 + chunk] - _dot(wb[r0:r0 + chunk], sb)
            os_ref[r0:r0 + chunk, :] = _dot(qd[r0:r0 + chunk], sb)
            vn_ref[r0:r0 + chunk, :] = vnew
            s_ref[bi, h] = sc * egl[r0:r0 + 1, :] + _dot_tn(kd[r0:r0 + chunk], vnew.astype(BF16))
        o = os_ref[...] + _dot(qkm.astype(BF16), vn_ref[...].astype(BF16))
        on = _rms(o, ng_ref[...])
        zh = z[:, h * GDN_DV:(h + 1) * GDN_DV]
        y_ref[:, :, h * GDN_DV:(h + 1) * GDN_DV] = (on * _silu(zh)).reshape(nb, tb, GDN_DV).astype(y_ref.dtype)
    so_ref[...] = s_ref[...]


def _gated_deltanet(fat, hist8, state, cw, alog_row, dt_row, ng, *, nb, tb):
    bsz, t, _ = fat.shape
    n_t = t // tb
    chunk = min(CHUNK, t)
    hl = SUBLANES
    qkv_blk = QKV_OFF // GDN_QKV_WIDTH
    z_blk = Z_OFF // (GDN_HEADS * GDN_DV)
    small_blk = SMALL_OFF // LANES
    in_specs = [pl.BlockSpec((nb, tb, GDN_QKV_WIDTH), lambda bi, i: (bi, i, qkv_blk))]
    args = [fat]
    if n_t > 1:
        per = tb // hl
        in_specs.append(pl.BlockSpec((nb, hl, GDN_QKV_WIDTH), lambda bi, i: (bi, jnp.maximum(i * per - 1, 0), qkv_blk)))
        args.append(fat)
    in_specs += [pl.BlockSpec((nb, hl, GDN_QKV_WIDTH), lambda bi, i: (bi, 0, 0)),
                 pl.BlockSpec((nb, tb, GDN_HEADS * GDN_DV), lambda bi, i: (bi, i, z_blk)),
                 pl.BlockSpec((nb, tb, LANES), lambda bi, i: (bi, i, small_blk)),
                 pl.BlockSpec((nb, GDN_HEADS, GDN_DK, GDN_DV), lambda bi, i: (bi, 0, 0, 0)),
                 pl.BlockSpec((GDN_CONV_WIDTH, GDN_QKV_WIDTH), lambda bi, i: (0, 0)),
                 pl.BlockSpec((1, LANES), lambda bi, i: (0, 0)),
                 pl.BlockSpec((1, LANES), lambda bi, i: (0, 0)),
                 pl.BlockSpec((1, GDN_DV), lambda bi, i: (0, 0))]
    args += [hist8, fat, fat, state, cw, alog_row, dt_row, ng]
    rows = nb * tb
    return pl.pallas_call(
        functools.partial(_gdn_kernel, n_t=n_t, chunk=chunk),
        grid=(bsz // nb, n_t),
        in_specs=in_specs,
        out_specs=[pl.BlockSpec((nb, tb, GDN_HEADS * GDN_DV), lambda bi, i: (bi, i, 0)),
                   pl.BlockSpec((nb, hl, GDN_QKV_WIDTH), lambda bi, i: (bi, 0, 0)),
                   pl.BlockSpec((nb, GDN_HEADS, GDN_DK, GDN_DV), lambda bi, i: (bi, 0, 0, 0))],
        out_shape=[jax.ShapeDtypeStruct((bsz, t, GDN_HEADS * GDN_DV), BF16),
                   jax.ShapeDtypeStruct((bsz, hl, GDN_QKV_WIDTH), F32),
                   jax.ShapeDtypeStruct((bsz, GDN_HEADS, GDN_DK, GDN_DV), F32)],
        scratch_shapes=[pltpu.VMEM((nb, hl + tb, GDN_QKV_WIDTH), F32),
                        pltpu.VMEM((nb, GDN_HEADS, GDN_DK, GDN_DV), F32),
                        pltpu.VMEM((rows, GDN_DV), F32),
                        pltpu.VMEM((rows, GDN_DV), F32)],
        compiler_params=_cparams(("arbitrary", "arbitrary")),
        name="gated_deltanet",
    )(*args)


def _mla_prep_kernel(*refs, emit_kv, scale):
    if emit_kv:
        (blk_ref, cs_ref, sn_ref, qg_ref, kg_ref, wq_ref, wqs_ref, wk_ref, wvt_ref,
         q_ref, lat_ref, kpe_ref, k_ref, vt_ref) = refs
    else:
        (blk_ref, cs_ref, sn_ref, qg_ref, kg_ref, wq_ref, wqs_ref,
         q_ref, lat_ref, kpe_ref) = refs
    nb, tb, _ = blk_ref.shape
    rows = nb * tb
    blk = blk_ref[...]
    cs = cs_ref[...]
    sn = sn_ref[...]
    qn = _rms(blk[..., :MLA_Q_LORA], qg_ref[...]).reshape(rows, MLA_Q_LORA).astype(BF16)
    qf = _dot(qn, wq_ref[...]).reshape(nb, tb, MLA_HEADS * QK_PAD)
    qs = _dot(qn, wqs_ref[...]).reshape(nb, tb, MLA_HEADS * LANES)
    for h in range(MLA_HEADS):
        nope = qf[..., h * QK_PAD:h * QK_PAD + MLA_NOPE_DIM]
        rope = qf[..., h * QK_PAD + MLA_NOPE_DIM:(h + 1) * QK_PAD] * cs + qs[..., h * LANES:(h + 1) * LANES] * sn
        q_ref[h, :, :, 0:MLA_NOPE_DIM] = (nope * scale).astype(q_ref.dtype)
        q_ref[h, :, :, MLA_NOPE_DIM:QK_PAD] = (rope * scale).astype(q_ref.dtype)

    lat = _rms(blk[..., MLA_Q_LORA + LANES:], kg_ref[...])
    lat_ref[...] = lat

    small = blk[..., MLA_Q_LORA:MLA_Q_LORA + LANES].reshape(rows, LANES)
    half = MLA_ROPE_DIM // 2
    lane = lax.broadcasted_iota(jnp.int32, (rows, LANES), 1)
    swapped = jnp.where(lane < half, pltpu.roll(small, LANES - half, 1), pltpu.roll(small, half, 1))
    kpe = small.reshape(nb, tb, LANES) * cs + swapped.reshape(nb, tb, LANES) * sn
    kpe_ref[...] = kpe[..., :MLA_ROPE_DIM]

    if emit_kv:
        latb = lat.reshape(rows, MLA_KV_LORA).astype(BF16)
        kn = _dot(latb, wk_ref[...]).reshape(nb, tb, MLA_HEADS * MLA_NOPE_DIM)
        ones_rows = jnp.where(lax.broadcasted_iota(jnp.int32, (VT_ROWS - MLA_V_DIM, ATTN_KB), 0) == 0,
                              1.0, 0.0).astype(vt_ref.dtype)
        for h in range(MLA_HEADS):
            k_ref[h, :, :, 0:MLA_NOPE_DIM] = kn[..., h * MLA_NOPE_DIM:(h + 1) * MLA_NOPE_DIM].astype(k_ref.dtype)
            k_ref[h, :, :, MLA_NOPE_DIM:QK_PAD] = kpe.astype(k_ref.dtype)
            vt = _dot_nt(wvt_ref[h], latb).astype(vt_ref.dtype)
            for kb in range(rows // ATTN_KB):
                vt_ref[h, 0, kb, 0:MLA_V_DIM, :] = vt[:, kb * ATTN_KB:(kb + 1) * ATTN_KB]
                vt_ref[h, 0, kb, MLA_V_DIM:VT_ROWS, :] = ones_rows


def _mla_prep(fat, cs, sn, qg, kg, wq, wqs, wk, wvt, *, nb, tb, emit_kv, scale):
    bsz, t, _ = fat.shape
    blk = QLAT_OFF // MLA_BLK_W
    hq = MLA_HEADS
    in_specs = [pl.BlockSpec((nb, tb, MLA_BLK_W), lambda bi, i: (bi, i, blk)),
                pl.BlockSpec((tb, LANES), lambda bi, i: (i, 0)),
                pl.BlockSpec((tb, LANES), lambda bi, i: (i, 0)),
                pl.BlockSpec((1, MLA_Q_LORA), lambda bi, i: (0, 0)),
                pl.BlockSpec((1, MLA_KV_LORA), lambda bi, i: (0, 0)),
                pl.BlockSpec(wq.shape, lambda bi, i: (0, 0)),
                pl.BlockSpec(wqs.shape, lambda bi, i: (0, 0))]
    args = [fat, cs, sn, qg, kg, wq, wqs]
    out_specs = [pl.BlockSpec((hq, nb, tb, QK_PAD), lambda bi, i: (0, bi, i, 0)),
                 pl.BlockSpec((nb, tb, MLA_KV_LORA), lambda bi, i: (bi, i, 0)),
                 pl.BlockSpec((nb, tb, MLA_ROPE_DIM), lambda bi, i: (bi, i, 0))]
    out_shape = [jax.ShapeDtypeStruct((hq, bsz, t, QK_PAD), BF16),
                 jax.ShapeDtypeStruct((bsz, t, MLA_KV_LORA), F32),
                 jax.ShapeDtypeStruct((bsz, t, MLA_ROPE_DIM), F32)]
    if emit_kv:
        assert nb == 1 and tb % ATTN_KB == 0
        in_specs += [pl.BlockSpec(wk.shape, lambda bi, i: (0, 0)),
                     pl.BlockSpec(wvt.shape, lambda bi, i: (0, 0, 0))]
        args += [wk, wvt]
        per = tb // ATTN_KB
        out_specs += [pl.BlockSpec((hq, nb, tb, QK_PAD), lambda bi, i: (0, bi, i, 0)),
                      pl.BlockSpec((hq, nb, per, VT_ROWS, ATTN_KB), lambda bi, i: (0, bi, i, 0, 0))]
        out_shape += [jax.ShapeDtypeStruct((hq, bsz, t, QK_PAD), BF16),
                      jax.ShapeDtypeStruct((hq, bsz, t // ATTN_KB, VT_ROWS, ATTN_KB), BF16)]
    return pl.pallas_call(
        functools.partial(_mla_prep_kernel, emit_kv=emit_kv, scale=scale),
        grid=(bsz // nb, t // tb),
        in_specs=in_specs, out_specs=out_specs, out_shape=out_shape,
        compiler_params=_cparams(("arbitrary", "arbitrary")),
        name="mla_prep",
    )(*args)


def _attn_prompt_kernel(q_ref, k_ref, vt_ref, o_ref, m_ref, acc_ref, *, nq):
    kb = ATTN_KB
    j = pl.program_id(2)
    m_ref[...] = jnp.full(m_ref.shape, -jnp.inf, F32)
    acc_ref[...] = jnp.zeros(acc_ref.shape, F32)

    def update(chains, ki, masked_chain):
        k0 = pl.multiple_of(ki * kb, kb)
        kblk = k_ref[0, 0, pl.ds(k0, kb), :]
        sts = [_dot_nt(kblk, q_ref[0, 0, c * kb:(c + 1) * kb, :]) for c in chains]
        pts, alphas = [], []
        for c, st in zip(chains, sts):
            if c == masked_chain:
                key_chunk = lax.broadcasted_iota(jnp.int32, (kb, kb), 0) // CHUNK
                qry_chunk = lax.broadcasted_iota(jnp.int32, (kb, kb), 1) // CHUNK
                st = jnp.where(key_chunk <= qry_chunk, st, -jnp.inf)
            m_old = m_ref[c]
            m_new = jnp.maximum(m_old, jnp.max(st, axis=0, keepdims=True))
            pts.append(jnp.exp2(st - m_new).astype(BF16))
            alphas.append(jnp.exp2(m_old - m_new))
            m_ref[c] = m_new
        vblk = vt_ref[0, 0, ki]
        for c, pt, alpha in zip(chains, pts, alphas):
            acc_ref[c] = alpha * acc_ref[c] + _dot(vblk, pt)

    def body(ki, carry):
        update(range(nq), ki, None)
        return carry

    lax.fori_loop(0, nq * j, body, 0)
    for d in range(nq):
        update(range(d, nq), nq * j + d, d)
    for c in range(nq):
        acc = acc_ref[c]
        ot = acc[0:MLA_V_DIM] / acc[MLA_V_DIM:MLA_V_DIM + 1]
        o_ref[0, c * kb:(c + 1) * kb, :] = ot.T.astype(o_ref.dtype)


def _attn_prompt(q, k, vt, *, nq):
    hq, bsz, t, _ = q.shape
    tq = ATTN_KB * nq
    return pl.pallas_call(
        functools.partial(_attn_prompt_kernel, nq=nq),
        grid=(bsz, hq, t // tq),
        in_specs=[pl.BlockSpec((1, 1, tq, QK_PAD), lambda b, h, i: (h, b, i, 0)),
                  pl.BlockSpec((1, 1, t, QK_PAD), lambda b, h, i: (h, b, 0, 0)),
                  pl.BlockSpec((1, 1, t // ATTN_KB, VT_ROWS, ATTN_KB), lambda b, h, i: (h, b, 0, 0, 0))],
        out_specs=pl.BlockSpec((1, tq, MLA_V_DIM), lambda b, h, i: (b, i, h)),
        out_shape=jax.ShapeDtypeStruct((bsz, t, MLA_HEADS * MLA_V_DIM), BF16),
        scratch_shapes=[pltpu.VMEM((nq, 1, ATTN_KB), F32), pltpu.VMEM((nq, VT_ROWS, ATTN_KB), F32)],
        compiler_params=_cparams(("arbitrary", "arbitrary", "arbitrary")),
        name="attn_prompt",
    )(q, k, vt)


def _attn_sample_kernel(q_ref, latc_ref, kpec_ref, latn_ref, kpen_ref, wuk_ref, wuv_ref, o_ref):
    tn = latn_ref.shape[1]
    past = latc_ref.shape[1]
    hq = MLA_HEADS
    qabs = jnp.concatenate(
        [_dot(q_ref[h, 0, :, 0:MLA_NOPE_DIM], wuk_ref[h]) for h in range(hq)], axis=0).astype(BF16)
    qrope = jnp.concatenate([q_ref[h, 0, :, MLA_NOPE_DIM:MLA_NOPE_DIM + MLA_ROPE_DIM] for h in range(hq)], axis=0)
    latc = latc_ref[0].astype(BF16)
    latn = latn_ref[0].astype(BF16)
    sc = _dot_nt(qabs, latc) + _dot_nt(qrope, kpec_ref[0].astype(BF16))
    sn = _dot_nt(qabs, latn) + _dot_nt(qrope, kpen_ref[0].astype(BF16))
    rows = hq * tn
    qpos = past + lax.broadcasted_iota(jnp.int32, (rows, 1), 0) % tn
    kc_pos = lax.broadcasted_iota(jnp.int32, (rows, past), 1)
    kn_pos = past + lax.broadcasted_iota(jnp.int32, (rows, tn), 1)
    sc = jnp.where(kc_pos // CHUNK <= qpos // CHUNK, sc, -jnp.inf)
    sn = jnp.where(kn_pos // CHUNK <= qpos // CHUNK, sn, -jnp.inf)
    m = jnp.maximum(jnp.max(sc, axis=-1, keepdims=True), jnp.max(sn, axis=-1, keepdims=True))
    pc = jnp.exp(sc - m)
    pn = jnp.exp(sn - m)
    l = jnp.sum(pc, axis=-1, keepdims=True) + jnp.sum(pn, axis=-1, keepdims=True)
    olat = (_dot(pc.astype(BF16), latc) + _dot(pn.astype(BF16), latn)) / l
    for h in range(hq):
        oh = _dot(olat[h * tn:(h + 1) * tn].astype(BF16), wuv_ref[h])
        o_ref[0, :, h * MLA_V_DIM:(h + 1) * MLA_V_DIM] = oh.astype(o_ref.dtype)


def _attn_sample(q, cache_lat, cache_kpe, lat_new, kpe_new, wuk_t, wuv):
    hq, bsz, tn, _ = q.shape
    past = cache_lat.shape[1]
    return pl.pallas_call(
        _attn_sample_kernel,
        grid=(bsz,),
        in_specs=[pl.BlockSpec((hq, 1, tn, QK_PAD), lambda b: (0, b, 0, 0)),
                  pl.BlockSpec((1, past, MLA_KV_LORA), lambda b: (b, 0, 0)),
                  pl.BlockSpec((1, past, MLA_ROPE_DIM), lambda b: (b, 0, 0)),
                  pl.BlockSpec((1, tn, MLA_KV_LORA), lambda b: (b, 0, 0)),
                  pl.BlockSpec((1, tn, MLA_ROPE_DIM), lambda b: (b, 0, 0)),
                  pl.BlockSpec(wuk_t.shape, lambda b: (0, 0, 0)),
                  pl.BlockSpec(wuv.shape, lambda b: (0, 0, 0))],
        out_specs=pl.BlockSpec((1, tn, MLA_HEADS * MLA_V_DIM), lambda b: (b, 0, 0)),
        out_shape=jax.ShapeDtypeStruct((bsz, tn, MLA_HEADS * MLA_V_DIM), BF16),
        compiler_params=_cparams(("arbitrary",)),
        name="attn_sample",
    )(q, cache_lat, cache_kpe, lat_new, kpe_new, wuk_t, wuv)


def _merge_kernel(ya_ref, yb_ref, yc_ref, gate_ref, x_ref, g1_ref, g0_ref, b0_ref, wb_ref, wo_ref,
                  lg_ref, lb_ref, o_ref, *, ln0, alpha):
    nb, tb, d = x_ref.shape
    rows = nb * tb
    x = x_ref[...]
    if ln0:
        x = _ln(x, g0_ref[...], b0_ref[...])
    merged = None
    for i, y_ref in enumerate((ya_ref, yb_ref, yc_ref)):
        proj = _dot(y_ref[...].reshape(rows, BRANCH_WIDTH), wb_ref[i])
        gate = _sigmoid(gate_ref[:, :, i * d:(i + 1) * d]).reshape(rows, d)
        merged = gate * proj if merged is None else merged + gate * proj
    out = _dot(merged.astype(BF16), wo_ref[...]).reshape(nb, tb, d)
    o_ref[...] = _ln(alpha * x + (1.0 + g1_ref[...]) * out, lg_ref[...], lb_ref[...])


def _merge(ya, yb, yc, fat, x, g1, g0, b0, wb, wo, lg, lb, *, nb, tb, ln0, alpha):
    bsz, t, d = x.shape
    row = lambda bi, i: (bi, i, 0)
    const2 = lambda bi, i: (0, 0)
    return pl.pallas_call(
        functools.partial(_merge_kernel, ln0=ln0, alpha=alpha),
        grid=(bsz // nb, t // tb),
        in_specs=[pl.BlockSpec((nb, tb, BRANCH_WIDTH), row),
                  pl.BlockSpec((nb, tb, BRANCH_WIDTH), row),
                  pl.BlockSpec((nb, tb, BRANCH_WIDTH), row),
                  pl.BlockSpec((nb, tb, GATE_W), row),
                  pl.BlockSpec((nb, tb, d), row),
                  pl.BlockSpec((nb, 1, d), lambda bi, i: (bi, 0, 0)),
                  pl.BlockSpec((1, d), const2),
                  pl.BlockSpec((1, d), const2),
                  pl.BlockSpec(wb.shape, lambda bi, i: (0, 0, 0)),
                  pl.BlockSpec(wo.shape, const2),
                  pl.BlockSpec((1, d), const2),
                  pl.BlockSpec((1, d), const2)],
        out_specs=pl.BlockSpec((nb, tb, d), row),
        out_shape=jax.ShapeDtypeStruct((bsz, t, d), F32),
        compiler_params=_cparams(("arbitrary", "arbitrary")),
        name="merge",
    )(ya, yb, yc, fat, x, g1, g0, b0, wb, wo, lg, lb)


FFN_BLOCK = 1408


def _ffn_kernel(x_ref, sc_ref, sh_ref, g2_ref, wa_ref, wv_ref, wd_ref, cw_ref, cb_ref, lg_ref, lb_ref, hist_ref,
                o_ref, ho_ref, h_ref, acc_ref, carry_ref, ap_ref, *, alpha):
    nb, tb, d = x_ref.shape
    rows = nb * tb
    t = pl.program_id(1)
    f = pl.program_id(2)
    n_f = pl.num_programs(2)
    hl = SUBLANES
    fb = wa_ref.shape[1]

    @pl.when(f == 0)
    def _():
        h = x_ref[...] * (1.0 + sc_ref[...]) + sh_ref[...]
        h_ref[...] = h.reshape(rows, d).astype(BF16)

    @pl.when(t == 0)
    def _():
        carry_ref[f] = hist_ref[...]

    h = h_ref[...]
    a = _dot(h, wa_ref[...]).reshape(nb, tb, fb)
    v = _dot(h, wv_ref[...]).reshape(nb, tb, fb)
    ap_ref[:, 0:hl, :] = carry_ref[f]
    ap_ref[:, hl:hl + tb, :] = a
    off = hl - (FFN_CONV_WIDTH - 1)
    conv = cw_ref[FFN_CONV_WIDTH - 1:FFN_CONV_WIDTH, :] * a
    for k in range(FFN_CONV_WIDTH - 1):
        conv = conv + cw_ref[k:k + 1, :] * ap_ref[:, off + k:off + k + tb, :]
    act = _silu(conv + cb_ref[...]) * v
    contrib = _dot(act.reshape(rows, fb).astype(BF16), wd_ref[...])
    last = ap_ref[:, tb:tb + hl, :]
    carry_ref[f] = last
    ho_ref[0] = last

    @pl.when(f == 0)
    def _():
        acc_ref[...] = contrib

    @pl.when(f > 0)
    def _():
        acc_ref[...] = acc_ref[...] + contrib

    @pl.when(f == n_f - 1)
    def _():
        y = acc_ref[...].reshape(nb, tb, d)
        o_ref[...] = _ln(alpha * x_ref[...] + (1.0 + g2_ref[...]) * y, lg_ref[...], lb_ref[...])


def _ffn(x, sc, sh, g2, wa, wv, wd, cw, cb, lg, lb, hist8, *, nb, tb, alpha):
    bsz, t, d = x.shape
    dff = wa.shape[1]
    fb = FFN_BLOCK
    n_f = dff // fb
    hl = SUBLANES
    rows = nb * tb
    row = lambda bi, i, f: (bi, i, 0)
    mod = lambda bi, i, f: (bi, 0, 0)
    const2 = lambda bi, i, f: (0, 0)
    return pl.pallas_call(
        functools.partial(_ffn_kernel, alpha=alpha),
        grid=(bsz // nb, t // tb, n_f),
        in_specs=[pl.BlockSpec((nb, tb, d), row),
                  pl.BlockSpec((nb, 1, d), mod),
                  pl.BlockSpec((nb, 1, d), mod),
                  pl.BlockSpec((nb, 1, d), mod),
                  pl.BlockSpec((d, fb), lambda bi, i, f: (0, f)),
                  pl.BlockSpec((d, fb), lambda bi, i, f: (0, f)),
                  pl.BlockSpec((fb, d), lambda bi, i, f: (f, 0)),
                  pl.BlockSpec((FFN_CONV_WIDTH, fb), lambda bi, i, f: (0, f)),
                  pl.BlockSpec((1, fb), lambda bi, i, f: (0, f)),
                  pl.BlockSpec((1, d), const2),
                  pl.BlockSpec((1, d), const2),
                  pl.BlockSpec((nb, hl, fb), lambda bi, i, f: (bi, 0, f))],
        out_specs=[pl.BlockSpec((nb, tb, d), row),
                   pl.BlockSpec((1, nb, hl, fb), lambda bi, i, f: (i, bi, 0, f))],
        out_shape=[jax.ShapeDtypeStruct((bsz, t, d), F32),
                   jax.ShapeDtypeStruct((t // tb, bsz, hl, dff), F32)],
        scratch_shapes=[pltpu.VMEM((rows, d), BF16),
                        pltpu.VMEM((rows, d), F32),
                        pltpu.VMEM((n_f, nb, hl, fb), F32),
                        pltpu.VMEM((nb, hl + tb, fb), F32)],
        compiler_params=_cparams(("arbitrary", "arbitrary", "arbitrary")),
        name="conv_ffn",
    )(x, sc, sh, g2, wa, wv, wd, cw, cb, lg, lb, hist8)


def _front_pad(hist, rows):
    b, r, c = hist.shape
    return jnp.concatenate([jnp.zeros((b, rows - r, c), hist.dtype), hist], axis=1)


def _rope_rows(pos):
    half = MLA_ROPE_DIM // 2
    inv_freq = ROPE_THETA ** (-jnp.arange(half, dtype=F32) / half)
    ang = pos.astype(F32)[:, None] * inv_freq[None, :]
    cos, sin = jnp.cos(ang), jnp.sin(ang)
    pad = jnp.zeros((pos.shape[0], LANES - MLA_ROPE_DIM), F32)
    return jnp.concatenate([cos, cos, pad], axis=1), jnp.concatenate([-sin, sin, pad], axis=1)


def _lane_row(vals, lane0):
    return jnp.zeros((1, LANES), F32).at[0, lane0:lane0 + vals.shape[0]].set(vals.astype(F32))


def _layer_weights(l, w_in, mla_w_uq, mla_w_ukv, w_branch, w_out, w_up, w_down):
    d = D_MODEL
    o = np.cumsum([0, 2 * CONV_A_CH, GDN_QKV_WIDTH, GDN_HEADS * GDN_DV, GDN_HEADS, GDN_HEADS,
                   MLA_Q_LORA, MLA_KV_LORA, MLA_ROPE_DIM, N_BRANCH * D_MODEL]).tolist()
    w = w_in[l]
    seg = lambda i: w[:, o[i]:o[i + 1]]
    pad = jnp.zeros((d, LANES - MLA_ROPE_DIM - 2 * GDN_HEADS), w.dtype)
    w_fat = jnp.concatenate([seg(8), seg(1), seg(2), seg(0), seg(5), seg(7), seg(3), seg(4), pad, seg(6)],
                            axis=1).astype(BF16)
    hd = MLA_NOPE_DIM + MLA_ROPE_DIM
    half = MLA_ROPE_DIM // 2
    wq = mla_w_uq[l]
    zq = jnp.zeros((MLA_Q_LORA, QK_PAD - hd), wq.dtype)
    zs = jnp.zeros((MLA_Q_LORA, LANES - MLA_ROPE_DIM), wq.dtype)
    wq_cols, wqs_cols = [], []
    for h in range(MLA_HEADS):
        wq_cols += [wq[:, h * hd:(h + 1) * hd], zq]
        r0 = h * hd + MLA_NOPE_DIM
        wqs_cols += [wq[:, r0 + half:r0 + 2 * half], wq[:, r0:r0 + half], zs]
    wkv = mla_w_ukv[l]
    kvw = MLA_NOPE_DIM + MLA_V_DIM
    wuk_t = jnp.stack([wkv[:, h * kvw:h * kvw + MLA_NOPE_DIM].T for h in range(MLA_HEADS)]).astype(BF16)
    wuv = jnp.stack([wkv[:, h * kvw + MLA_NOPE_DIM:(h + 1) * kvw] for h in range(MLA_HEADS)]).astype(BF16)
    return dict(
        w_fat=w_fat,
        wq=jnp.concatenate(wq_cols, axis=1).astype(BF16),
        wqs=jnp.concatenate(wqs_cols, axis=1).astype(BF16),
        wk=jnp.concatenate([wkv[:, h * kvw:h * kvw + MLA_NOPE_DIM] for h in range(MLA_HEADS)], axis=1).astype(BF16),
        wvt=jnp.transpose(wuv, (0, 2, 1)), wuk_t=wuk_t, wuv=wuv,
        wb=w_branch[l].astype(BF16), wo=w_out[l].astype(BF16),
        wa=w_up[l][:, :D_FF].astype(BF16), wv=w_up[l][:, D_FF:].astype(BF16), wd=w_down[l].astype(BF16),
    )


def _run_layer(x, mods, st, wts, p, *, ln0, g0, b0, alpha, cs, sn, tiles, prompt):
    bsz, t, d = x.shape
    sh1, sc1, g1, sh2, sc2, g2 = mods
    nb, tb, gdn_nb, gdn_tb = tiles
    if st is None:
        hist_a = jnp.zeros((bsz, CONV_A_WIDTH - 1, CONV_A_CH), F32)
        hist_b = jnp.zeros((bsz, GDN_CONV_WIDTH - 1, GDN_QKV_WIDTH), F32)
        state_b = jnp.zeros((bsz, GDN_HEADS, GDN_DK, GDN_DV), F32)
        hist_f = jnp.zeros((bsz, FFN_CONV_WIDTH - 1, D_FF), F32)
    else:
        cache_lat, cache_kpe, hist_a, hist_b, state_b, hist_f = st
    row = lambda v: v.reshape(1, -1)

    fat = _inproj(x, sc1, sh1, g0, b0, wts['w_fat'], nb=nb, tb=tb, ln0=ln0)
    y_a, ha = _conv_module(fat, _front_pad(hist_a, CONV_A_HALO), p['conv_a_w'], row(p['conv_a_b']),
                           row(p['ln_a_g']), row(p['ln_a_b']), nb=nb, tb=tb)
    y_b, hb, sb = _gated_deltanet(fat, _front_pad(hist_b, SUBLANES), state_b, p['gdn_conv_w'],
                                  _lane_row(p['gdn_a_log'], DEC_LANE), _lane_row(p['gdn_dt_bias'], DEC_LANE),
                                  row(p['gdn_norm_g']), nb=gdn_nb, tb=gdn_tb)
    if prompt:
        q, lat, kpe, k, v = _mla_prep(fat, cs, sn, row(p['mla_q_norm_g']), row(p['mla_kv_norm_g']),
                                      wts['wq'], wts['wqs'], wts['wk'], wts['wvt'], nb=nb, tb=tb, emit_kv=True,
                                      scale=ATTN_SCALE * math.log2(math.e))
        y_c = _attn_prompt(q, k, v, nq=math.gcd(ATTN_CHAINS, t // ATTN_KB))
    else:
        q, lat, kpe = _mla_prep(fat, cs, sn, row(p['mla_q_norm_g']), row(p['mla_kv_norm_g']),
                                wts['wq'], wts['wqs'], None, None, nb=nb, tb=tb, emit_kv=False, scale=ATTN_SCALE)
        y_c = _attn_sample(q, cache_lat, cache_kpe, lat, kpe, wts['wuk_t'], wts['wuv'])
    x1 = _merge(y_a, y_b, y_c, fat, x, g1, g0, b0, wts['wb'], wts['wo'], row(p['ln1_g']), row(p['ln1_b']),
                nb=nb, tb=tb, ln0=ln0, alpha=alpha)
    x2, hf = _ffn(x1, sc2, sh2, g2, wts['wa'], wts['wv'], wts['wd'], p['ffn_conv_w'], row(p['ffn_conv_b']),
                  row(p['ln2_g']), row(p['ln2_b']), _front_pad(hist_f, SUBLANES), nb=nb, tb=tb, alpha=alpha)
    states = (lat, kpe, ha[:, CONV_A_HALO - (CONV_A_WIDTH - 1):], hb[:, SUBLANES - (GDN_CONV_WIDTH - 1):],
              sb, hf[-1, :, SUBLANES - (FFN_CONV_WIDTH - 1):])
    return x2, states


def kernel(x_prompt, x_sample, cache_mla_latent, cache_mla_kpe, state_conv_a, state_gdn_conv, state_gdn, state_ffn_conv, c_prompt, c_sample, ln0_g, ln0_b, w_ada, b_ada, w_in, conv_a_w, conv_a_b, ln_a_g, ln_a_b, gdn_conv_w, gdn_a_log, gdn_dt_bias, gdn_norm_g, mla_q_norm_g, mla_kv_norm_g, mla_w_uq, mla_w_ukv, w_branch, w_out, ln1_g, ln1_b, w_up, ffn_conv_w, ffn_conv_b, w_down, ln2_g, ln2_b):
    depth = w_ada.shape[0]
    bp, seq, d = x_prompt.shape
    bs, dec_seq, _ = x_sample.shape
    past = cache_mla_latent.shape[2]
    alpha = (2 * depth) ** 0.25

    n_c = bp + bs
    c_rows = -(-n_c // SUBLANES) * SUBLANES
    c_all = jnp.concatenate([c_prompt, c_sample, jnp.zeros((c_rows - n_c, d), F32)], axis=0)
    mod = _modulation(c_all, w_ada, b_ada)

    cs_p, sn_p = _rope_rows(jnp.arange(seq, dtype=jnp.int32))
    cs_s, sn_s = _rope_rows(past + jnp.arange(dec_seq, dtype=jnp.int32))
    g0, b0 = ln0_g.reshape(1, d), ln0_b.reshape(1, d)

    tile_p = min(512, seq)
    tiles_p = (1, tile_p, 1, min(GDN_ROWS, seq))
    tiles_s = (bs, dec_seq, GDN_ROWS // dec_seq, dec_seq)

    xp, xs = x_prompt, x_sample
    p_states, s_states = [], []
    for l in range(depth):
        p = {'conv_a_w': conv_a_w[l], 'conv_a_b': conv_a_b[l], 'ln_a_g': ln_a_g[l], 'ln_a_b': ln_a_b[l],
             'gdn_conv_w': gdn_conv_w[l], 'gdn_a_log': gdn_a_log[l], 'gdn_dt_bias': gdn_dt_bias[l],
             'gdn_norm_g': gdn_norm_g[l], 'mla_q_norm_g': mla_q_norm_g[l], 'mla_kv_norm_g': mla_kv_norm_g[l],
             'ln1_g': ln1_g[l], 'ln1_b': ln1_b[l], 'ffn_conv_w': ffn_conv_w[l], 'ffn_conv_b': ffn_conv_b[l],
             'ln2_g': ln2_g[l], 'ln2_b': ln2_b[l]}
        wts = _layer_weights(l, w_in, mla_w_uq, mla_w_ukv, w_branch, w_out, w_up, w_down)
        mods_p = [m.reshape(bp, 1, d) for m in jnp.split(mod[l, :bp], 6, axis=-1)]
        mods_s = [m.reshape(bs, 1, d) for m in jnp.split(mod[l, bp:n_c], 6, axis=-1)]
        xp, st_p = _run_layer(xp, mods_p, None, wts, p, ln0=(l == 0), g0=g0, b0=b0, alpha=alpha,
                              cs=cs_p, sn=sn_p, tiles=tiles_p, prompt=True)
        st_in = (cache_mla_latent[l], cache_mla_kpe[l], state_conv_a[l], state_gdn_conv[l], state_gdn[l],
                 state_ffn_conv[l])
        xs, st_s = _run_layer(xs, mods_s, st_in, wts, p, ln0=(l == 0), g0=g0, b0=b0, alpha=alpha,
                              cs=cs_s, sn=sn_s, tiles=tiles_s, prompt=False)
        p_states.append(st_p)
        s_states.append(st_s)
    p_out = [jnp.stack(z, axis=0) for z in zip(*p_states)]
    s_out = [jnp.stack(z, axis=0) for z in zip(*s_states)]
    return (xp, xs, *p_out, *s_out)
```

```python
import functools
import math

import jax
import jax.numpy as jnp
import numpy as np
from jax import lax
from jax.experimental import pallas as pl
from jax.experimental.pallas import tpu as pltpu

F32 = jnp.float32
BF16 = jnp.bfloat16

D_MODEL = 1024
CHUNK = 64
CONV_A_CH = 512
CONV_A_WIDTH = 31
GDN_HEADS = 4
GDN_DK = 128
GDN_DV = 128
GDN_CONV_WIDTH = 4
GDN_QKV_WIDTH = 2 * GDN_HEADS * GDN_DK + GDN_HEADS * GDN_DV
MLA_HEADS = 4
MLA_Q_LORA = 384
MLA_KV_LORA = 256
MLA_NOPE_DIM = 128
MLA_ROPE_DIM = 64
MLA_V_DIM = 128
ROPE_THETA = 10000.0
N_BRANCH = 3
BRANCH_WIDTH = 512
D_FF = 2816
FFN_CONV_WIDTH = 3
LN_EPS = 1e-5
RMS_EPS = 1e-6

LANES = 128
SUBLANES = 8
VMEM_LIMIT = 48 * 1024 * 1024

GATE_OFF, GATE_W = 0, N_BRANCH * D_MODEL
QKV_OFF = GATE_OFF + GATE_W
Z_OFF = QKV_OFF + GDN_QKV_WIDTH
PREA_OFF = Z_OFF + GDN_HEADS * GDN_DV
QLAT_OFF = PREA_OFF + 2 * CONV_A_CH
SMALL_OFF = QLAT_OFF + MLA_Q_LORA
KVLAT_OFF = SMALL_OFF + LANES
FAT_W = KVLAT_OFF + MLA_KV_LORA
MLA_BLK_W = FAT_W - QLAT_OFF
BETA_LANE = MLA_ROPE_DIM
DEC_LANE = MLA_ROPE_DIM + GDN_HEADS
QK_PAD = 2 * LANES
ATTN_KB = 256
ATTN_CHAINS = 8
VT_ROWS = MLA_V_DIM + 16
ATTN_SCALE = (MLA_NOPE_DIM + MLA_ROPE_DIM) ** -0.5


def _cparams(sem):
    return pltpu.CompilerParams(dimension_semantics=sem, vmem_limit_bytes=VMEM_LIMIT)


def _sigmoid(x):
    return jax.nn.sigmoid(x)


def _silu(x):
    return x * jax.nn.sigmoid(x)


def _ln(x, g, b):
    mu = jnp.mean(x, axis=-1, keepdims=True)
    xc = x - mu
    var = jnp.mean(xc * xc, axis=-1, keepdims=True)
    return xc * lax.rsqrt(var + LN_EPS) * g + b


def _rms(x, g):
    return x * lax.rsqrt(jnp.mean(x * x, axis=-1, keepdims=True) + RMS_EPS) * g


def _dot(a, b):
    return jnp.dot(a, b, preferred_element_type=F32)


def _dot_nt(a, b):
    return lax.dot_general(a, b, (((1,), (1,)), ((), ())), preferred_element_type=F32)


def _dot_tn(a, b):
    return lax.dot_general(a, b, (((0,), (0,)), ((), ())), preferred_element_type=F32)


def _dot_split(a, b):
    ah = a.astype(BF16)
    al = (a - ah.astype(F32)).astype(BF16)
    bh = b.astype(BF16)
    bl = (b - bh.astype(F32)).astype(BF16)
    return _dot(ah, bh) + _dot(ah, bl) + _dot(al, bh)


def _split3(x):
    hi = x.astype(BF16)
    r = x - hi.astype(F32)
    mid = r.astype(BF16)
    lo = (r - mid.astype(F32)).astype(BF16)
    return hi, mid, lo


def _mod_kernel(c_ref, w_ref, b_ref, o_ref):
    s = _silu(c_ref[...])
    o_ref[0] = _dot(s.astype(BF16), w_ref[0].astype(BF16)) + b_ref[0]


def _modulation(c_all, w_ada, b_ada):
    depth, d, n = w_ada.shape
    rows = c_all.shape[0]
    tn = 1536
    return pl.pallas_call(
        _mod_kernel,
        grid=(depth, n // tn),
        in_specs=[pl.BlockSpec((rows, d), lambda l, j: (0, 0)),
                  pl.BlockSpec((1, d, tn), lambda l, j: (l, 0, j)),
                  pl.BlockSpec((1, 1, tn), lambda l, j: (l, 0, j))],
        out_specs=pl.BlockSpec((1, rows, tn), lambda l, j: (l, 0, j)),
        out_shape=jax.ShapeDtypeStruct((depth, rows, n), F32),
        compiler_params=_cparams(("arbitrary", "arbitrary")),
        name="modulation",
    )(c_all, w_ada, b_ada.reshape(depth, 1, n))


def _inproj_kernel(x_ref, sc_ref, sh_ref, g0_ref, b0_ref, w_ref, o_ref, *, ln0):
    nb, tb, d = x_ref.shape
    x = x_ref[...]
    if ln0:
        x = _ln(x, g0_ref[...], b0_ref[...])
    h = x * (1.0 + sc_ref[...]) + sh_ref[...]
    o = _dot(h.reshape(nb * tb, d).astype(BF16), w_ref[...])
    o_ref[...] = o.reshape(nb, tb, o.shape[-1])


def _inproj(x, sc, sh, g0, b0, w_fat, *, nb, tb, ln0):
    bsz, t, d = x.shape
    tn = FAT_W // 3
    return pl.pallas_call(
        functools.partial(_inproj_kernel, ln0=ln0),
        grid=(FAT_W // tn, bsz // nb, t // tb),
        in_specs=[pl.BlockSpec((nb, tb, d), lambda n, b, i: (b, i, 0)),
                  pl.BlockSpec((nb, 1, d), lambda n, b, i: (b, 0, 0)),
                  pl.BlockSpec((nb, 1, d), lambda n, b, i: (b, 0, 0)),
                  pl.BlockSpec((1, d), lambda n, b, i: (0, 0)),
                  pl.BlockSpec((1, d), lambda n, b, i: (0, 0)),
                  pl.BlockSpec((d, tn), lambda n, b, i: (0, n))],
        out_specs=pl.BlockSpec((nb, tb, tn), lambda n, b, i: (b, i, n)),
        out_shape=jax.ShapeDtypeStruct((bsz, t, FAT_W), F32),
        compiler_params=_cparams(("arbitrary", "arbitrary", "arbitrary")),
        name="inproj",
    )(x, sc, sh, g0, b0, w_fat)


CONV_A_HALO = 32


def _glu(pre):
    return pre[..., :CONV_A_CH] * _sigmoid(pre[..., CONV_A_CH:])


def _conva_kernel(*refs, n_t, nbc, rc):
    if n_t > 1:
        pre_ref, halo_ref, hist_ref, w_ref, cb_ref, g_ref, b_ref, y_ref, ho_ref, xp_ref = refs
    else:
        pre_ref, hist_ref, w_ref, cb_ref, g_ref, b_ref, y_ref, ho_ref, xp_ref = refs
    nb, tb, _ = pre_ref.shape
    t = pl.program_id(1)
    hl = CONV_A_HALO

    @pl.when(t == 0)
    def _():
        xp_ref[:, 0:hl, :] = hist_ref[...]

    if n_t > 1:
        @pl.when(t > 0)
        def _():
            xp_ref[:, 0:hl, :] = _glu(halo_ref[...])

    xp_ref[:, hl:hl + tb, :] = _glu(pre_ref[...])
    off = hl - (CONV_A_WIDTH - 1)
    for b0 in range(0, nb, nbc):
        for r0 in range(0, tb, rc):
            acc = w_ref[0:1, :] * xp_ref[b0:b0 + nbc, r0 + off:r0 + off + rc, :]
            for k in range(1, CONV_A_WIDTH):
                acc = acc + w_ref[k:k + 1, :] * xp_ref[b0:b0 + nbc, r0 + off + k:r0 + off + k + rc, :]
            y = _silu(_ln(acc + cb_ref[...], g_ref[...], b_ref[...]))
            y_ref[b0:b0 + nbc, r0:r0 + rc, :] = y.astype(y_ref.dtype)
    ho_ref[...] = xp_ref[:, tb:tb + hl, :]


def _conv_module(fat, hist32, w, cb, g, b, *, nb, tb):
    bsz, t, _ = fat.shape
    n_t = t // tb
    pre_blk = PREA_OFF // (2 * CONV_A_CH)
    hl = CONV_A_HALO
    in_specs = [pl.BlockSpec((nb, tb, 2 * CONV_A_CH), lambda bi, i: (bi, i, pre_blk))]
    args = [fat]
    if n_t > 1:
        per = tb // hl
        in_specs.append(pl.BlockSpec((nb, hl, 2 * CONV_A_CH),
                                     lambda bi, i: (bi, jnp.maximum(i * per - 1, 0), pre_blk)))
        args.append(fat)
    in_specs += [pl.BlockSpec((nb, hl, CONV_A_CH), lambda bi, i: (bi, 0, 0)),
                 pl.BlockSpec((CONV_A_WIDTH, CONV_A_CH), lambda bi, i: (0, 0)),
                 pl.BlockSpec((1, CONV_A_CH), lambda bi, i: (0, 0)),
                 pl.BlockSpec((1, CONV_A_CH), lambda bi, i: (0, 0)),
                 pl.BlockSpec((1, CONV_A_CH), lambda bi, i: (0, 0))]
    args += [hist32, w, cb, g, b]
    if n_t > 1:
        nbc, rc = 1, 64
    else:
        nbc, rc = 4, tb
    return pl.pallas_call(
        functools.partial(_conva_kernel, n_t=n_t, nbc=nbc, rc=rc),
        grid=(bsz // nb, n_t),
        in_specs=in_specs,
        out_specs=[pl.BlockSpec((nb, tb, CONV_A_CH), lambda bi, i: (bi, i, 0)),
                   pl.BlockSpec((nb, hl, CONV_A_CH), lambda bi, i: (bi, 0, 0))],
        out_shape=[jax.ShapeDtypeStruct((bsz, t, CONV_A_CH), BF16),
                   jax.ShapeDtypeStruct((bsz, hl, CONV_A_CH), F32)],
        scratch_shapes=[pltpu.VMEM((nb, hl + tb, CONV_A_CH), F32)],
        compiler_params=_cparams(("arbitrary", "arbitrary")),
        name="conv_module",
    )(*args)


GDN_ROWS = 256
GDN_SUB = 128


def _gdn_kernel(*refs, n_t, chunk):
    if n_t > 1:
        (qkv_ref, halo_ref, hist_ref, z_ref, small_ref, st_ref, cw_ref, alog_ref, dt_ref, ng_ref,
         y_ref, ho_ref, so_ref, xp_ref, s_ref, vn_ref, os_ref) = refs
    else:
        (qkv_ref, hist_ref, z_ref, small_ref, st_ref, cw_ref, alog_ref, dt_ref, ng_ref,
         y_ref, ho_ref, so_ref, xp_ref, s_ref, vn_ref, os_ref) = refs
    nb, tb, _ = qkv_ref.shape
    rows = nb * tb
    t = pl.program_id(1)
    hl = SUBLANES
    nk = GDN_HEADS * GDN_DK

    @pl.when(t == 0)
    def _():
        xp_ref[:, 0:hl, :] = hist_ref[...]
        s_ref[...] = st_ref[...]

    if n_t > 1:
        @pl.when(t > 0)
        def _():
            xp_ref[:, 0:hl, :] = halo_ref[...]

    xp_ref[:, hl:hl + tb, :] = qkv_ref[...]
    xp = xp_ref[...]
    acc = cw_ref[0:1, :] * xp
    for k in range(1, GDN_CONV_WIDTH):
        acc = cw_ref[k:k + 1, :] * xp + pltpu.roll(acc, 1, 1)
    c = _silu(acc[:, hl:hl + tb, :]).reshape(rows, GDN_QKV_WIDTH)
    ho_ref[...] = xp_ref[:, tb:tb + hl, :]

    sm = small_ref[...].reshape(rows, LANES)
    beta_all = _sigmoid(sm)
    xg = sm + dt_ref[...]
    softplus = jnp.maximum(xg, 0.0) + jnp.log1p(jnp.exp(-jnp.abs(xg)))
    g_all = -jnp.exp(alog_ref[...]) * softplus

    rs = GDN_SUB
    n_sub = rows // rs
    shift = int(math.log2(chunk))
    ri = lax.broadcasted_iota(jnp.int32, (rs, rs), 0)
    ci = lax.broadcasted_iota(jnp.int32, (rs, rs), 1)
    same = (ri >> shift) == (ci >> shift)
    tri = same & (ci <= ri)
    strict = same & (ci < ri)
    tri_b = jnp.where(tri, 1.0, 0.0).astype(BF16)
    triu_b = jnp.where(same & (ri <= ci), 1.0, 0.0).astype(BF16)
    same_b = jnp.where(same, 1.0, 0.0).astype(BF16)
    gc_col, gc_row, gl_col = [], [], []
    for s in range(n_sub):
        g3 = _split3(g_all[s * rs:(s + 1) * rs])
        gc_col.append(_dot(tri_b, g3[0]) + _dot(tri_b, g3[1]) + _dot(tri_b, g3[2]))
        gc_row.append(_dot_tn(g3[0], triu_b) + _dot_tn(g3[1], triu_b) + _dot_tn(g3[2], triu_b))
        gl_col.append(_dot(same_b, g3[0]) + _dot(same_b, g3[1]) + _dot(same_b, g3[2]))

    chains = [(s, h) for s in range(n_sub) for h in range(GDN_HEADS)]
    each = lambda f, *ls: [f(*xs) for xs in zip(*ls)]
    qs, ks, kbs, rhss, decays, betas, egs, gcs, gls = [], [], [], [], [], [], [], [], []
    for s, h in chains:
        r = slice(s * rs, (s + 1) * rs)
        qh = c[r, h * GDN_DK:(h + 1) * GDN_DK]
        kh = c[r, nk + h * GDN_DK:nk + (h + 1) * GDN_DK]
        vh = c[r, 2 * nk + h * GDN_DV:2 * nk + (h + 1) * GDN_DV]
        qh = qh * lax.rsqrt(jnp.sum(qh * qh, axis=-1, keepdims=True) + RMS_EPS) * (GDN_DK ** -0.5)
        kh = kh * lax.rsqrt(jnp.sum(kh * kh, axis=-1, keepdims=True) + RMS_EPS)
        beta = beta_all[r, BETA_LANE + h:BETA_LANE + h + 1]
        gc = gc_col[s][:, DEC_LANE + h:DEC_LANE + h + 1]
        gr = gc_row[s][DEC_LANE + h:DEC_LANE + h + 1, :]
        eg = jnp.exp(gc)
        qs.append(qh)
        ks.append(kh)
        kbs.append(kh.astype(BF16))
        betas.append(beta)
        gcs.append(gc)
        gls.append(gl_col[s][:, DEC_LANE + h:DEC_LANE + h + 1])
        egs.append(eg)
        decays.append(jnp.where(tri, jnp.exp(jnp.where(tri, gc - gr, 0.0)), 0.0))
        rhss.append(jnp.concatenate([vh * beta, kh * (beta * eg)], axis=1))
    kks = each(lambda kb: _dot_nt(kb, kb), kbs)
    a0s = each(lambda beta, kk, decay: jnp.where(strict, -(beta * kk * decay), 0.0), betas, kks, decays)
    ms = a0s
    pows = a0s
    for _ in range(1, shift):
        pows = each(lambda a: _dot(a.astype(BF16), a.astype(BF16)), pows)
        ms = each(lambda m, a: m + a + _dot(m.astype(BF16), a.astype(BF16)), ms, pows)
    mbs = each(lambda m: m.astype(BF16), ms)
    x0s = each(lambda rhs, mb: rhs + _dot(mb, rhs.astype(BF16)), rhss, mbs)
    ress = each(lambda rhs, x0, a0: rhs - x0 + _dot_split(a0, x0), rhss, x0s, a0s)
    sols = each(lambda x0, res, mb: x0 + res + _dot(mb, res.astype(BF16)), x0s, ress, mbs)
    qkms = each(lambda qh, kb, decay: jnp.where(tri, _dot_nt(qh.astype(BF16), kb) * decay, 0.0).astype(BF16),
                qs, kbs, decays)
    us = each(lambda sol: sol[:, :GDN_DV], sols)
    wbs = each(lambda sol: sol[:, GDN_DV:].astype(BF16), sols)
    qds = each(lambda qh, eg: (qh * eg).astype(BF16), qs, egs)
    kds = each(lambda kh, gl, gc: (kh * jnp.exp(gl - gc)).astype(BF16), ks, gls, gcs)
    egls = each(jnp.exp, gls)
    for j in range(rows // chunk):
        bi = (j * chunk) // tb
        s = (j * chunk) // rs
        r0 = j * chunk - s * rs
        for h in range(GDN_HEADS):
            i = s * GDN_HEADS + h
            sc = s_ref[bi, h]
            sb = sc.astype(BF16)
            vnew = us[i][r0:r0 + chunk] - _dot(wbs[i][r0:r0 + chunk], sb)
            os_ref[h, j * chunk:(j + 1) * chunk, :] = _dot(qds[i][r0:r0 + chunk], sb)
            vn_ref[h, j * chunk:(j + 1) * chunk, :] = vnew
            s_ref[bi, h] = sc * egls[i][r0:r0 + 1, :] + _dot_tn(kds[i][r0:r0 + chunk], vnew.astype(BF16))
    z = z_ref[...].reshape(rows, GDN_HEADS * GDN_DV)
    for i, (s, h) in enumerate(chains):
        r = slice(s * rs, (s + 1) * rs)
        o = os_ref[h, r, :] + _dot(qkms[i], vn_ref[h, r, :].astype(BF16))
        on = _rms(o, ng_ref[...])
        y = (on * _silu(z[r, h * GDN_DV:(h + 1) * GDN_DV])).astype(y_ref.dtype)
        if tb >= rs:
            y_ref[0, s * rs:(s + 1) * rs, h * GDN_DV:(h + 1) * GDN_DV] = y
        else:
            y_ref[s * (rs // tb):(s + 1) * (rs // tb), :, h * GDN_DV:(h + 1) * GDN_DV] = y.reshape(rs // tb, tb, GDN_DV)
    so_ref[...] = s_ref[...]


def _gated_deltanet(fat, hist8, state, cw, alog_row, dt_row, ng, *, nb, tb):
    bsz, t, _ = fat.shape
    n_t = t // tb
    chunk = min(CHUNK, t)
    hl = SUBLANES
    qkv_blk = QKV_OFF // GDN_QKV_WIDTH
    z_blk = Z_OFF // (GDN_HEADS * GDN_DV)
    small_blk = SMALL_OFF // LANES
    in_specs = [pl.BlockSpec((nb, tb, GDN_QKV_WIDTH), lambda bi, i: (bi, i, qkv_blk))]
    args = [fat]
    if n_t > 1:
        per = tb // hl
        in_specs.append(pl.BlockSpec((nb, hl, GDN_QKV_WIDTH), lambda bi, i: (bi, jnp.maximum(i * per - 1, 0), qkv_blk)))
        args.append(fat)
    in_specs += [pl.BlockSpec((nb, hl, GDN_QKV_WIDTH), lambda bi, i: (bi, 0, 0)),
                 pl.BlockSpec((nb, tb, GDN_HEADS * GDN_DV), lambda bi, i: (bi, i, z_blk)),
                 pl.BlockSpec((nb, tb, LANES), lambda bi, i: (bi, i, small_blk)),
                 pl.BlockSpec((nb, GDN_HEADS, GDN_DK, GDN_DV), lambda bi, i: (bi, 0, 0, 0)),
                 pl.BlockSpec((GDN_CONV_WIDTH, GDN_QKV_WIDTH), lambda bi, i: (0, 0)),
                 pl.BlockSpec((1, LANES), lambda bi, i: (0, 0)),
                 pl.BlockSpec((1, LANES), lambda bi, i: (0, 0)),
                 pl.BlockSpec((1, GDN_DV), lambda bi, i: (0, 0))]
    args += [hist8, fat, fat, state, cw, alog_row, dt_row, ng]
    rows = nb * tb
    return pl.pallas_call(
        functools.partial(_gdn_kernel, n_t=n_t, chunk=chunk),
        grid=(bsz // nb, n_t),
        in_specs=in_specs,
        out_specs=[pl.BlockSpec((nb, tb, GDN_HEADS * GDN_DV), lambda bi, i: (bi, i, 0)),
                   pl.BlockSpec((nb, hl, GDN_QKV_WIDTH), lambda bi, i: (bi, 0, 0)),
                   pl.BlockSpec((nb, GDN_HEADS, GDN_DK, GDN_DV), lambda bi, i: (bi, 0, 0, 0))],
        out_shape=[jax.ShapeDtypeStruct((bsz, t, GDN_HEADS * GDN_DV), BF16),
                   jax.ShapeDtypeStruct((bsz, hl, GDN_QKV_WIDTH), F32),
                   jax.ShapeDtypeStruct((bsz, GDN_HEADS, GDN_DK, GDN_DV), F32)],
        scratch_shapes=[pltpu.VMEM((nb, hl + tb, GDN_QKV_WIDTH), F32),
                        pltpu.VMEM((nb, GDN_HEADS, GDN_DK, GDN_DV), F32),
                        pltpu.VMEM((GDN_HEADS, rows, GDN_DV), F32),
                        pltpu.VMEM((GDN_HEADS, rows, GDN_DV), F32)],
        compiler_params=_cparams(("arbitrary", "arbitrary")),
        name="gated_deltanet",
    )(*args)


def _mla_prep_kernel(*refs, emit_kv, scale):
    if emit_kv:
        (blk_ref, cs_ref, sn_ref, qg_ref, kg_ref, wq_ref, wqs_ref, wk_ref, wvt_ref,
         q_ref, lat_ref, kpe_ref, k_ref, vt_ref) = refs
    else:
        (blk_ref, cs_ref, sn_ref, qg_ref, kg_ref, wq_ref, wqs_ref,
         q_ref, lat_ref, kpe_ref) = refs
    nb, tb, _ = blk_ref.shape
    rows = nb * tb
    blk = blk_ref[...]
    cs = cs_ref[...]
    sn = sn_ref[...]
    qn = _rms(blk[..., :MLA_Q_LORA], qg_ref[...]).reshape(rows, MLA_Q_LORA).astype(BF16)
    qf = _dot(qn, wq_ref[...]).reshape(nb, tb, MLA_HEADS * QK_PAD)
    qs = _dot(qn, wqs_ref[...]).reshape(nb, tb, MLA_HEADS * LANES)
    for h in range(MLA_HEADS):
        nope = qf[..., h * QK_PAD:h * QK_PAD + MLA_NOPE_DIM]
        rope = qf[..., h * QK_PAD + MLA_NOPE_DIM:(h + 1) * QK_PAD] * cs + qs[..., h * LANES:(h + 1) * LANES] * sn
        q_ref[h, :, :, 0:MLA_NOPE_DIM] = (nope * scale).astype(q_ref.dtype)
        q_ref[h, :, :, MLA_NOPE_DIM:QK_PAD] = (rope * scale).astype(q_ref.dtype)

    lat = _rms(blk[..., MLA_Q_LORA + LANES:], kg_ref[...])
    lat_ref[...] = lat

    small = blk[..., MLA_Q_LORA:MLA_Q_LORA + LANES].reshape(rows, LANES)
    half = MLA_ROPE_DIM // 2
    lane = lax.broadcasted_iota(jnp.int32, (rows, LANES), 1)
    swapped = jnp.where(lane < half, pltpu.roll(small, LANES - half, 1), pltpu.roll(small, half, 1))
    kpe = small.reshape(nb, tb, LANES) * cs + swapped.reshape(nb, tb, LANES) * sn
    kpe_ref[...] = kpe[..., :MLA_ROPE_DIM]

    if emit_kv:
        latb = lat.reshape(rows, MLA_KV_LORA).astype(BF16)
        kn = _dot(latb, wk_ref[...]).reshape(nb, tb, MLA_HEADS * MLA_NOPE_DIM)
        ones_rows = jnp.where(lax.broadcasted_iota(jnp.int32, (VT_ROWS - MLA_V_DIM, ATTN_KB), 0) == 0,
                              1.0, 0.0).astype(vt_ref.dtype)
        for h in range(MLA_HEADS):
            k_ref[h, :, :, 0:MLA_NOPE_DIM] = kn[..., h * MLA_NOPE_DIM:(h + 1) * MLA_NOPE_DIM].astype(k_ref.dtype)
            k_ref[h, :, :, MLA_NOPE_DIM:QK_PAD] = kpe.astype(k_ref.dtype)
            vt = _dot_nt(wvt_ref[h], latb).astype(vt_ref.dtype)
            for kb in range(rows // ATTN_KB):
                vt_ref[h, 0, kb, 0:MLA_V_DIM, :] = vt[:, kb * ATTN_KB:(kb + 1) * ATTN_KB]
                vt_ref[h, 0, kb, MLA_V_DIM:VT_ROWS, :] = ones_rows


def _mla_prep(fat, cs, sn, qg, kg, wq, wqs, wk, wvt, *, nb, tb, emit_kv, scale):
    bsz, t, _ = fat.shape
    blk = QLAT_OFF // MLA_BLK_W
    hq = MLA_HEADS
    in_specs = [pl.BlockSpec((nb, tb, MLA_BLK_W), lambda bi, i: (bi, i, blk)),
                pl.BlockSpec((tb, LANES), lambda bi, i: (i, 0)),
                pl.BlockSpec((tb, LANES), lambda bi, i: (i, 0)),
                pl.BlockSpec((1, MLA_Q_LORA), lambda bi, i: (0, 0)),
                pl.BlockSpec((1, MLA_KV_LORA), lambda bi, i: (0, 0)),
                pl.BlockSpec(wq.shape, lambda bi, i: (0, 0)),
                pl.BlockSpec(wqs.shape, lambda bi, i: (0, 0))]
    args = [fat, cs, sn, qg, kg, wq, wqs]
    out_specs = [pl.BlockSpec((hq, nb, tb, QK_PAD), lambda bi, i: (0, bi, i, 0)),
                 pl.BlockSpec((nb, tb, MLA_KV_LORA), lambda bi, i: (bi, i, 0)),
                 pl.BlockSpec((nb, tb, MLA_ROPE_DIM), lambda bi, i: (bi, i, 0))]
    out_shape = [jax.ShapeDtypeStruct((hq, bsz, t, QK_PAD), BF16),
                 jax.ShapeDtypeStruct((bsz, t, MLA_KV_LORA), F32),
                 jax.ShapeDtypeStruct((bsz, t, MLA_ROPE_DIM), F32)]
    if emit_kv:
        assert nb == 1 and tb % ATTN_KB == 0
        in_specs += [pl.BlockSpec(wk.shape, lambda bi, i: (0, 0)),
                     pl.BlockSpec(wvt.shape, lambda bi, i: (0, 0, 0))]
        args += [wk, wvt]
        per = tb // ATTN_KB
        out_specs += [pl.BlockSpec((hq, nb, tb, QK_PAD), lambda bi, i: (0, bi, i, 0)),
                      pl.BlockSpec((hq, nb, per, VT_ROWS, ATTN_KB), lambda bi, i: (0, bi, i, 0, 0))]
        out_shape += [jax.ShapeDtypeStruct((hq, bsz, t, QK_PAD), BF16),
                      jax.ShapeDtypeStruct((hq, bsz, t // ATTN_KB, VT_ROWS, ATTN_KB), BF16)]
    return pl.pallas_call(
        functools.partial(_mla_prep_kernel, emit_kv=emit_kv, scale=scale),
        grid=(bsz // nb, t // tb),
        in_specs=in_specs, out_specs=out_specs, out_shape=out_shape,
        compiler_params=_cparams(("arbitrary", "arbitrary")),
        name="mla_prep",
    )(*args)


def _attn_prompt_kernel(q_ref, k_ref, vt_ref, o_ref, m_ref, acc_ref, st_ref, *, nq):
    kb = ATTN_KB
    j = pl.program_id(2)
    m_ref[...] = jnp.full(m_ref.shape, -jnp.inf, F32)
    acc_ref[...] = jnp.zeros(acc_ref.shape, F32)

    def scores(chains, ki):
        k0 = pl.multiple_of(ki * kb, kb)
        kblk = k_ref[0, 0, pl.ds(k0, kb), :]
        return [_dot_nt(kblk, q_ref[0, 0, c * kb:(c + 1) * kb, :]) for c in chains]

    def finish(chains, sts, ki, masked_chain):
        pts, alphas = [], []
        for c, st in zip(chains, sts):
            if c == masked_chain:
                key_chunk = lax.broadcasted_iota(jnp.int32, (kb, kb), 0) // CHUNK
                qry_chunk = lax.broadcasted_iota(jnp.int32, (kb, kb), 1) // CHUNK
                st = jnp.where(key_chunk <= qry_chunk, st, -jnp.inf)
            m_old = m_ref[c]
            m_new = jnp.maximum(m_old, jnp.max(st, axis=0, keepdims=True))
            pts.append(jnp.exp2(st - m_new).astype(BF16))
            alphas.append(jnp.exp2(m_old - m_new))
            m_ref[c] = m_new
        vblk = vt_ref[0, 0, ki]
        for c, pt, alpha in zip(chains, pts, alphas):
            acc_ref[c] = alpha * acc_ref[c] + _dot(vblk, pt)

    everyone = range(nq)
    for c, st in zip(everyone, scores(everyone, 0)):
        st_ref[c] = st

    def body(ki, carry):
        cur = [st_ref[c] for c in everyone]
        nxt = scores(everyone, ki + 1)
        finish(everyone, cur, ki, None)
        for c in everyone:
            st_ref[c] = nxt[c]
        return carry

    lax.fori_loop(0, nq * j, body, 0)
    finish(everyone, [st_ref[c] for c in everyone], nq * j, 0)
    for d in range(1, nq):
        rest = range(d, nq)
        finish(rest, scores(rest, nq * j + d), nq * j + d, d)
    for c in range(nq):
        acc = acc_ref[c]
        ot = acc[0:MLA_V_DIM] / acc[MLA_V_DIM:MLA_V_DIM + 1]
        o_ref[0, c * kb:(c + 1) * kb, :] = ot.T.astype(o_ref.dtype)


def _attn_prompt(q, k, vt, *, nq):
    hq, bsz, t, _ = q.shape
    tq = ATTN_KB * nq
    return pl.pallas_call(
        functools.partial(_attn_prompt_kernel, nq=nq),
        grid=(bsz, hq, t // tq),
        in_specs=[pl.BlockSpec((1, 1, tq, QK_PAD), lambda b, h, i: (h, b, i, 0)),
                  pl.BlockSpec((1, 1, t, QK_PAD), lambda b, h, i: (h, b, 0, 0)),
                  pl.BlockSpec((1, 1, t // ATTN_KB, VT_ROWS, ATTN_KB), lambda b, h, i: (h, b, 0, 0, 0))],
        out_specs=pl.BlockSpec((1, tq, MLA_V_DIM), lambda b, h, i: (b, i, h)),
        out_shape=jax.ShapeDtypeStruct((bsz, t, MLA_HEADS * MLA_V_DIM), BF16),
        scratch_shapes=[pltpu.VMEM((nq, 1, ATTN_KB), F32), pltpu.VMEM((nq, VT_ROWS, ATTN_KB), F32),
                        pltpu.VMEM((nq, ATTN_KB, ATTN_KB), F32)],
        compiler_params=_cparams(("arbitrary", "arbitrary", "arbitrary")),
        name="attn_prompt",
    )(q, k, vt)


def _attn_sample_kernel(q_ref, latc_ref, kpec_ref, latn_ref, kpen_ref, wuk_ref, wuv_ref, o_ref):
    tn = latn_ref.shape[1]
    past = latc_ref.shape[1]
    hq = MLA_HEADS
    qabs = jnp.concatenate(
        [_dot(q_ref[h, 0, :, 0:MLA_NOPE_DIM], wuk_ref[h]) for h in range(hq)], axis=0).astype(BF16)
    qrope = jnp.concatenate([q_ref[h, 0, :, MLA_NOPE_DIM:MLA_NOPE_DIM + MLA_ROPE_DIM] for h in range(hq)], axis=0)
    latc = latc_ref[0].astype(BF16)
    latn = latn_ref[0].astype(BF16)
    sc = _dot_nt(qabs, latc) + _dot_nt(qrope, kpec_ref[0].astype(BF16))
    sn = _dot_nt(qabs, latn) + _dot_nt(qrope, kpen_ref[0].astype(BF16))
    rows = hq * tn
    qpos = past + lax.broadcasted_iota(jnp.int32, (rows, 1), 0) % tn
    kc_pos = lax.broadcasted_iota(jnp.int32, (rows, past), 1)
    kn_pos = past + lax.broadcasted_iota(jnp.int32, (rows, tn), 1)
    sc = jnp.where(kc_pos // CHUNK <= qpos // CHUNK, sc, -jnp.inf)
    sn = jnp.where(kn_pos // CHUNK <= qpos // CHUNK, sn, -jnp.inf)
    m = jnp.maximum(jnp.max(sc, axis=-1, keepdims=True), jnp.max(sn, axis=-1, keepdims=True))
    pc = jnp.exp(sc - m)
    pn = jnp.exp(sn - m)
    l = jnp.sum(pc, axis=-1, keepdims=True) + jnp.sum(pn, axis=-1, keepdims=True)
    olat = (_dot(pc.astype(BF16), latc) + _dot(pn.astype(BF16), latn)) / l
    for h in range(hq):
        oh = _dot(olat[h * tn:(h + 1) * tn].astype(BF16), wuv_ref[h])
        o_ref[0, :, h * MLA_V_DIM:(h + 1) * MLA_V_DIM] = oh.astype(o_ref.dtype)


def _attn_sample(q, cache_lat, cache_kpe, lat_new, kpe_new, wuk_t, wuv):
    hq, bsz, tn, _ = q.shape
    past = cache_lat.shape[1]
    return pl.pallas_call(
        _attn_sample_kernel,
        grid=(bsz,),
        in_specs=[pl.BlockSpec((hq, 1, tn, QK_PAD), lambda b: (0, b, 0, 0)),
                  pl.BlockSpec((1, past, MLA_KV_LORA), lambda b: (b, 0, 0)),
                  pl.BlockSpec((1, past, MLA_ROPE_DIM), lambda b: (b, 0, 0)),
                  pl.BlockSpec((1, tn, MLA_KV_LORA), lambda b: (b, 0, 0)),
                  pl.BlockSpec((1, tn, MLA_ROPE_DIM), lambda b: (b, 0, 0)),
                  pl.BlockSpec(wuk_t.shape, lambda b: (0, 0, 0)),
                  pl.BlockSpec(wuv.shape, lambda b: (0, 0, 0))],
        out_specs=pl.BlockSpec((1, tn, MLA_HEADS * MLA_V_DIM), lambda b: (b, 0, 0)),
        out_shape=jax.ShapeDtypeStruct((bsz, tn, MLA_HEADS * MLA_V_DIM), BF16),
        compiler_params=_cparams(("arbitrary",)),
        name="attn_sample",
    )(q, cache_lat, cache_kpe, lat_new, kpe_new, wuk_t, wuv)


def _merge_kernel(ya_ref, yb_ref, yc_ref, gate_ref, x_ref, g1_ref, g0_ref, b0_ref, wb_ref, wo_ref,
                  lg_ref, lb_ref, o_ref, *, ln0, alpha):
    nb, tb, d = x_ref.shape
    rows = nb * tb
    x = x_ref[...]
    if ln0:
        x = _ln(x, g0_ref[...], b0_ref[...])
    merged = None
    for i, y_ref in enumerate((ya_ref, yb_ref, yc_ref)):
        proj = _dot(y_ref[...].reshape(rows, BRANCH_WIDTH), wb_ref[i])
        gate = _sigmoid(gate_ref[:, :, i * d:(i + 1) * d]).reshape(rows, d)
        merged = gate * proj if merged is None else merged + gate * proj
    out = _dot(merged.astype(BF16), wo_ref[...]).reshape(nb, tb, d)
    o_ref[...] = _ln(alpha * x + (1.0 + g1_ref[...]) * out, lg_ref[...], lb_ref[...])


def _merge(ya, yb, yc, fat, x, g1, g0, b0, wb, wo, lg, lb, *, nb, tb, ln0, alpha):
    bsz, t, d = x.shape
    row = lambda bi, i: (bi, i, 0)
    const2 = lambda bi, i: (0, 0)
    return pl.pallas_call(
        functools.partial(_merge_kernel, ln0=ln0, alpha=alpha),
        grid=(bsz // nb, t // tb),
        in_specs=[pl.BlockSpec((nb, tb, BRANCH_WIDTH), row),
                  pl.BlockSpec((nb, tb, BRANCH_WIDTH), row),
                  pl.BlockSpec((nb, tb, BRANCH_WIDTH), row),
                  pl.BlockSpec((nb, tb, GATE_W), row),
                  pl.BlockSpec((nb, tb, d), row),
                  pl.BlockSpec((nb, 1, d), lambda bi, i: (bi, 0, 0)),
                  pl.BlockSpec((1, d), const2),
                  pl.BlockSpec((1, d), const2),
                  pl.BlockSpec(wb.shape, lambda bi, i: (0, 0, 0)),
                  pl.BlockSpec(wo.shape, const2),
                  pl.BlockSpec((1, d), const2),
                  pl.BlockSpec((1, d), const2)],
        out_specs=pl.BlockSpec((nb, tb, d), row),
        out_shape=jax.ShapeDtypeStruct((bsz, t, d), F32),
        compiler_params=_cparams(("arbitrary", "arbitrary")),
        name="merge",
    )(ya, yb, yc, fat, x, g1, g0, b0, wb, wo, lg, lb)


FFN_BLOCK = 1408


def _ffn_kernel(x_ref, sc_ref, sh_ref, g2_ref, wa_ref, wv_ref, wd_ref, cw_ref, cb_ref, lg_ref, lb_ref, hist_ref,
                o_ref, ho_ref, h_ref, acc_ref, carry_ref, ap_ref, *, alpha):
    nb, tb, d = x_ref.shape
    rows = nb * tb
    t = pl.program_id(1)
    f = pl.program_id(2)
    n_f = pl.num_programs(2)
    hl = SUBLANES
    fb = wa_ref.shape[1]

    @pl.when(f == 0)
    def _():
        h = x_ref[...] * (1.0 + sc_ref[...]) + sh_ref[...]
        h_ref[...] = h.reshape(rows, d).astype(BF16)

    @pl.when(t == 0)
    def _():
        carry_ref[f] = hist_ref[...]

    h = h_ref[...]
    a = _dot(h, wa_ref[...]).reshape(nb, tb, fb)
    v = _dot(h, wv_ref[...]).reshape(nb, tb, fb)
    ap_ref[:, 0:hl, :] = carry_ref[f]
    ap_ref[:, hl:hl + tb, :] = a
    ap = ap_ref[...]
    conv = cw_ref[0:1, :] * ap
    for k in range(1, FFN_CONV_WIDTH):
        conv = cw_ref[k:k + 1, :] * ap + pltpu.roll(conv, 1, 1)
    act = _silu(conv[:, hl:hl + tb, :] + cb_ref[...]) * v
    contrib = _dot(act.reshape(rows, fb).astype(BF16), wd_ref[...])
    last = ap_ref[:, tb:tb + hl, :]
    carry_ref[f] = last
    ho_ref[0] = last

    @pl.when(f == 0)
    def _():
        acc_ref[...] = contrib

    @pl.when(f > 0)
    def _():
        acc_ref[...] = acc_ref[...] + contrib

    @pl.when(f == n_f - 1)
    def _():
        y = acc_ref[...].reshape(nb, tb, d)
        o_ref[...] = _ln(alpha * x_ref[...] + (1.0 + g2_ref[...]) * y, lg_ref[...], lb_ref[...])


def _ffn(x, sc, sh, g2, wa, wv, wd, cw, cb, lg, lb, hist8, *, nb, tb, alpha):
    bsz, t, d = x.shape
    dff = wa.shape[1]
    fb = FFN_BLOCK
    n_f = dff // fb
    hl = SUBLANES
    rows = nb * tb
    row = lambda bi, i, f: (bi, i, 0)
    mod = lambda bi, i, f: (bi, 0, 0)
    const2 = lambda bi, i, f: (0, 0)
    return pl.pallas_call(
        functools.partial(_ffn_kernel, alpha=alpha),
        grid=(bsz // nb, t // tb, n_f),
        in_specs=[pl.BlockSpec((nb, tb, d), row),
                  pl.BlockSpec((nb, 1, d), mod),
                  pl.BlockSpec((nb, 1, d), mod),
                  pl.BlockSpec((nb, 1, d), mod),
                  pl.BlockSpec((d, fb), lambda bi, i, f: (0, f)),
                  pl.BlockSpec((d, fb), lambda bi, i, f: (0, f)),
                  pl.BlockSpec((fb, d), lambda bi, i, f: (f, 0)),
                  pl.BlockSpec((FFN_CONV_WIDTH, fb), lambda bi, i, f: (0, f)),
                  pl.BlockSpec((1, fb), lambda bi, i, f: (0, f)),
                  pl.BlockSpec((1, d), const2),
                  pl.BlockSpec((1, d), const2),
                  pl.BlockSpec((nb, hl, fb), lambda bi, i, f: (bi, 0, f))],
        out_specs=[pl.BlockSpec((nb, tb, d), row),
                   pl.BlockSpec((1, nb, hl, fb), lambda bi, i, f: (i, bi, 0, f))],
        out_shape=[jax.ShapeDtypeStruct((bsz, t, d), F32),
                   jax.ShapeDtypeStruct((t // tb, bsz, hl, dff), F32)],
        scratch_shapes=[pltpu.VMEM((rows, d), BF16),
                        pltpu.VMEM((rows, d), F32),
                        pltpu.VMEM((n_f, nb, hl, fb), F32),
                        pltpu.VMEM((nb, hl + tb, fb), F32)],
        compiler_params=_cparams(("arbitrary", "arbitrary", "arbitrary")),
        name="conv_ffn",
    )(x, sc, sh, g2, wa, wv, wd, cw, cb, lg, lb, hist8)


def _front_pad(hist, rows):
    b, r, c = hist.shape
    return jnp.concatenate([jnp.zeros((b, rows - r, c), hist.dtype), hist], axis=1)


def _rope_rows(pos):
    half = MLA_ROPE_DIM // 2
    inv_freq = ROPE_THETA ** (-jnp.arange(half, dtype=F32) / half)
    ang = pos.astype(F32)[:, None] * inv_freq[None, :]
    cos, sin = jnp.cos(ang), jnp.sin(ang)
    pad = jnp.zeros((pos.shape[0], LANES - MLA_ROPE_DIM), F32)
    return jnp.concatenate([cos, cos, pad], axis=1), jnp.concatenate([-sin, sin, pad], axis=1)


def _lane_row(vals, lane0):
    return jnp.zeros((1, LANES), F32).at[0, lane0:lane0 + vals.shape[0]].set(vals.astype(F32))


def _layer_weights(l, w_in, mla_w_uq, mla_w_ukv, w_branch, w_out, w_up, w_down):
    d = D_MODEL
    o = np.cumsum([0, 2 * CONV_A_CH, GDN_QKV_WIDTH, GDN_HEADS * GDN_DV, GDN_HEADS, GDN_HEADS,
                   MLA_Q_LORA, MLA_KV_LORA, MLA_ROPE_DIM, N_BRANCH * D_MODEL]).tolist()
    w = w_in[l]
    seg = lambda i: w[:, o[i]:o[i + 1]]
    pad = jnp.zeros((d, LANES - MLA_ROPE_DIM - 2 * GDN_HEADS), w.dtype)
    w_fat = jnp.concatenate([seg(8), seg(1), seg(2), seg(0), seg(5), seg(7), seg(3), seg(4), pad, seg(6)],
                            axis=1).astype(BF16)
    hd = MLA_NOPE_DIM + MLA_ROPE_DIM
    half = MLA_ROPE_DIM // 2
    wq = mla_w_uq[l]
    zq = jnp.zeros((MLA_Q_LORA, QK_PAD - hd), wq.dtype)
    zs = jnp.zeros((MLA_Q_LORA, LANES - MLA_ROPE_DIM), wq.dtype)
    wq_cols, wqs_cols = [], []
    for h in range(MLA_HEADS):
        wq_cols += [wq[:, h * hd:(h + 1) * hd], zq]
        r0 = h * hd + MLA_NOPE_DIM
        wqs_cols += [wq[:, r0 + half:r0 + 2 * half], wq[:, r0:r0 + half], zs]
    wkv = mla_w_ukv[l]
    kvw = MLA_NOPE_DIM + MLA_V_DIM
    wuk_t = jnp.stack([wkv[:, h * kvw:h * kvw + MLA_NOPE_DIM].T for h in range(MLA_HEADS)]).astype(BF16)
    wuv = jnp.stack([wkv[:, h * kvw + MLA_NOPE_DIM:(h + 1) * kvw] for h in range(MLA_HEADS)]).astype(BF16)
    return dict(
        w_fat=w_fat,
        wq=jnp.concatenate(wq_cols, axis=1).astype(BF16),
        wqs=jnp.concatenate(wqs_cols, axis=1).astype(BF16),
        wk=jnp.concatenate([wkv[:, h * kvw:h * kvw + MLA_NOPE_DIM] for h in range(MLA_HEADS)], axis=1).astype(BF16),
        wvt=jnp.transpose(wuv, (0, 2, 1)), wuk_t=wuk_t, wuv=wuv,
        wb=w_branch[l].astype(BF16), wo=w_out[l].astype(BF16),
        wa=w_up[l][:, :D_FF].astype(BF16), wv=w_up[l][:, D_FF:].astype(BF16), wd=w_down[l].astype(BF16),
    )


def _run_layer(x, mods, st, wts, p, *, ln0, g0, b0, alpha, cs, sn, tiles, prompt):
    bsz, t, d = x.shape
    sh1, sc1, g1, sh2, sc2, g2 = mods
    nb, tb, gdn_nb, gdn_tb = tiles
    if st is None:
        hist_a = jnp.zeros((bsz, CONV_A_WIDTH - 1, CONV_A_CH), F32)
        hist_b = jnp.zeros((bsz, GDN_CONV_WIDTH - 1, GDN_QKV_WIDTH), F32)
        state_b = jnp.zeros((bsz, GDN_HEADS, GDN_DK, GDN_DV), F32)
        hist_f = jnp.zeros((bsz, FFN_CONV_WIDTH - 1, D_FF), F32)
    else:
        cache_lat, cache_kpe, hist_a, hist_b, state_b, hist_f = st
    row = lambda v: v.reshape(1, -1)

    fat = _inproj(x, sc1, sh1, g0, b0, wts['w_fat'], nb=nb, tb=tb, ln0=ln0)
    y_a, ha = _conv_module(fat, _front_pad(hist_a, CONV_A_HALO), p['conv_a_w'], row(p['conv_a_b']),
                           row(p['ln_a_g']), row(p['ln_a_b']), nb=nb, tb=tb)
    y_b, hb, sb = _gated_deltanet(fat, _front_pad(hist_b, SUBLANES), state_b, p['gdn_conv_w'],
                                  _lane_row(p['gdn_a_log'], DEC_LANE), _lane_row(p['gdn_dt_bias'], DEC_LANE),
                                  row(p['gdn_norm_g']), nb=gdn_nb, tb=gdn_tb)
    if prompt:
        q, lat, kpe, k, v = _mla_prep(fat, cs, sn, row(p['mla_q_norm_g']), row(p['mla_kv_norm_g']),
                                      wts['wq'], wts['wqs'], wts['wk'], wts['wvt'], nb=nb, tb=tb, emit_kv=True,
                                      scale=ATTN_SCALE * math.log2(math.e))
        y_c = _attn_prompt(q, k, v, nq=math.gcd(ATTN_CHAINS, t // ATTN_KB))
    else:
        q, lat, kpe = _mla_prep(fat, cs, sn, row(p['mla_q_norm_g']), row(p['mla_kv_norm_g']),
                                wts['wq'], wts['wqs'], None, None, nb=nb, tb=tb, emit_kv=False, scale=ATTN_SCALE)
        y_c = _attn_sample(q, cache_lat, cache_kpe, lat, kpe, wts['wuk_t'], wts['wuv'])
    x1 = _merge(y_a, y_b, y_c, fat, x, g1, g0, b0, wts['wb'], wts['wo'], row(p['ln1_g']), row(p['ln1_b']),
                nb=nb, tb=tb, ln0=ln0, alpha=alpha)
    x2, hf = _ffn(x1, sc2, sh2, g2, wts['wa'], wts['wv'], wts['wd'], p['ffn_conv_w'], row(p['ffn_conv_b']),
                  row(p['ln2_g']), row(p['ln2_b']), _front_pad(hist_f, SUBLANES), nb=nb, tb=tb, alpha=alpha)
    states = (lat, kpe, ha[:, CONV_A_HALO - (CONV_A_WIDTH - 1):], hb[:, SUBLANES - (GDN_CONV_WIDTH - 1):],
              sb, hf[-1, :, SUBLANES - (FFN_CONV_WIDTH - 1):])
    return x2, states


def kernel(x_prompt, x_sample, cache_mla_latent, cache_mla_kpe, state_conv_a, state_gdn_conv, state_gdn, state_ffn_conv, c_prompt, c_sample, ln0_g, ln0_b, w_ada, b_ada, w_in, conv_a_w, conv_a_b, ln_a_g, ln_a_b, gdn_conv_w, gdn_a_log, gdn_dt_bias, gdn_norm_g, mla_q_norm_g, mla_kv_norm_g, mla_w_uq, mla_w_ukv, w_branch, w_out, ln1_g, ln1_b, w_up, ffn_conv_w, ffn_conv_b, w_down, ln2_g, ln2_b):
    depth = w_ada.shape[0]
    bp, seq, d = x_prompt.shape
    bs, dec_seq, _ = x_sample.shape
    past = cache_mla_latent.shape[2]
    alpha = (2 * depth) ** 0.25

    n_c = bp + bs
    c_rows = -(-n_c // SUBLANES) * SUBLANES
    c_all = jnp.concatenate([c_prompt, c_sample, jnp.zeros((c_rows - n_c, d), F32)], axis=0)
    mod = _modulation(c_all, w_ada, b_ada)

    cs_p, sn_p = _rope_rows(jnp.arange(seq, dtype=jnp.int32))
    cs_s, sn_s = _rope_rows(past + jnp.arange(dec_seq, dtype=jnp.int32))
    g0, b0 = ln0_g.reshape(1, d), ln0_b.reshape(1, d)

    tile_p = min(512, seq)
    tiles_p = (1, tile_p, 1, min(GDN_ROWS, seq))
    tiles_s = (bs, dec_seq, GDN_ROWS // dec_seq, dec_seq)

    xp, xs = x_prompt, x_sample
    p_states, s_states = [], []
    for l in range(depth):
        p = {'conv_a_w': conv_a_w[l], 'conv_a_b': conv_a_b[l], 'ln_a_g': ln_a_g[l], 'ln_a_b': ln_a_b[l],
             'gdn_conv_w': gdn_conv_w[l], 'gdn_a_log': gdn_a_log[l], 'gdn_dt_bias': gdn_dt_bias[l],
             'gdn_norm_g': gdn_norm_g[l], 'mla_q_norm_g': mla_q_norm_g[l], 'mla_kv_norm_g': mla_kv_norm_g[l],
             'ln1_g': ln1_g[l], 'ln1_b': ln1_b[l], 'ffn_conv_w': ffn_conv_w[l], 'ffn_conv_b': ffn_conv_b[l],
             'ln2_g': ln2_g[l], 'ln2_b': ln2_b[l]}
        wts = _layer_weights(l, w_in, mla_w_uq, mla_w_ukv, w_branch, w_out, w_up, w_down)
        mods_p = [m.reshape(bp, 1, d) for m in jnp.split(mod[l, :bp], 6, axis=-1)]
        mods_s = [m.reshape(bs, 1, d) for m in jnp.split(mod[l, bp:n_c], 6, axis=-1)]
        xp, st_p = _run_layer(xp, mods_p, None, wts, p, ln0=(l == 0), g0=g0, b0=b0, alpha=alpha,
                              cs=cs_p, sn=sn_p, tiles=tiles_p, prompt=True)
        st_in = (cache_mla_latent[l], cache_mla_kpe[l], state_conv_a[l], state_gdn_conv[l], state_gdn[l],
                 state_ffn_conv[l])
        xs, st_s = _run_layer(xs, mods_s, st_in, wts, p, ln0=(l == 0), g0=g0, b0=b0, alpha=alpha,
                              cs=cs_s, sn=sn_s, tiles=tiles_s, prompt=False)
        p_states.append(st_p)
        s_states.append(st_s)
    p_out = [jnp.stack(z, axis=0) for z in zip(*p_states)]
    s_out = [jnp.stack(z, axis=0) for z in zip(*s_states)]
    return (xp, xs, *p_out, *s_out)
```

```python
import functools
import math

import jax
import jax.numpy as jnp
import numpy as np
from jax import lax
from jax.experimental import pallas as pl
from jax.experimental.pallas import tpu as pltpu

F32 = jnp.float32
BF16 = jnp.bfloat16

D_MODEL = 1024
CHUNK = 64
CONV_A_CH = 512
CONV_A_WIDTH = 31
GDN_HEADS = 4
GDN_DK = 128
GDN_DV = 128
GDN_CONV_WIDTH = 4
GDN_QKV_WIDTH = 2 * GDN_HEADS * GDN_DK + GDN_HEADS * GDN_DV
MLA_HEADS = 4
MLA_Q_LORA = 384
MLA_KV_LORA = 256
MLA_NOPE_DIM = 128
MLA_ROPE_DIM = 64
MLA_V_DIM = 128
ROPE_THETA = 10000.0
N_BRANCH = 3
BRANCH_WIDTH = 512
D_FF = 2816
FFN_CONV_WIDTH = 3
LN_EPS = 1e-5
RMS_EPS = 1e-6

LANES = 128
SUBLANES = 8
VMEM_LIMIT = 48 * 1024 * 1024

QKV_OFF = 0
QLAT_OFF = QKV_OFF + GDN_QKV_WIDTH
SMALL_OFF = QLAT_OFF + MLA_Q_LORA
KVLAT_OFF = SMALL_OFF + LANES
FAT32_W = KVLAT_OFF + MLA_KV_LORA
MLA_BLK_W = FAT32_W - QLAT_OFF
GATE_OFF, GATE_W = 0, N_BRANCH * D_MODEL
PREA_OFF = GATE_OFF + GATE_W
Z_OFF = PREA_OFF + 2 * CONV_A_CH
FAT16_W = Z_OFF + GDN_HEADS * GDN_DV
INPROJ_TN = FAT32_W
BETA_LANE = MLA_ROPE_DIM
DEC_LANE = MLA_ROPE_DIM + GDN_HEADS
QK_PAD = 2 * LANES
ATTN_KB = 256
ATTN_CHAINS = 8
VT_ROWS = MLA_V_DIM + 16
ATTN_SCALE = (MLA_NOPE_DIM + MLA_ROPE_DIM) ** -0.5


def _cparams(sem):
    return pltpu.CompilerParams(dimension_semantics=sem, vmem_limit_bytes=VMEM_LIMIT)


def _sigmoid(x):
    return jax.nn.sigmoid(x)


def _silu(x):
    return x * jax.nn.sigmoid(x)


def _ln(x, g, b):
    mu = jnp.mean(x, axis=-1, keepdims=True)
    xc = x - mu
    var = jnp.mean(xc * xc, axis=-1, keepdims=True)
    return xc * lax.rsqrt(var + LN_EPS) * g + b


def _rms(x, g):
    return x * lax.rsqrt(jnp.mean(x * x, axis=-1, keepdims=True) + RMS_EPS) * g


def _dot(a, b):
    return jnp.dot(a, b, preferred_element_type=F32)


def _dot_nt(a, b):
    return lax.dot_general(a, b, (((1,), (1,)), ((), ())), preferred_element_type=F32)


def _dot_tn(a, b):
    return lax.dot_general(a, b, (((0,), (0,)), ((), ())), preferred_element_type=F32)


def _dot_split(a, b):
    ah = a.astype(BF16)
    al = (a - ah.astype(F32)).astype(BF16)
    bh = b.astype(BF16)
    bl = (b - bh.astype(F32)).astype(BF16)
    return _dot(ah, bh) + _dot(ah, bl) + _dot(al, bh)


def _split3(x):
    hi = x.astype(BF16)
    r = x - hi.astype(F32)
    mid = r.astype(BF16)
    lo = (r - mid.astype(F32)).astype(BF16)
    return hi, mid, lo


def _mod_kernel(c_ref, w_ref, b_ref, o_ref):
    s = _silu(c_ref[...])
    o_ref[0] = _dot(s.astype(BF16), w_ref[0].astype(BF16)) + b_ref[0]


def _modulation(c_all, w_ada, b_ada):
    depth, d, n = w_ada.shape
    rows = c_all.shape[0]
    tn = 1536
    return pl.pallas_call(
        _mod_kernel,
        grid=(depth, n // tn),
        in_specs=[pl.BlockSpec((rows, d), lambda l, j: (0, 0)),
                  pl.BlockSpec((1, d, tn), lambda l, j: (l, 0, j)),
                  pl.BlockSpec((1, 1, tn), lambda l, j: (l, 0, j))],
        out_specs=pl.BlockSpec((1, rows, tn), lambda l, j: (l, 0, j)),
        out_shape=jax.ShapeDtypeStruct((depth, rows, n), F32),
        compiler_params=_cparams(("arbitrary", "arbitrary")),
        name="modulation",
    )(c_all, w_ada, b_ada.reshape(depth, 1, n))


def _inproj_kernel(x_ref, sc_ref, sh_ref, g0_ref, b0_ref, w_ref, o_ref, *, ln0):
    nb, tb, d = x_ref.shape
    x = x_ref[...]
    if ln0:
        x = _ln(x, g0_ref[...], b0_ref[...])
    h = x * (1.0 + sc_ref[...]) + sh_ref[...]
    o = _dot(h.reshape(nb * tb, d).astype(BF16), w_ref[...])
    o_ref[...] = o.reshape(nb, tb, o.shape[-1]).astype(o_ref.dtype)


def _inproj(x, sc, sh, g0, b0, w_fat, out_dtype, *, nb, tb, ln0):
    bsz, t, d = x.shape
    tn = INPROJ_TN
    width = w_fat.shape[1]
    return pl.pallas_call(
        functools.partial(_inproj_kernel, ln0=ln0),
        grid=(width // tn, bsz // nb, t // tb),
        in_specs=[pl.BlockSpec((nb, tb, d), lambda n, b, i: (b, i, 0)),
                  pl.BlockSpec((nb, 1, d), lambda n, b, i: (b, 0, 0)),
                  pl.BlockSpec((nb, 1, d), lambda n, b, i: (b, 0, 0)),
                  pl.BlockSpec((1, d), lambda n, b, i: (0, 0)),
                  pl.BlockSpec((1, d), lambda n, b, i: (0, 0)),
                  pl.BlockSpec((d, tn), lambda n, b, i: (0, n))],
        out_specs=pl.BlockSpec((nb, tb, tn), lambda n, b, i: (b, i, n)),
        out_shape=jax.ShapeDtypeStruct((bsz, t, width), out_dtype),
        compiler_params=_cparams(("arbitrary", "arbitrary", "arbitrary")),
        name="inproj",
    )(x, sc, sh, g0, b0, w_fat)


CONV_A_HALO = 32


def _glu(pre):
    pre = pre.astype(F32)
    return pre[..., :CONV_A_CH] * _sigmoid(pre[..., CONV_A_CH:])


def _conva_kernel(*refs, n_t, nbc, rc):
    if n_t > 1:
        pre_ref, halo_ref, hist_ref, w_ref, cb_ref, g_ref, b_ref, y_ref, ho_ref, xp_ref, xs_ref = refs
    else:
        pre_ref, hist_ref, w_ref, cb_ref, g_ref, b_ref, y_ref, ho_ref, xp_ref, xs_ref = refs
    nb, tb, _ = pre_ref.shape
    t = pl.program_id(1)
    hl = CONV_A_HALO

    @pl.when(t == 0)
    def _():
        xp_ref[:, 0:hl, :] = hist_ref[...]

    if n_t > 1:
        @pl.when(t > 0)
        def _():
            xp_ref[:, 0:hl, :] = _glu(halo_ref[...])

    xp_ref[:, hl:hl + tb, :] = _glu(pre_ref[...])
    xp = xp_ref[...]
    xs_ref[0] = xp
    for b in range(1, SUBLANES):
        xs_ref[b] = pltpu.roll(xp, b, 1)
    for b0 in range(0, nb, nbc):
        for r0 in range(0, tb, rc):
            acc = None
            for k in range(CONV_A_WIDTH):
                back = CONV_A_WIDTH - 1 - k
                whole, part = back // SUBLANES, back % SUBLANES
                start = hl - whole * SUBLANES + r0
                term = w_ref[k:k + 1, :] * xs_ref[part, b0:b0 + nbc, start:start + rc, :]
                acc = term if acc is None else acc + term
            y = _silu(_ln(acc + cb_ref[...], g_ref[...], b_ref[...]))
            y_ref[b0:b0 + nbc, r0:r0 + rc, :] = y.astype(y_ref.dtype)
    ho_ref[...] = xp_ref[:, tb:tb + hl, :]


def _conv_module(fat, hist32, w, cb, g, b, *, nb, tb):
    bsz, t, _ = fat.shape
    n_t = t // tb
    pre_blk = PREA_OFF // (2 * CONV_A_CH)
    hl = CONV_A_HALO
    in_specs = [pl.BlockSpec((nb, tb, 2 * CONV_A_CH), lambda bi, i: (bi, i, pre_blk))]
    args = [fat]
    if n_t > 1:
        per = tb // hl
        in_specs.append(pl.BlockSpec((nb, hl, 2 * CONV_A_CH),
                                     lambda bi, i: (bi, jnp.maximum(i * per - 1, 0), pre_blk)))
        args.append(fat)
    in_specs += [pl.BlockSpec((nb, hl, CONV_A_CH), lambda bi, i: (bi, 0, 0)),
                 pl.BlockSpec((CONV_A_WIDTH, CONV_A_CH), lambda bi, i: (0, 0)),
                 pl.BlockSpec((1, CONV_A_CH), lambda bi, i: (0, 0)),
                 pl.BlockSpec((1, CONV_A_CH), lambda bi, i: (0, 0)),
                 pl.BlockSpec((1, CONV_A_CH), lambda bi, i: (0, 0))]
    args += [hist32, w, cb, g, b]
    if n_t > 1:
        nbc, rc = 1, 64
    else:
        nbc, rc = 4, tb
    return pl.pallas_call(
        functools.partial(_conva_kernel, n_t=n_t, nbc=nbc, rc=rc),
        grid=(bsz // nb, n_t),
        in_specs=in_specs,
        out_specs=[pl.BlockSpec((nb, tb, CONV_A_CH), lambda bi, i: (bi, i, 0)),
                   pl.BlockSpec((nb, hl, CONV_A_CH), lambda bi, i: (bi, 0, 0))],
        out_shape=[jax.ShapeDtypeStruct((bsz, t, CONV_A_CH), BF16),
                   jax.ShapeDtypeStruct((bsz, hl, CONV_A_CH), F32)],
        scratch_shapes=[pltpu.VMEM((nb, hl + tb, CONV_A_CH), F32),
                        pltpu.VMEM((SUBLANES, nb, hl + tb, CONV_A_CH), F32)],
        compiler_params=_cparams(("arbitrary", "arbitrary")),
        name="conv_module",
    )(*args)


GDN_ROWS = 256
GDN_SUB = 128


def _gdn_kernel(*refs, n_t, chunk):
    if n_t > 1:
        (qkv_ref, halo_ref, hist_ref, z_ref, small_ref, st_ref, cw_ref, alog_ref, dt_ref, ng_ref,
         y_ref, ho_ref, so_ref, xp_ref, s_ref, vn_ref, os_ref) = refs
    else:
        (qkv_ref, hist_ref, z_ref, small_ref, st_ref, cw_ref, alog_ref, dt_ref, ng_ref,
         y_ref, ho_ref, so_ref, xp_ref, s_ref, vn_ref, os_ref) = refs
    nb, tb, _ = qkv_ref.shape
    rows = nb * tb
    t = pl.program_id(1)
    hl = SUBLANES
    nk = GDN_HEADS * GDN_DK

    @pl.when(t == 0)
    def _():
        xp_ref[:, 0:hl, :] = hist_ref[...]
        s_ref[...] = st_ref[...]

    if n_t > 1:
        @pl.when(t > 0)
        def _():
            xp_ref[:, 0:hl, :] = halo_ref[...]

    xp_ref[:, hl:hl + tb, :] = qkv_ref[...]
    xp = xp_ref[...]
    acc = cw_ref[0:1, :] * xp
    for k in range(1, GDN_CONV_WIDTH):
        acc = cw_ref[k:k + 1, :] * xp + pltpu.roll(acc, 1, 1)
    c = _silu(acc[:, hl:hl + tb, :]).reshape(rows, GDN_QKV_WIDTH)
    ho_ref[...] = xp_ref[:, tb:tb + hl, :]

    sm = small_ref[...].reshape(rows, LANES)
    beta_all = _sigmoid(sm)
    xg = sm + dt_ref[...]
    softplus = jnp.maximum(xg, 0.0) + jnp.log1p(jnp.exp(-jnp.abs(xg)))
    g_all = -jnp.exp(alog_ref[...]) * softplus

    rs = GDN_SUB
    n_sub = rows // rs
    shift = int(math.log2(chunk))
    ri = lax.broadcasted_iota(jnp.int32, (rs, rs), 0)
    ci = lax.broadcasted_iota(jnp.int32, (rs, rs), 1)
    same = (ri >> shift) == (ci >> shift)
    tri = same & (ci <= ri)
    strict = same & (ci < ri)
    tri_b = jnp.where(tri, 1.0, 0.0).astype(BF16)
    triu_b = jnp.where(same & (ri <= ci), 1.0, 0.0).astype(BF16)
    same_b = jnp.where(same, 1.0, 0.0).astype(BF16)
    gc_col, gc_row, gl_col = [], [], []
    for s in range(n_sub):
        g3 = _split3(g_all[s * rs:(s + 1) * rs])
        gc_col.append(_dot(tri_b, g3[0]) + _dot(tri_b, g3[1]) + _dot(tri_b, g3[2]))
        gc_row.append(_dot_tn(g3[0], triu_b) + _dot_tn(g3[1], triu_b) + _dot_tn(g3[2], triu_b))
        gl_col.append(_dot(same_b, g3[0]) + _dot(same_b, g3[1]) + _dot(same_b, g3[2]))

    chains = [(s, h) for s in range(n_sub) for h in range(GDN_HEADS)]
    each = lambda f, *ls: [f(*xs) for xs in zip(*ls)]
    qs, ks, kbs, rhss, decays, betas, egs, gcs, gls = [], [], [], [], [], [], [], [], []
    for s, h in chains:
        r = slice(s * rs, (s + 1) * rs)
        qh = c[r, h * GDN_DK:(h + 1) * GDN_DK]
        kh = c[r, nk + h * GDN_DK:nk + (h + 1) * GDN_DK]
        vh = c[r, 2 * nk + h * GDN_DV:2 * nk + (h + 1) * GDN_DV]
        qh = qh * lax.rsqrt(jnp.sum(qh * qh, axis=-1, keepdims=True) + RMS_EPS) * (GDN_DK ** -0.5)
        kh = kh * lax.rsqrt(jnp.sum(kh * kh, axis=-1, keepdims=True) + RMS_EPS)
        beta = beta_all[r, BETA_LANE + h:BETA_LANE + h + 1]
        gc = gc_col[s][:, DEC_LANE + h:DEC_LANE + h + 1]
        gr = gc_row[s][DEC_LANE + h:DEC_LANE + h + 1, :]
        eg = jnp.exp(gc)
        qs.append(qh)
        ks.append(kh)
        kbs.append(kh.astype(BF16))
        betas.append(beta)
        gcs.append(gc)
        gls.append(gl_col[s][:, DEC_LANE + h:DEC_LANE + h + 1])
        egs.append(eg)
        decays.append(jnp.where(tri, jnp.exp(jnp.where(tri, gc - gr, 0.0)), 0.0))
        rhss.append(jnp.concatenate([vh * beta, kh * (beta * eg)], axis=1))
    kks = each(lambda kb: _dot_nt(kb, kb), kbs)
    a0s = each(lambda beta, kk, decay: jnp.where(strict, -(beta * kk * decay), 0.0), betas, kks, decays)
    ms = a0s
    pows = a0s
    for _ in range(1, shift):
        pows = each(lambda a: _dot(a.astype(BF16), a.astype(BF16)), pows)
        ms = each(lambda m, a: m + a + _dot(m.astype(BF16), a.astype(BF16)), ms, pows)
    mbs = each(lambda m: m.astype(BF16), ms)
    x0s = each(lambda rhs, mb: rhs + _dot(mb, rhs.astype(BF16)), rhss, mbs)
    ress = each(lambda rhs, x0, a0: rhs - x0 + _dot_split(a0, x0), rhss, x0s, a0s)
    sols = each(lambda x0, res, mb: x0 + res + _dot(mb, res.astype(BF16)), x0s, ress, mbs)
    qkms = each(lambda qh, kb, decay: jnp.where(tri, _dot_nt(qh.astype(BF16), kb) * decay, 0.0).astype(BF16),
                qs, kbs, decays)
    us = each(lambda sol: sol[:, :GDN_DV], sols)
    wbs = each(lambda sol: sol[:, GDN_DV:].astype(BF16), sols)
    qds = each(lambda qh, eg: (qh * eg).astype(BF16), qs, egs)
    kds = each(lambda kh, gl, gc: (kh * jnp.exp(gl - gc)).astype(BF16), ks, gls, gcs)
    egls = each(jnp.exp, gls)
    for j in range(rows // chunk):
        bi = (j * chunk) // tb
        s = (j * chunk) // rs
        r0 = j * chunk - s * rs
        for h in range(GDN_HEADS):
            i = s * GDN_HEADS + h
            sc = s_ref[bi, h]
            sb = sc.astype(BF16)
            vnew = us[i][r0:r0 + chunk] - _dot(wbs[i][r0:r0 + chunk], sb)
            os_ref[h, j * chunk:(j + 1) * chunk, :] = _dot(qds[i][r0:r0 + chunk], sb)
            vn_ref[h, j * chunk:(j + 1) * chunk, :] = vnew
            s_ref[bi, h] = sc * egls[i][r0:r0 + 1, :] + _dot_tn(kds[i][r0:r0 + chunk], vnew.astype(BF16))
    z = z_ref[...].astype(F32).reshape(rows, GDN_HEADS * GDN_DV)
    for i, (s, h) in enumerate(chains):
        r = slice(s * rs, (s + 1) * rs)
        o = os_ref[h, r, :] + _dot(qkms[i], vn_ref[h, r, :].astype(BF16))
        on = _rms(o, ng_ref[...])
        y = (on * _silu(z[r, h * GDN_DV:(h + 1) * GDN_DV])).astype(y_ref.dtype)
        if tb >= rs:
            y_ref[0, s * rs:(s + 1) * rs, h * GDN_DV:(h + 1) * GDN_DV] = y
        else:
            y_ref[s * (rs // tb):(s + 1) * (rs // tb), :, h * GDN_DV:(h + 1) * GDN_DV] = y.reshape(rs // tb, tb, GDN_DV)
    so_ref[...] = s_ref[...]


def _gated_deltanet(fat, fat16, hist8, state, cw, alog_row, dt_row, ng, *, nb, tb):
    bsz, t, _ = fat.shape
    n_t = t // tb
    chunk = min(CHUNK, t)
    hl = SUBLANES
    qkv_blk = QKV_OFF // GDN_QKV_WIDTH
    z_blk = Z_OFF // (GDN_HEADS * GDN_DV)
    small_blk = SMALL_OFF // LANES
    in_specs = [pl.BlockSpec((nb, tb, GDN_QKV_WIDTH), lambda bi, i: (bi, i, qkv_blk))]
    args = [fat]
    if n_t > 1:
        per = tb // hl
        in_specs.append(pl.BlockSpec((nb, hl, GDN_QKV_WIDTH), lambda bi, i: (bi, jnp.maximum(i * per - 1, 0), qkv_blk)))
        args.append(fat)
    in_specs += [pl.BlockSpec((nb, hl, GDN_QKV_WIDTH), lambda bi, i: (bi, 0, 0)),
                 pl.BlockSpec((nb, tb, GDN_HEADS * GDN_DV), lambda bi, i: (bi, i, z_blk)),
                 pl.BlockSpec((nb, tb, LANES), lambda bi, i: (bi, i, small_blk)),
                 pl.BlockSpec((nb, GDN_HEADS, GDN_DK, GDN_DV), lambda bi, i: (bi, 0, 0, 0)),
                 pl.BlockSpec((GDN_CONV_WIDTH, GDN_QKV_WIDTH), lambda bi, i: (0, 0)),
                 pl.BlockSpec((1, LANES), lambda bi, i: (0, 0)),
                 pl.BlockSpec((1, LANES), lambda bi, i: (0, 0)),
                 pl.BlockSpec((1, GDN_DV), lambda bi, i: (0, 0))]
    args += [hist8, fat16, fat, state, cw, alog_row, dt_row, ng]
    rows = nb * tb
    return pl.pallas_call(
        functools.partial(_gdn_kernel, n_t=n_t, chunk=chunk),
        grid=(bsz // nb, n_t),
        in_specs=in_specs,
        out_specs=[pl.BlockSpec((nb, tb, GDN_HEADS * GDN_DV), lambda bi, i: (bi, i, 0)),
                   pl.BlockSpec((nb, hl, GDN_QKV_WIDTH), lambda bi, i: (bi, 0, 0)),
                   pl.BlockSpec((nb, GDN_HEADS, GDN_DK, GDN_DV), lambda bi, i: (bi, 0, 0, 0))],
        out_shape=[jax.ShapeDtypeStruct((bsz, t, GDN_HEADS * GDN_DV), BF16),
                   jax.ShapeDtypeStruct((bsz, hl, GDN_QKV_WIDTH), F32),
                   jax.ShapeDtypeStruct((bsz, GDN_HEADS, GDN_DK, GDN_DV), F32)],
        scratch_shapes=[pltpu.VMEM((nb, hl + tb, GDN_QKV_WIDTH), F32),
                        pltpu.VMEM((nb, GDN_HEADS, GDN_DK, GDN_DV), F32),
                        pltpu.VMEM((GDN_HEADS, rows, GDN_DV), F32),
                        pltpu.VMEM((GDN_HEADS, rows, GDN_DV), F32)],
        compiler_params=_cparams(("arbitrary", "arbitrary")),
        name="gated_deltanet",
    )(*args)


def _mla_prep_kernel(*refs, emit_kv, scale):
    if emit_kv:
        (blk_ref, cs_ref, sn_ref, qg_ref, kg_ref, wq_ref, wqs_ref, wk_ref, wvt_ref,
         q_ref, lat_ref, kpe_ref, k_ref, vt_ref) = refs
    else:
        (blk_ref, cs_ref, sn_ref, qg_ref, kg_ref, wq_ref, wqs_ref,
         q_ref, lat_ref, kpe_ref) = refs
    nb, tb, _ = blk_ref.shape
    rows = nb * tb
    blk = blk_ref[...]
    cs = cs_ref[...]
    sn = sn_ref[...]
    qn = _rms(blk[..., :MLA_Q_LORA], qg_ref[...]).reshape(rows, MLA_Q_LORA).astype(BF16)
    qf = _dot(qn, wq_ref[...]).reshape(nb, tb, MLA_HEADS * QK_PAD)
    qs = _dot(qn, wqs_ref[...]).reshape(nb, tb, MLA_HEADS * LANES)
    for h in range(MLA_HEADS):
        nope = qf[..., h * QK_PAD:h * QK_PAD + MLA_NOPE_DIM]
        rope = qf[..., h * QK_PAD + MLA_NOPE_DIM:(h + 1) * QK_PAD] * cs + qs[..., h * LANES:(h + 1) * LANES] * sn
        q_ref[h, :, :, 0:MLA_NOPE_DIM] = (nope * scale).astype(q_ref.dtype)
        q_ref[h, :, :, MLA_NOPE_DIM:QK_PAD] = (rope * scale).astype(q_ref.dtype)

    lat = _rms(blk[..., MLA_Q_LORA + LANES:], kg_ref[...])
    lat_ref[...] = lat

    small = blk[..., MLA_Q_LORA:MLA_Q_LORA + LANES].reshape(rows, LANES)
    half = MLA_ROPE_DIM // 2
    lane = lax.broadcasted_iota(jnp.int32, (rows, LANES), 1)
    swapped = jnp.where(lane < half, pltpu.roll(small, LANES - half, 1), pltpu.roll(small, half, 1))
    kpe = small.reshape(nb, tb, LANES) * cs + swapped.reshape(nb, tb, LANES) * sn
    kpe_ref[...] = kpe[..., :MLA_ROPE_DIM]

    if emit_kv:
        latb = lat.reshape(rows, MLA_KV_LORA).astype(BF16)
        kn = _dot(latb, wk_ref[...]).reshape(nb, tb, MLA_HEADS * MLA_NOPE_DIM)
        ones_rows = jnp.where(lax.broadcasted_iota(jnp.int32, (VT_ROWS - MLA_V_DIM, ATTN_KB), 0) == 0,
                              1.0, 0.0).astype(vt_ref.dtype)
        for h in range(MLA_HEADS):
            k_ref[h, :, :, 0:MLA_NOPE_DIM] = kn[..., h * MLA_NOPE_DIM:(h + 1) * MLA_NOPE_DIM].astype(k_ref.dtype)
            k_ref[h, :, :, MLA_NOPE_DIM:QK_PAD] = kpe.astype(k_ref.dtype)
            vt = _dot_nt(wvt_ref[h], latb).astype(vt_ref.dtype)
            for kb in range(rows // ATTN_KB):
                vt_ref[h, 0, kb, 0:MLA_V_DIM, :] = vt[:, kb * ATTN_KB:(kb + 1) * ATTN_KB]
                vt_ref[h, 0, kb, MLA_V_DIM:VT_ROWS, :] = ones_rows


def _mla_prep(fat, cs, sn, qg, kg, wq, wqs, wk, wvt, *, nb, tb, emit_kv, scale):
    bsz, t, _ = fat.shape
    blk = QLAT_OFF // MLA_BLK_W
    hq = MLA_HEADS
    in_specs = [pl.BlockSpec((nb, tb, MLA_BLK_W), lambda bi, i: (bi, i, blk)),
                pl.BlockSpec((tb, LANES), lambda bi, i: (i, 0)),
                pl.BlockSpec((tb, LANES), lambda bi, i: (i, 0)),
                pl.BlockSpec((1, MLA_Q_LORA), lambda bi, i: (0, 0)),
                pl.BlockSpec((1, MLA_KV_LORA), lambda bi, i: (0, 0)),
                pl.BlockSpec(wq.shape, lambda bi, i: (0, 0)),
                pl.BlockSpec(wqs.shape, lambda bi, i: (0, 0))]
    args = [fat, cs, sn, qg, kg, wq, wqs]
    out_specs = [pl.BlockSpec((hq, nb, tb, QK_PAD), lambda bi, i: (0, bi, i, 0)),
                 pl.BlockSpec((nb, tb, MLA_KV_LORA), lambda bi, i: (bi, i, 0)),
                 pl.BlockSpec((nb, tb, MLA_ROPE_DIM), lambda bi, i: (bi, i, 0))]
    out_shape = [jax.ShapeDtypeStruct((hq, bsz, t, QK_PAD), BF16),
                 jax.ShapeDtypeStruct((bsz, t, MLA_KV_LORA), F32),
                 jax.ShapeDtypeStruct((bsz, t, MLA_ROPE_DIM), F32)]
    if emit_kv:
        assert nb == 1 and tb % ATTN_KB == 0
        in_specs += [pl.BlockSpec(wk.shape, lambda bi, i: (0, 0)),
                     pl.BlockSpec(wvt.shape, lambda bi, i: (0, 0, 0))]
        args += [wk, wvt]
        per = tb // ATTN_KB
        out_specs += [pl.BlockSpec((hq, nb, tb, QK_PAD), lambda bi, i: (0, bi, i, 0)),
                      pl.BlockSpec((hq, nb, per, VT_ROWS, ATTN_KB), lambda bi, i: (0, bi, i, 0, 0))]
        out_shape += [jax.ShapeDtypeStruct((hq, bsz, t, QK_PAD), BF16),
                      jax.ShapeDtypeStruct((hq, bsz, t // ATTN_KB, VT_ROWS, ATTN_KB), BF16)]
    return pl.pallas_call(
        functools.partial(_mla_prep_kernel, emit_kv=emit_kv, scale=scale),
        grid=(bsz // nb, t // tb),
        in_specs=in_specs, out_specs=out_specs, out_shape=out_shape,
        compiler_params=_cparams(("arbitrary", "arbitrary")),
        name="mla_prep",
    )(*args)


def _attn_prompt_kernel(q_ref, k_ref, vt_ref, o_ref, m_ref, acc_ref, st_ref, *, nq):
    kb = ATTN_KB
    j = pl.program_id(2)
    m_ref[...] = jnp.full(m_ref.shape, -jnp.inf, F32)
    acc_ref[...] = jnp.zeros(acc_ref.shape, F32)

    def scores(chains, ki):
        k0 = pl.multiple_of(ki * kb, kb)
        kblk = k_ref[0, 0, pl.ds(k0, kb), :]
        return [_dot_nt(kblk, q_ref[0, 0, c * kb:(c + 1) * kb, :]) for c in chains]

    def finish(chains, sts, ki, masked_chain):
        pts, alphas = [], []
        for c, st in zip(chains, sts):
            if c == masked_chain:
                key_chunk = lax.broadcasted_iota(jnp.int32, (kb, kb), 0) // CHUNK
                qry_chunk = lax.broadcasted_iota(jnp.int32, (kb, kb), 1) // CHUNK
                st = jnp.where(key_chunk <= qry_chunk, st, -jnp.inf)
            m_old = m_ref[c]
            m_new = jnp.maximum(m_old, jnp.max(st, axis=0, keepdims=True))
            pts.append(jnp.exp2(st - m_new).astype(BF16))
            alphas.append(jnp.exp2(m_old - m_new))
            m_ref[c] = m_new
        vblk = vt_ref[0, 0, ki]
        for c, pt, alpha in zip(chains, pts, alphas):
            acc_ref[c] = alpha * acc_ref[c] + _dot(vblk, pt)

    everyone = range(nq)
    for c, st in zip(everyone, scores(everyone, 0)):
        st_ref[c] = st

    def body(ki, carry):
        cur = [st_ref[c] for c in everyone]
        nxt = scores(everyone, ki + 1)
        finish(everyone, cur, ki, None)
        for c in everyone:
            st_ref[c] = nxt[c]
        return carry

    lax.fori_loop(0, nq * j, body, 0)
    cur = [st_ref[c] for c in everyone]
    for d in range(nq):
        nxt = scores(range(d + 1, nq), nq * j + d + 1) if d + 1 < nq else []
        finish(range(d, nq), cur, nq * j + d, d)
        cur = nxt
    for c in range(nq):
        acc = acc_ref[c]
        ot = acc[0:MLA_V_DIM] / acc[MLA_V_DIM:MLA_V_DIM + 1]
        o_ref[0, c * kb:(c + 1) * kb, :] = ot.T.astype(o_ref.dtype)


def _attn_prompt(q, k, vt, *, nq):
    hq, bsz, t, _ = q.shape
    tq = ATTN_KB * nq
    return pl.pallas_call(
        functools.partial(_attn_prompt_kernel, nq=nq),
        grid=(bsz, hq, t // tq),
        in_specs=[pl.BlockSpec((1, 1, tq, QK_PAD), lambda b, h, i: (h, b, i, 0)),
                  pl.BlockSpec((1, 1, t, QK_PAD), lambda b, h, i: (h, b, 0, 0)),
                  pl.BlockSpec((1, 1, t // ATTN_KB, VT_ROWS, ATTN_KB), lambda b, h, i: (h, b, 0, 0, 0))],
        out_specs=pl.BlockSpec((1, tq, MLA_V_DIM), lambda b, h, i: (b, i, h)),
        out_shape=jax.ShapeDtypeStruct((bsz, t, MLA_HEADS * MLA_V_DIM), BF16),
        scratch_shapes=[pltpu.VMEM((nq, 1, ATTN_KB), F32), pltpu.VMEM((nq, VT_ROWS, ATTN_KB), F32),
                        pltpu.VMEM((nq, ATTN_KB, ATTN_KB), F32)],
        compiler_params=_cparams(("arbitrary", "arbitrary", "arbitrary")),
        name="attn_prompt",
    )(q, k, vt)


def _attn_sample_kernel(q_ref, latc_ref, kpec_ref, latn_ref, kpen_ref, wuk_ref, wuv_ref, o_ref):
    tn = latn_ref.shape[1]
    past = latc_ref.shape[2]
    hq = MLA_HEADS
    qabs = jnp.concatenate(
        [_dot(q_ref[h, 0, :, 0:MLA_NOPE_DIM], wuk_ref[h]) for h in range(hq)], axis=0).astype(BF16)
    qrope = jnp.concatenate([q_ref[h, 0, :, MLA_NOPE_DIM:MLA_NOPE_DIM + MLA_ROPE_DIM] for h in range(hq)], axis=0)
    latc = latc_ref[0, 0].astype(BF16)
    latn = latn_ref[0].astype(BF16)
    sc = _dot_nt(qabs, latc) + _dot_nt(qrope, kpec_ref[0, 0].astype(BF16))
    sn = _dot_nt(qabs, latn) + _dot_nt(qrope, kpen_ref[0].astype(BF16))
    rows = hq * tn
    qpos = past + lax.broadcasted_iota(jnp.int32, (rows, 1), 0) % tn
    kc_pos = lax.broadcasted_iota(jnp.int32, (rows, past), 1)
    kn_pos = past + lax.broadcasted_iota(jnp.int32, (rows, tn), 1)
    sc = jnp.where(kc_pos // CHUNK <= qpos // CHUNK, sc, -jnp.inf)
    sn = jnp.where(kn_pos // CHUNK <= qpos // CHUNK, sn, -jnp.inf)
    m = jnp.maximum(jnp.max(sc, axis=-1, keepdims=True), jnp.max(sn, axis=-1, keepdims=True))
    pc = jnp.exp(sc - m)
    pn = jnp.exp(sn - m)
    l = jnp.sum(pc, axis=-1, keepdims=True) + jnp.sum(pn, axis=-1, keepdims=True)
    olat = (_dot(pc.astype(BF16), latc) + _dot(pn.astype(BF16), latn)) / l
    for h in range(hq):
        oh = _dot(olat[h * tn:(h + 1) * tn].astype(BF16), wuv_ref[h])
        o_ref[0, :, h * MLA_V_DIM:(h + 1) * MLA_V_DIM] = oh.astype(o_ref.dtype)


def _attn_sample(q, caches_lat, caches_kpe, layer, lat_new, kpe_new, wuk_t, wuv):
    hq, bsz, tn, _ = q.shape
    past = caches_lat.shape[2]
    return pl.pallas_call(
        _attn_sample_kernel,
        grid=(bsz,),
        in_specs=[pl.BlockSpec((hq, 1, tn, QK_PAD), lambda b: (0, b, 0, 0)),
                  pl.BlockSpec((1, 1, past, MLA_KV_LORA), lambda b: (layer, b, 0, 0)),
                  pl.BlockSpec((1, 1, past, MLA_ROPE_DIM), lambda b: (layer, b, 0, 0)),
                  pl.BlockSpec((1, tn, MLA_KV_LORA), lambda b: (b, 0, 0)),
                  pl.BlockSpec((1, tn, MLA_ROPE_DIM), lambda b: (b, 0, 0)),
                  pl.BlockSpec(wuk_t.shape, lambda b: (0, 0, 0)),
                  pl.BlockSpec(wuv.shape, lambda b: (0, 0, 0))],
        out_specs=pl.BlockSpec((1, tn, MLA_HEADS * MLA_V_DIM), lambda b: (b, 0, 0)),
        out_shape=jax.ShapeDtypeStruct((bsz, tn, MLA_HEADS * MLA_V_DIM), BF16),
        compiler_params=_cparams(("arbitrary",)),
        name="attn_sample",
    )(q, caches_lat, caches_kpe, lat_new, kpe_new, wuk_t, wuv)


def _merge_kernel(ya_ref, yb_ref, yc_ref, gate_ref, x_ref, g1_ref, g0_ref, b0_ref, wb_ref, wo_ref,
                  lg_ref, lb_ref, o_ref, *, ln0, alpha):
    nb, tb, d = x_ref.shape
    rows = nb * tb
    x = x_ref[...]
    if ln0:
        x = _ln(x, g0_ref[...], b0_ref[...])
    merged = None
    for i, y_ref in enumerate((ya_ref, yb_ref, yc_ref)):
        proj = _dot(y_ref[...].reshape(rows, BRANCH_WIDTH), wb_ref[i])
        gate = _sigmoid(gate_ref[:, :, i * d:(i + 1) * d].astype(F32)).reshape(rows, d)
        merged = gate * proj if merged is None else merged + gate * proj
    out = _dot(merged.astype(BF16), wo_ref[...]).reshape(nb, tb, d)
    o_ref[...] = _ln(alpha * x + (1.0 + g1_ref[...]) * out, lg_ref[...], lb_ref[...])


def _merge(ya, yb, yc, fat, x, g1, g0, b0, wb, wo, lg, lb, *, nb, tb, ln0, alpha):
    bsz, t, d = x.shape
    row = lambda bi, i: (bi, i, 0)
    const2 = lambda bi, i: (0, 0)
    return pl.pallas_call(
        functools.partial(_merge_kernel, ln0=ln0, alpha=alpha),
        grid=(bsz // nb, t // tb),
        in_specs=[pl.BlockSpec((nb, tb, BRANCH_WIDTH), row),
                  pl.BlockSpec((nb, tb, BRANCH_WIDTH), row),
                  pl.BlockSpec((nb, tb, BRANCH_WIDTH), row),
                  pl.BlockSpec((nb, tb, GATE_W), row),
                  pl.BlockSpec((nb, tb, d), row),
                  pl.BlockSpec((nb, 1, d), lambda bi, i: (bi, 0, 0)),
                  pl.BlockSpec((1, d), const2),
                  pl.BlockSpec((1, d), const2),
                  pl.BlockSpec(wb.shape, lambda bi, i: (0, 0, 0)),
                  pl.BlockSpec(wo.shape, const2),
                  pl.BlockSpec((1, d), const2),
                  pl.BlockSpec((1, d), const2)],
        out_specs=pl.BlockSpec((nb, tb, d), row),
        out_shape=jax.ShapeDtypeStruct((bsz, t, d), F32),
        compiler_params=_cparams(("arbitrary", "arbitrary")),
        name="merge",
    )(ya, yb, yc, fat, x, g1, g0, b0, wb, wo, lg, lb)


FFN_BLOCK = 1408


def _ffn_kernel(x_ref, sc_ref, sh_ref, g2_ref, wa_ref, wv_ref, wd_ref, cw_ref, cb_ref, lg_ref, lb_ref, hist_ref,
                o_ref, ho_ref, h_ref, acc_ref, carry_ref, ap_ref, *, alpha):
    nb, tb, d = x_ref.shape
    rows = nb * tb
    t = pl.program_id(1)
    f = pl.program_id(2)
    n_f = pl.num_programs(2)
    hl = SUBLANES
    fb = wa_ref.shape[1]

    @pl.when(f == 0)
    def _():
        h = x_ref[...] * (1.0 + sc_ref[...]) + sh_ref[...]
        h_ref[...] = h.reshape(rows, d).astype(BF16)

    @pl.when(t == 0)
    def _():
        carry_ref[f] = hist_ref[...]

    h = h_ref[...]
    a = _dot(h, wa_ref[...]).reshape(nb, tb, fb)
    v = _dot(h, wv_ref[...]).reshape(nb, tb, fb)
    ap_ref[:, 0:hl, :] = carry_ref[f]
    ap_ref[:, hl:hl + tb, :] = a
    ap = ap_ref[...]
    conv = cw_ref[0:1, :] * ap
    for k in range(1, FFN_CONV_WIDTH):
        conv = cw_ref[k:k + 1, :] * ap + pltpu.roll(conv, 1, 1)
    act = _silu(conv[:, hl:hl + tb, :] + cb_ref[...]) * v
    contrib = _dot(act.reshape(rows, fb).astype(BF16), wd_ref[...])
    last = ap_ref[:, tb:tb + hl, :]
    carry_ref[f] = last
    ho_ref[0] = last

    @pl.when(f == 0)
    def _():
        acc_ref[...] = contrib

    @pl.when(f > 0)
    def _():
        acc_ref[...] = acc_ref[...] + contrib

    @pl.when(f == n_f - 1)
    def _():
        y = acc_ref[...].reshape(nb, tb, d)
        o_ref[...] = _ln(alpha * x_ref[...] + (1.0 + g2_ref[...]) * y, lg_ref[...], lb_ref[...])


def _ffn(x, sc, sh, g2, wa, wv, wd, cw, cb, lg, lb, hist8, *, nb, tb, alpha):
    bsz, t, d = x.shape
    dff = wa.shape[1]
    fb = FFN_BLOCK
    n_f = dff // fb
    hl = SUBLANES
    rows = nb * tb
    row = lambda bi, i, f: (bi, i, 0)
    mod = lambda bi, i, f: (bi, 0, 0)
    const2 = lambda bi, i, f: (0, 0)
    return pl.pallas_call(
        functools.partial(_ffn_kernel, alpha=alpha),
        grid=(bsz // nb, t // tb, n_f),
        in_specs=[pl.BlockSpec((nb, tb, d), row),
                  pl.BlockSpec((nb, 1, d), mod),
                  pl.BlockSpec((nb, 1, d), mod),
                  pl.BlockSpec((nb, 1, d), mod),
                  pl.BlockSpec((d, fb), lambda bi, i, f: (0, f)),
                  pl.BlockSpec((d, fb), lambda bi, i, f: (0, f)),
                  pl.BlockSpec((fb, d), lambda bi, i, f: (f, 0)),
                  pl.BlockSpec((FFN_CONV_WIDTH, fb), lambda bi, i, f: (0, f)),
                  pl.BlockSpec((1, fb), lambda bi, i, f: (0, f)),
                  pl.BlockSpec((1, d), const2),
                  pl.BlockSpec((1, d), const2),
                  pl.BlockSpec((nb, hl, fb), lambda bi, i, f: (bi, 0, f))],
        out_specs=[pl.BlockSpec((nb, tb, d), row),
                   pl.BlockSpec((1, nb, hl, fb), lambda bi, i, f: (i, bi, 0, f))],
        out_shape=[jax.ShapeDtypeStruct((bsz, t, d), F32),
                   jax.ShapeDtypeStruct((t // tb, bsz, hl, dff), F32)],
        scratch_shapes=[pltpu.VMEM((rows, d), BF16),
                        pltpu.VMEM((rows, d), F32),
                        pltpu.VMEM((n_f, nb, hl, fb), F32),
                        pltpu.VMEM((nb, hl + tb, fb), F32)],
        compiler_params=_cparams(("arbitrary", "arbitrary", "arbitrary")),
        name="conv_ffn",
    )(x, sc, sh, g2, wa, wv, wd, cw, cb, lg, lb, hist8)


def _front_pad(hist, rows):
    b, r, c = hist.shape
    return jnp.concatenate([jnp.zeros((b, rows - r, c), hist.dtype), hist], axis=1)


def _rope_rows(pos):
    half = MLA_ROPE_DIM // 2
    inv_freq = ROPE_THETA ** (-jnp.arange(half, dtype=F32) / half)
    ang = pos.astype(F32)[:, None] * inv_freq[None, :]
    cos, sin = jnp.cos(ang), jnp.sin(ang)
    pad = jnp.zeros((pos.shape[0], LANES - MLA_ROPE_DIM), F32)
    return jnp.concatenate([cos, cos, pad], axis=1), jnp.concatenate([-sin, sin, pad], axis=1)


def _lane_row(vals, lane0):
    return jnp.zeros((1, LANES), F32).at[0, lane0:lane0 + vals.shape[0]].set(vals.astype(F32))


def _layer_weights(l, w_in, mla_w_uq, mla_w_ukv, w_branch, w_out, w_up, w_down):
    d = D_MODEL
    o = np.cumsum([0, 2 * CONV_A_CH, GDN_QKV_WIDTH, GDN_HEADS * GDN_DV, GDN_HEADS, GDN_HEADS,
                   MLA_Q_LORA, MLA_KV_LORA, MLA_ROPE_DIM, N_BRANCH * D_MODEL]).tolist()
    w = w_in[l]
    seg = lambda i: w[:, o[i]:o[i + 1]].astype(BF16)
    pad = jnp.zeros((d, LANES - MLA_ROPE_DIM - 2 * GDN_HEADS), BF16)
    w_fat32 = jnp.concatenate([seg(1), seg(5), seg(7), seg(3), seg(4), pad, seg(6)], axis=1)
    w_fat16 = jnp.concatenate([seg(8), seg(0), seg(2)], axis=1)
    hd = MLA_NOPE_DIM + MLA_ROPE_DIM
    half = MLA_ROPE_DIM // 2
    wq = mla_w_uq[l]
    zq = jnp.zeros((MLA_Q_LORA, QK_PAD - hd), wq.dtype)
    zs = jnp.zeros((MLA_Q_LORA, LANES - MLA_ROPE_DIM), wq.dtype)
    wq_cols, wqs_cols = [], []
    for h in range(MLA_HEADS):
        wq_cols += [wq[:, h * hd:(h + 1) * hd], zq]
        r0 = h * hd + MLA_NOPE_DIM
        wqs_cols += [wq[:, r0 + half:r0 + 2 * half], wq[:, r0:r0 + half], zs]
    wkv = mla_w_ukv[l]
    kvw = MLA_NOPE_DIM + MLA_V_DIM
    wuk_t = jnp.stack([wkv[:, h * kvw:h * kvw + MLA_NOPE_DIM].T for h in range(MLA_HEADS)]).astype(BF16)
    wuv = jnp.stack([wkv[:, h * kvw + MLA_NOPE_DIM:(h + 1) * kvw] for h in range(MLA_HEADS)]).astype(BF16)
    return dict(
        w_fat32=w_fat32, w_fat16=w_fat16,
        wq=jnp.concatenate(wq_cols, axis=1).astype(BF16),
        wqs=jnp.concatenate(wqs_cols, axis=1).astype(BF16),
        wk=jnp.concatenate([wkv[:, h * kvw:h * kvw + MLA_NOPE_DIM] for h in range(MLA_HEADS)], axis=1).astype(BF16),
        wvt=jnp.transpose(wuv, (0, 2, 1)), wuk_t=wuk_t, wuv=wuv,
        wb=w_branch[l].astype(BF16), wo=w_out[l].astype(BF16),
        wa=w_up[l][:, :D_FF].astype(BF16), wv=w_up[l][:, D_FF:].astype(BF16), wd=w_down[l].astype(BF16),
    )


def _run_layer(x, mods, st, wts, p, *, ln0, g0, b0, alpha, cs, sn, tiles, prompt):
    bsz, t, d = x.shape
    sh1, sc1, g1, sh2, sc2, g2 = mods
    nb, tb, gdn_nb, gdn_tb = tiles
    if st is None:
        hist_a = jnp.zeros((bsz, CONV_A_WIDTH - 1, CONV_A_CH), F32)
        hist_b = jnp.zeros((bsz, GDN_CONV_WIDTH - 1, GDN_QKV_WIDTH), F32)
        state_b = jnp.zeros((bsz, GDN_HEADS, GDN_DK, GDN_DV), F32)
        hist_f = jnp.zeros((bsz, FFN_CONV_WIDTH - 1, D_FF), F32)
    else:
        caches_lat, caches_kpe, layer, hist_a, hist_b, state_b, hist_f = st
    row = lambda v: v.reshape(1, -1)

    fat = _inproj(x, sc1, sh1, g0, b0, wts['w_fat32'], F32, nb=nb, tb=tb, ln0=ln0)
    fat16 = _inproj(x, sc1, sh1, g0, b0, wts['w_fat16'], BF16, nb=nb, tb=tb, ln0=ln0)
    y_a, ha = _conv_module(fat16, _front_pad(hist_a, CONV_A_HALO), p['conv_a_w'], row(p['conv_a_b']),
                           row(p['ln_a_g']), row(p['ln_a_b']), nb=nb, tb=tb)
    y_b, hb, sb = _gated_deltanet(fat, fat16, _front_pad(hist_b, SUBLANES), state_b, p['gdn_conv_w'],
                                  _lane_row(p['gdn_a_log'], DEC_LANE), _lane_row(p['gdn_dt_bias'], DEC_LANE),
                                  row(p['gdn_norm_g']), nb=gdn_nb, tb=gdn_tb)
    if prompt:
        q, lat, kpe, k, v = _mla_prep(fat, cs, sn, row(p['mla_q_norm_g']), row(p['mla_kv_norm_g']),
                                      wts['wq'], wts['wqs'], wts['wk'], wts['wvt'], nb=nb, tb=tb, emit_kv=True,
                                      scale=ATTN_SCALE * math.log2(math.e))
        y_c = _attn_prompt(q, k, v, nq=math.gcd(ATTN_CHAINS, t // ATTN_KB))
    else:
        q, lat, kpe = _mla_prep(fat, cs, sn, row(p['mla_q_norm_g']), row(p['mla_kv_norm_g']),
                                wts['wq'], wts['wqs'], None, None, nb=nb, tb=tb, emit_kv=False, scale=ATTN_SCALE)
        y_c = _attn_sample(q, caches_lat, caches_kpe, layer, lat, kpe, wts['wuk_t'], wts['wuv'])
    x1 = _merge(y_a, y_b, y_c, fat16, x, g1, g0, b0, wts['wb'], wts['wo'], row(p['ln1_g']), row(p['ln1_b']),
                nb=nb, tb=tb, ln0=ln0, alpha=alpha)
    x2, hf = _ffn(x1, sc2, sh2, g2, wts['wa'], wts['wv'], wts['wd'], p['ffn_conv_w'], row(p['ffn_conv_b']),
                  row(p['ln2_g']), row(p['ln2_b']), _front_pad(hist_f, SUBLANES), nb=nb, tb=tb, alpha=alpha)
    states = (lat, kpe, ha[:, CONV_A_HALO - (CONV_A_WIDTH - 1):], hb[:, SUBLANES - (GDN_CONV_WIDTH - 1):],
              sb, hf[-1, :, SUBLANES - (FFN_CONV_WIDTH - 1):])
    return x2, states


def kernel(x_prompt, x_sample, cache_mla_latent, cache_mla_kpe, state_conv_a, state_gdn_conv, state_gdn, state_ffn_conv, c_prompt, c_sample, ln0_g, ln0_b, w_ada, b_ada, w_in, conv_a_w, conv_a_b, ln_a_g, ln_a_b, gdn_conv_w, gdn_a_log, gdn_dt_bias, gdn_norm_g, mla_q_norm_g, mla_kv_norm_g, mla_w_uq, mla_w_ukv, w_branch, w_out, ln1_g, ln1_b, w_up, ffn_conv_w, ffn_conv_b, w_down, ln2_g, ln2_b):
    depth = w_ada.shape[0]
    bp, seq, d = x_prompt.shape
    bs, dec_seq, _ = x_sample.shape
    past = cache_mla_latent.shape[2]
    alpha = (2 * depth) ** 0.25

    n_c = bp + bs
    c_rows = -(-n_c // SUBLANES) * SUBLANES
    c_all = jnp.concatenate([c_prompt, c_sample, jnp.zeros((c_rows - n_c, d), F32)], axis=0)
    mod = _modulation(c_all, w_ada, b_ada)

    cs_p, sn_p = _rope_rows(jnp.arange(seq, dtype=jnp.int32))
    cs_s, sn_s = _rope_rows(past + jnp.arange(dec_seq, dtype=jnp.int32))
    g0, b0 = ln0_g.reshape(1, d), ln0_b.reshape(1, d)

    tile_p = min(512, seq)
    tiles_p = (1, tile_p, 1, min(GDN_ROWS, seq))
    tiles_s = (bs, dec_seq, GDN_ROWS // dec_seq, dec_seq)

    xp, xs = x_prompt, x_sample
    p_states, s_states = [], []
    for l in range(depth):
        p = {'conv_a_w': conv_a_w[l], 'conv_a_b': conv_a_b[l], 'ln_a_g': ln_a_g[l], 'ln_a_b': ln_a_b[l],
             'gdn_conv_w': gdn_conv_w[l], 'gdn_a_log': gdn_a_log[l], 'gdn_dt_bias': gdn_dt_bias[l],
             'gdn_norm_g': gdn_norm_g[l], 'mla_q_norm_g': mla_q_norm_g[l], 'mla_kv_norm_g': mla_kv_norm_g[l],
             'ln1_g': ln1_g[l], 'ln1_b': ln1_b[l], 'ffn_conv_w': ffn_conv_w[l], 'ffn_conv_b': ffn_conv_b[l],
             'ln2_g': ln2_g[l], 'ln2_b': ln2_b[l]}
        wts = _layer_weights(l, w_in, mla_w_uq, mla_w_ukv, w_branch, w_out, w_up, w_down)
        mods_p = [m.reshape(bp, 1, d) for m in jnp.split(mod[l, :bp], 6, axis=-1)]
        mods_s = [m.reshape(bs, 1, d) for m in jnp.split(mod[l, bp:n_c], 6, axis=-1)]
        xp, st_p = _run_layer(xp, mods_p, None, wts, p, ln0=(l == 0), g0=g0, b0=b0, alpha=alpha,
                              cs=cs_p, sn=sn_p, tiles=tiles_p, prompt=True)
        st_in = (cache_mla_latent, cache_mla_kpe, l, state_conv_a[l], state_gdn_conv[l], state_gdn[l],
                 state_ffn_conv[l])
        xs, st_s = _run_layer(xs, mods_s, st_in, wts, p, ln0=(l == 0), g0=g0, b0=b0, alpha=alpha,
                              cs=cs_s, sn=sn_s, tiles=tiles_s, prompt=False)
        p_states.append(st_p)
        s_states.append(st_s)
    p_out = [jnp.stack(z, axis=0) for z in zip(*p_states)]
    s_out = [jnp.stack(z, axis=0) for z in zip(*s_states)]
    return (xp, xs, *p_out, *s_out)
```

```python
import functools
import math

import jax
import jax.numpy as jnp
import numpy as np
from jax import lax
from jax.experimental import pallas as pl
from jax.experimental.pallas import tpu as pltpu

F32 = jnp.float32
BF16 = jnp.bfloat16

D_MODEL = 1024
CHUNK = 64
CONV_A_CH = 512
CONV_A_WIDTH = 31
GDN_HEADS = 4
GDN_DK = 128
GDN_DV = 128
GDN_CONV_WIDTH = 4
GDN_QKV_WIDTH = 2 * GDN_HEADS * GDN_DK + GDN_HEADS * GDN_DV
MLA_HEADS = 4
MLA_Q_LORA = 384
MLA_KV_LORA = 256
MLA_NOPE_DIM = 128
MLA_ROPE_DIM = 64
MLA_V_DIM = 128
ROPE_THETA = 10000.0
N_BRANCH = 3
BRANCH_WIDTH = 512
D_FF = 2816
FFN_CONV_WIDTH = 3
LN_EPS = 1e-5
RMS_EPS = 1e-6

LANES = 128
SUBLANES = 8
VMEM_LIMIT = 48 * 1024 * 1024

QKV_OFF = 0
QLAT_OFF = QKV_OFF + GDN_QKV_WIDTH
SMALL_OFF = QLAT_OFF + MLA_Q_LORA
KVLAT_OFF = SMALL_OFF + LANES
FAT32_W = KVLAT_OFF + MLA_KV_LORA
MLA_BLK_W = FAT32_W - QLAT_OFF
GATE_OFF, GATE_W = 0, N_BRANCH * D_MODEL
PREA_OFF = GATE_OFF + GATE_W
Z_OFF = PREA_OFF + 2 * CONV_A_CH
FAT16_W = Z_OFF + GDN_HEADS * GDN_DV
INPROJ_TN = FAT32_W
BETA_LANE = MLA_ROPE_DIM
DEC_LANE = MLA_ROPE_DIM + GDN_HEADS
QK_PAD = 2 * LANES
ATTN_KB = 256
ATTN_CHAINS = 8
VT_ROWS = MLA_V_DIM + 16
ATTN_SCALE = (MLA_NOPE_DIM + MLA_ROPE_DIM) ** -0.5


def _cparams(sem):
    return pltpu.CompilerParams(dimension_semantics=sem, vmem_limit_bytes=VMEM_LIMIT)


def _sigmoid(x):
    return jax.nn.sigmoid(x)


def _silu(x):
    return x * jax.nn.sigmoid(x)


def _ln(x, g, b):
    mu = jnp.mean(x, axis=-1, keepdims=True)
    xc = x - mu
    var = jnp.mean(xc * xc, axis=-1, keepdims=True)
    return xc * lax.rsqrt(var + LN_EPS) * g + b


def _rms(x, g):
    return x * lax.rsqrt(jnp.mean(x * x, axis=-1, keepdims=True) + RMS_EPS) * g


def _dot(a, b):
    return jnp.dot(a, b, preferred_element_type=F32)


def _dot_nt(a, b):
    return lax.dot_general(a, b, (((1,), (1,)), ((), ())), preferred_element_type=F32)


def _dot_tn(a, b):
    return lax.dot_general(a, b, (((0,), (0,)), ((), ())), preferred_element_type=F32)


def _dot_split(a, b):
    ah = a.astype(BF16)
    al = (a - ah.astype(F32)).astype(BF16)
    bh = b.astype(BF16)
    bl = (b - bh.astype(F32)).astype(BF16)
    return _dot(ah, bh) + _dot(ah, bl) + _dot(al, bh)


def _split3(x):
    hi = x.astype(BF16)
    r = x - hi.astype(F32)
    mid = r.astype(BF16)
    lo = (r - mid.astype(F32)).astype(BF16)
    return hi, mid, lo


def _mod_kernel(c_ref, w_ref, b_ref, o_ref):
    s = _silu(c_ref[...])
    o_ref[0] = _dot(s.astype(BF16), w_ref[0].astype(BF16)) + b_ref[0]


def _modulation(c_all, w_ada, b_ada):
    depth, d, n = w_ada.shape
    rows = c_all.shape[0]
    tn = 1536
    return pl.pallas_call(
        _mod_kernel,
        grid=(depth, n // tn),
        in_specs=[pl.BlockSpec((rows, d), lambda l, j: (0, 0)),
                  pl.BlockSpec((1, d, tn), lambda l, j: (l, 0, j)),
                  pl.BlockSpec((1, 1, tn), lambda l, j: (l, 0, j))],
        out_specs=pl.BlockSpec((1, rows, tn), lambda l, j: (l, 0, j)),
        out_shape=jax.ShapeDtypeStruct((depth, rows, n), F32),
        compiler_params=_cparams(("arbitrary", "arbitrary")),
        name="modulation",
    )(c_all, w_ada, b_ada.reshape(depth, 1, n))


def _inproj_kernel(x_ref, sc_ref, sh_ref, g0_ref, b0_ref, w_ref, o_ref, *, ln0):
    nb, tb, d = x_ref.shape
    x = x_ref[...]
    if ln0:
        x = _ln(x, g0_ref[...], b0_ref[...])
    h = x * (1.0 + sc_ref[...]) + sh_ref[...]
    o = _dot(h.reshape(nb * tb, d).astype(BF16), w_ref[...])
    o_ref[...] = o.reshape(nb, tb, o.shape[-1]).astype(o_ref.dtype)


def _inproj(x, sc, sh, g0, b0, w_fat, out_dtype, *, nb, tb, ln0):
    bsz, t, d = x.shape
    tn = INPROJ_TN
    width = w_fat.shape[1]
    return pl.pallas_call(
        functools.partial(_inproj_kernel, ln0=ln0),
        grid=(width // tn, bsz // nb, t // tb),
        in_specs=[pl.BlockSpec((nb, tb, d), lambda n, b, i: (b, i, 0)),
                  pl.BlockSpec((nb, 1, d), lambda n, b, i: (b, 0, 0)),
                  pl.BlockSpec((nb, 1, d), lambda n, b, i: (b, 0, 0)),
                  pl.BlockSpec((1, d), lambda n, b, i: (0, 0)),
                  pl.BlockSpec((1, d), lambda n, b, i: (0, 0)),
                  pl.BlockSpec((d, tn), lambda n, b, i: (0, n))],
        out_specs=pl.BlockSpec((nb, tb, tn), lambda n, b, i: (b, i, n)),
        out_shape=jax.ShapeDtypeStruct((bsz, t, width), out_dtype),
        compiler_params=_cparams(("arbitrary", "arbitrary", "arbitrary")),
        name="inproj",
    )(x, sc, sh, g0, b0, w_fat)


CONV_A_HALO = 32


def _glu(pre):
    pre = pre.astype(F32)
    return pre[..., :CONV_A_CH] * _sigmoid(pre[..., CONV_A_CH:])


def _conva_kernel(*refs, n_t, nbc, rc):
    if n_t > 1:
        pre_ref, halo_ref, hist_ref, w_ref, cb_ref, g_ref, b_ref, y_ref, ho_ref, xp_ref, xs_ref = refs
    else:
        pre_ref, hist_ref, w_ref, cb_ref, g_ref, b_ref, y_ref, ho_ref, xp_ref, xs_ref = refs
    nb, tb, _ = pre_ref.shape
    t = pl.program_id(1)
    hl = CONV_A_HALO

    @pl.when(t == 0)
    def _():
        xp_ref[:, 0:hl, :] = hist_ref[...]

    if n_t > 1:
        @pl.when(t > 0)
        def _():
            xp_ref[:, 0:hl, :] = _glu(halo_ref[...])

    xp_ref[:, hl:hl + tb, :] = _glu(pre_ref[...])
    xp = xp_ref[...]
    xs_ref[0] = xp
    for b in range(1, SUBLANES):
        xs_ref[b] = pltpu.roll(xp, b, 1)
    for b0 in range(0, nb, nbc):
        for r0 in range(0, tb, rc):
            acc = None
            for k in range(CONV_A_WIDTH):
                back = CONV_A_WIDTH - 1 - k
                whole, part = back // SUBLANES, back % SUBLANES
                start = hl - whole * SUBLANES + r0
                term = w_ref[k:k + 1, :] * xs_ref[part, b0:b0 + nbc, start:start + rc, :]
                acc = term if acc is None else acc + term
            y = _silu(_ln(acc + cb_ref[...], g_ref[...], b_ref[...]))
            y_ref[b0:b0 + nbc, r0:r0 + rc, :] = y.astype(y_ref.dtype)
    ho_ref[...] = xp_ref[:, tb:tb + hl, :]


def _conv_module(fat, hist32, w, cb, g, b, *, nb, tb):
    bsz, t, _ = fat.shape
    n_t = t // tb
    pre_blk = PREA_OFF // (2 * CONV_A_CH)
    hl = CONV_A_HALO
    in_specs = [pl.BlockSpec((nb, tb, 2 * CONV_A_CH), lambda bi, i: (bi, i, pre_blk))]
    args = [fat]
    if n_t > 1:
        per = tb // hl
        in_specs.append(pl.BlockSpec((nb, hl, 2 * CONV_A_CH),
                                     lambda bi, i: (bi, jnp.maximum(i * per - 1, 0), pre_blk)))
        args.append(fat)
    in_specs += [pl.BlockSpec((nb, hl, CONV_A_CH), lambda bi, i: (bi, 0, 0)),
                 pl.BlockSpec((CONV_A_WIDTH, CONV_A_CH), lambda bi, i: (0, 0)),
                 pl.BlockSpec((1, CONV_A_CH), lambda bi, i: (0, 0)),
                 pl.BlockSpec((1, CONV_A_CH), lambda bi, i: (0, 0)),
                 pl.BlockSpec((1, CONV_A_CH), lambda bi, i: (0, 0))]
    args += [hist32, w, cb, g, b]
    if n_t > 1:
        nbc, rc = 1, 64
    else:
        nbc, rc = 4, tb
    return pl.pallas_call(
        functools.partial(_conva_kernel, n_t=n_t, nbc=nbc, rc=rc),
        grid=(bsz // nb, n_t),
        in_specs=in_specs,
        out_specs=[pl.BlockSpec((nb, tb, CONV_A_CH), lambda bi, i: (bi, i, 0)),
                   pl.BlockSpec((nb, hl, CONV_A_CH), lambda bi, i: (bi, 0, 0))],
        out_shape=[jax.ShapeDtypeStruct((bsz, t, CONV_A_CH), BF16),
                   jax.ShapeDtypeStruct((bsz, hl, CONV_A_CH), F32)],
        scratch_shapes=[pltpu.VMEM((nb, hl + tb, CONV_A_CH), F32),
                        pltpu.VMEM((SUBLANES, nb, hl + tb, CONV_A_CH), F32)],
        compiler_params=_cparams(("arbitrary", "arbitrary")),
        name="conv_module",
    )(*args)


GDN_ROWS = 512
GDN_SUB = 128


def _gdn_kernel(*refs, n_t, chunk):
    if n_t > 1:
        (qkv_ref, halo_ref, hist_ref, z_ref, small_ref, st_ref, cw_ref, alog_ref, dt_ref, ng_ref,
         y_ref, ho_ref, so_ref, xp_ref, s_ref, vn_ref, os_ref) = refs
    else:
        (qkv_ref, hist_ref, z_ref, small_ref, st_ref, cw_ref, alog_ref, dt_ref, ng_ref,
         y_ref, ho_ref, so_ref, xp_ref, s_ref, vn_ref, os_ref) = refs
    nb, tb, _ = qkv_ref.shape
    rows = nb * tb
    t = pl.program_id(1)
    hl = SUBLANES
    nk = GDN_HEADS * GDN_DK

    @pl.when(t == 0)
    def _():
        xp_ref[:, 0:hl, :] = hist_ref[...]
        s_ref[...] = st_ref[...]

    if n_t > 1:
        @pl.when(t > 0)
        def _():
            xp_ref[:, 0:hl, :] = halo_ref[...]

    xp_ref[:, hl:hl + tb, :] = qkv_ref[...]
    xp = xp_ref[...]
    acc = cw_ref[0:1, :] * xp
    for k in range(1, GDN_CONV_WIDTH):
        acc = cw_ref[k:k + 1, :] * xp + pltpu.roll(acc, 1, 1)
    c = _silu(acc[:, hl:hl + tb, :]).reshape(rows, GDN_QKV_WIDTH)
    ho_ref[...] = xp_ref[:, tb:tb + hl, :]

    sm = small_ref[...].reshape(rows, LANES)
    beta_all = _sigmoid(sm)
    xg = sm + dt_ref[...]
    softplus = jnp.maximum(xg, 0.0) + jnp.log1p(jnp.exp(-jnp.abs(xg)))
    g_all = -jnp.exp(alog_ref[...]) * softplus

    rs = GDN_SUB
    n_sub = rows // rs
    shift = int(math.log2(chunk))
    ri = lax.broadcasted_iota(jnp.int32, (rs, rs), 0)
    ci = lax.broadcasted_iota(jnp.int32, (rs, rs), 1)
    same = (ri >> shift) == (ci >> shift)
    tri = same & (ci <= ri)
    strict = same & (ci < ri)
    tri_b = jnp.where(tri, 1.0, 0.0).astype(BF16)
    triu_b = jnp.where(same & (ri <= ci), 1.0, 0.0).astype(BF16)
    same_b = jnp.where(same, 1.0, 0.0).astype(BF16)
    gc_col, gc_row, gl_col = [], [], []
    for s in range(n_sub):
        g3 = _split3(g_all[s * rs:(s + 1) * rs])
        gc_col.append(_dot(tri_b, g3[0]) + _dot(tri_b, g3[1]) + _dot(tri_b, g3[2]))
        gc_row.append(_dot_tn(g3[0], triu_b) + _dot_tn(g3[1], triu_b) + _dot_tn(g3[2], triu_b))
        gl_col.append(_dot(same_b, g3[0]) + _dot(same_b, g3[1]) + _dot(same_b, g3[2]))

    chains = [(s, h) for s in range(n_sub) for h in range(GDN_HEADS)]
    each = lambda f, *ls: [f(*xs) for xs in zip(*ls)]
    qs, ks, kbs, rhss, decays, betas, egs, gcs, gls = [], [], [], [], [], [], [], [], []
    for s, h in chains:
        r = slice(s * rs, (s + 1) * rs)
        qh = c[r, h * GDN_DK:(h + 1) * GDN_DK]
        kh = c[r, nk + h * GDN_DK:nk + (h + 1) * GDN_DK]
        vh = c[r, 2 * nk + h * GDN_DV:2 * nk + (h + 1) * GDN_DV]
        qh = qh * lax.rsqrt(jnp.sum(qh * qh, axis=-1, keepdims=True) + RMS_EPS) * (GDN_DK ** -0.5)
        kh = kh * lax.rsqrt(jnp.sum(kh * kh, axis=-1, keepdims=True) + RMS_EPS)
        beta = beta_all[r, BETA_LANE + h:BETA_LANE + h + 1]
        gc = gc_col[s][:, DEC_LANE + h:DEC_LANE + h + 1]
        gr = gc_row[s][DEC_LANE + h:DEC_LANE + h + 1, :]
        eg = jnp.exp(gc)
        qs.append(qh)
        ks.append(kh)
        kbs.append(kh.astype(BF16))
        betas.append(beta)
        gcs.append(gc)
        gls.append(gl_col[s][:, DEC_LANE + h:DEC_LANE + h + 1])
        egs.append(eg)
        decays.append(jnp.where(tri, jnp.exp(jnp.where(tri, gc - gr, 0.0)), 0.0))
        rhss.append(jnp.concatenate([vh * beta, kh * (beta * eg)], axis=1))
    kks = each(lambda kb: _dot_nt(kb, kb), kbs)
    a0s = each(lambda beta, kk, decay: jnp.where(strict, -(beta * kk * decay), 0.0), betas, kks, decays)
    ms = a0s
    pows = a0s
    for _ in range(1, shift):
        pows = each(lambda a: _dot(a.astype(BF16), a.astype(BF16)), pows)
        ms = each(lambda m, a: m + a + _dot(m.astype(BF16), a.astype(BF16)), ms, pows)
    mbs = each(lambda m: m.astype(BF16), ms)
    x0s = each(lambda rhs, mb: rhs + _dot(mb, rhs.astype(BF16)), rhss, mbs)
    ress = each(lambda rhs, x0, a0: rhs - x0 + _dot_split(a0, x0), rhss, x0s, a0s)
    sols = each(lambda x0, res, mb: x0 + res + _dot(mb, res.astype(BF16)), x0s, ress, mbs)
    qkms = each(lambda qh, kb, decay: jnp.where(tri, _dot_nt(qh.astype(BF16), kb) * decay, 0.0).astype(BF16),
                qs, kbs, decays)
    us = each(lambda sol: sol[:, :GDN_DV], sols)
    wbs = each(lambda sol: sol[:, GDN_DV:].astype(BF16), sols)
    qds = each(lambda qh, eg: (qh * eg).astype(BF16), qs, egs)
    kds = each(lambda kh, gl, gc: (kh * jnp.exp(gl - gc)).astype(BF16), ks, gls, gcs)
    egls = each(jnp.exp, gls)
    for j in range(rows // chunk):
        bi = (j * chunk) // tb
        s = (j * chunk) // rs
        r0 = j * chunk - s * rs
        for h in range(GDN_HEADS):
            i = s * GDN_HEADS + h
            sc = s_ref[bi, h]
            sb = sc.astype(BF16)
            vnew = us[i][r0:r0 + chunk] - _dot(wbs[i][r0:r0 + chunk], sb)
            os_ref[h, j * chunk:(j + 1) * chunk, :] = _dot(qds[i][r0:r0 + chunk], sb)
            vn_ref[h, j * chunk:(j + 1) * chunk, :] = vnew
            s_ref[bi, h] = sc * egls[i][r0:r0 + 1, :] + _dot_tn(kds[i][r0:r0 + chunk], vnew.astype(BF16))
    z = z_ref[...].astype(F32).reshape(rows, GDN_HEADS * GDN_DV)
    for i, (s, h) in enumerate(chains):
        r = slice(s * rs, (s + 1) * rs)
        o = os_ref[h, r, :] + _dot(qkms[i], vn_ref[h, r, :].astype(BF16))
        on = _rms(o, ng_ref[...])
        y = (on * _silu(z[r, h * GDN_DV:(h + 1) * GDN_DV])).astype(y_ref.dtype)
        if tb >= rs:
            y_ref[0, s * rs:(s + 1) * rs, h * GDN_DV:(h + 1) * GDN_DV] = y
        else:
            y_ref[s * (rs // tb):(s + 1) * (rs // tb), :, h * GDN_DV:(h + 1) * GDN_DV] = y.reshape(rs // tb, tb, GDN_DV)
    so_ref[...] = s_ref[...]


def _gated_deltanet(fat, fat16, hist8, state, cw, alog_row, dt_row, ng, *, nb, tb):
    bsz, t, _ = fat.shape
    n_t = t // tb
    chunk = min(CHUNK, t)
    hl = SUBLANES
    qkv_blk = QKV_OFF // GDN_QKV_WIDTH
    z_blk = Z_OFF // (GDN_HEADS * GDN_DV)
    small_blk = SMALL_OFF // LANES
    in_specs = [pl.BlockSpec((nb, tb, GDN_QKV_WIDTH), lambda bi, i: (bi, i, qkv_blk))]
    args = [fat]
    if n_t > 1:
        per = tb // hl
        in_specs.append(pl.BlockSpec((nb, hl, GDN_QKV_WIDTH), lambda bi, i: (bi, jnp.maximum(i * per - 1, 0), qkv_blk)))
        args.append(fat)
    in_specs += [pl.BlockSpec((nb, hl, GDN_QKV_WIDTH), lambda bi, i: (bi, 0, 0)),
                 pl.BlockSpec((nb, tb, GDN_HEADS * GDN_DV), lambda bi, i: (bi, i, z_blk)),
                 pl.BlockSpec((nb, tb, LANES), lambda bi, i: (bi, i, small_blk)),
                 pl.BlockSpec((nb, GDN_HEADS, GDN_DK, GDN_DV), lambda bi, i: (bi, 0, 0, 0)),
                 pl.BlockSpec((GDN_CONV_WIDTH, GDN_QKV_WIDTH), lambda bi, i: (0, 0)),
                 pl.BlockSpec((1, LANES), lambda bi, i: (0, 0)),
                 pl.BlockSpec((1, LANES), lambda bi, i: (0, 0)),
                 pl.BlockSpec((1, GDN_DV), lambda bi, i: (0, 0))]
    args += [hist8, fat16, fat, state, cw, alog_row, dt_row, ng]
    rows = nb * tb
    return pl.pallas_call(
        functools.partial(_gdn_kernel, n_t=n_t, chunk=chunk),
        grid=(bsz // nb, n_t),
        in_specs=in_specs,
        out_specs=[pl.BlockSpec((nb, tb, GDN_HEADS * GDN_DV), lambda bi, i: (bi, i, 0)),
                   pl.BlockSpec((nb, hl, GDN_QKV_WIDTH), lambda bi, i: (bi, 0, 0)),
                   pl.BlockSpec((nb, GDN_HEADS, GDN_DK, GDN_DV), lambda bi, i: (bi, 0, 0, 0))],
        out_shape=[jax.ShapeDtypeStruct((bsz, t, GDN_HEADS * GDN_DV), BF16),
                   jax.ShapeDtypeStruct((bsz, hl, GDN_QKV_WIDTH), F32),
                   jax.ShapeDtypeStruct((bsz, GDN_HEADS, GDN_DK, GDN_DV), F32)],
        scratch_shapes=[pltpu.VMEM((nb, hl + tb, GDN_QKV_WIDTH), F32),
                        pltpu.VMEM((nb, GDN_HEADS, GDN_DK, GDN_DV), F32),
                        pltpu.VMEM((GDN_HEADS, rows, GDN_DV), F32),
                        pltpu.VMEM((GDN_HEADS, rows, GDN_DV), F32)],
        compiler_params=_cparams(("arbitrary", "arbitrary")),
        name="gated_deltanet",
    )(*args)


def _mla_prep_kernel(*refs, emit_kv, scale):
    if emit_kv:
        (blk_ref, cs_ref, sn_ref, qg_ref, kg_ref, wq_ref, wqs_ref, wk_ref, wvt_ref,
         q_ref, lat_ref, kpe_ref, k_ref, vt_ref) = refs
    else:
        (blk_ref, cs_ref, sn_ref, qg_ref, kg_ref, wq_ref, wqs_ref,
         q_ref, lat_ref, kpe_ref) = refs
    nb, tb, _ = blk_ref.shape
    rows = nb * tb
    blk = blk_ref[...]
    cs = cs_ref[...]
    sn = sn_ref[...]
    qn = _rms(blk[..., :MLA_Q_LORA], qg_ref[...]).reshape(rows, MLA_Q_LORA).astype(BF16)
    qf = _dot(qn, wq_ref[...]).reshape(nb, tb, MLA_HEADS * QK_PAD)
    qs = _dot(qn, wqs_ref[...]).reshape(nb, tb, MLA_HEADS * LANES)
    for h in range(MLA_HEADS):
        nope = qf[..., h * QK_PAD:h * QK_PAD + MLA_NOPE_DIM]
        rope = qf[..., h * QK_PAD + MLA_NOPE_DIM:(h + 1) * QK_PAD] * cs + qs[..., h * LANES:(h + 1) * LANES] * sn
        q_ref[h, :, :, 0:MLA_NOPE_DIM] = (nope * scale).astype(q_ref.dtype)
        q_ref[h, :, :, MLA_NOPE_DIM:QK_PAD] = (rope * scale).astype(q_ref.dtype)

    lat = _rms(blk[..., MLA_Q_LORA + LANES:], kg_ref[...])
    lat_ref[...] = lat

    small = blk[..., MLA_Q_LORA:MLA_Q_LORA + LANES].reshape(rows, LANES)
    half = MLA_ROPE_DIM // 2
    lane = lax.broadcasted_iota(jnp.int32, (rows, LANES), 1)
    swapped = jnp.where(lane < half, pltpu.roll(small, LANES - half, 1), pltpu.roll(small, half, 1))
    kpe = small.reshape(nb, tb, LANES) * cs + swapped.reshape(nb, tb, LANES) * sn
    kpe_ref[...] = kpe[..., :MLA_ROPE_DIM]

    if emit_kv:
        latb = lat.reshape(rows, MLA_KV_LORA).astype(BF16)
        kn = _dot(latb, wk_ref[...]).reshape(nb, tb, MLA_HEADS * MLA_NOPE_DIM)
        ones_rows = jnp.where(lax.broadcasted_iota(jnp.int32, (VT_ROWS - MLA_V_DIM, ATTN_KB), 0) == 0,
                              1.0, 0.0).astype(vt_ref.dtype)
        for h in range(MLA_HEADS):
            k_ref[h, :, :, 0:MLA_NOPE_DIM] = kn[..., h * MLA_NOPE_DIM:(h + 1) * MLA_NOPE_DIM].astype(k_ref.dtype)
            k_ref[h, :, :, MLA_NOPE_DIM:QK_PAD] = kpe.astype(k_ref.dtype)
            vt = _dot_nt(wvt_ref[h], latb).astype(vt_ref.dtype)
            for kb in range(rows // ATTN_KB):
                vt_ref[h, 0, kb, 0:MLA_V_DIM, :] = vt[:, kb * ATTN_KB:(kb + 1) * ATTN_KB]
                vt_ref[h, 0, kb, MLA_V_DIM:VT_ROWS, :] = ones_rows


def _mla_prep(fat, cs, sn, qg, kg, wq, wqs, wk, wvt, *, nb, tb, emit_kv, scale):
    bsz, t, _ = fat.shape
    blk = QLAT_OFF // MLA_BLK_W
    hq = MLA_HEADS
    in_specs = [pl.BlockSpec((nb, tb, MLA_BLK_W), lambda bi, i: (bi, i, blk)),
                pl.BlockSpec((tb, LANES), lambda bi, i: (i, 0)),
                pl.BlockSpec((tb, LANES), lambda bi, i: (i, 0)),
                pl.BlockSpec((1, MLA_Q_LORA), lambda bi, i: (0, 0)),
                pl.BlockSpec((1, MLA_KV_LORA), lambda bi, i: (0, 0)),
                pl.BlockSpec(wq.shape, lambda bi, i: (0, 0)),
                pl.BlockSpec(wqs.shape, lambda bi, i: (0, 0))]
    args = [fat, cs, sn, qg, kg, wq, wqs]
    out_specs = [pl.BlockSpec((hq, nb, tb, QK_PAD), lambda bi, i: (0, bi, i, 0)),
                 pl.BlockSpec((nb, tb, MLA_KV_LORA), lambda bi, i: (bi, i, 0)),
                 pl.BlockSpec((nb, tb, MLA_ROPE_DIM), lambda bi, i: (bi, i, 0))]
    out_shape = [jax.ShapeDtypeStruct((hq, bsz, t, QK_PAD), BF16),
                 jax.ShapeDtypeStruct((bsz, t, MLA_KV_LORA), F32),
                 jax.ShapeDtypeStruct((bsz, t, MLA_ROPE_DIM), F32)]
    if emit_kv:
        assert nb == 1 and tb % ATTN_KB == 0
        in_specs += [pl.BlockSpec(wk.shape, lambda bi, i: (0, 0)),
                     pl.BlockSpec(wvt.shape, lambda bi, i: (0, 0, 0))]
        args += [wk, wvt]
        per = tb // ATTN_KB
        out_specs += [pl.BlockSpec((hq, nb, tb, QK_PAD), lambda bi, i: (0, bi, i, 0)),
                      pl.BlockSpec((hq, nb, per, VT_ROWS, ATTN_KB), lambda bi, i: (0, bi, i, 0, 0))]
        out_shape += [jax.ShapeDtypeStruct((hq, bsz, t, QK_PAD), BF16),
                      jax.ShapeDtypeStruct((hq, bsz, t // ATTN_KB, VT_ROWS, ATTN_KB), BF16)]
    return pl.pallas_call(
        functools.partial(_mla_prep_kernel, emit_kv=emit_kv, scale=scale),
        grid=(bsz // nb, t // tb),
        in_specs=in_specs, out_specs=out_specs, out_shape=out_shape,
        compiler_params=_cparams(("arbitrary", "arbitrary")),
        name="mla_prep",
    )(*args)


def _attn_prompt_kernel(q_ref, k_ref, vt_ref, o_ref, m_ref, acc_ref, st_ref, *, nq):
    kb = ATTN_KB
    j = pl.program_id(2)
    m_ref[...] = jnp.full(m_ref.shape, -jnp.inf, F32)
    acc_ref[...] = jnp.zeros(acc_ref.shape, F32)

    def scores(chains, ki):
        k0 = pl.multiple_of(ki * kb, kb)
        kblk = k_ref[0, 0, pl.ds(k0, kb), :]
        return [_dot_nt(kblk, q_ref[0, 0, c * kb:(c + 1) * kb, :]) for c in chains]

    def finish(chains, sts, ki, masked_chain):
        pts, alphas = [], []
        for c, st in zip(chains, sts):
            if c == masked_chain:
                key_chunk = lax.broadcasted_iota(jnp.int32, (kb, kb), 0) // CHUNK
                qry_chunk = lax.broadcasted_iota(jnp.int32, (kb, kb), 1) // CHUNK
                st = jnp.where(key_chunk <= qry_chunk, st, -jnp.inf)
            m_old = m_ref[c]
            m_new = jnp.maximum(m_old, jnp.max(st, axis=0, keepdims=True))
            pts.append(jnp.exp2(st - m_new).astype(BF16))
            alphas.append(jnp.exp2(m_old - m_new))
            m_ref[c] = m_new
        vblk = vt_ref[0, 0, ki]
        for c, pt, alpha in zip(chains, pts, alphas):
            acc_ref[c] = alpha * acc_ref[c] + _dot(vblk, pt)

    everyone = range(nq)
    for c, st in zip(everyone, scores(everyone, 0)):
        st_ref[c] = st

    def step(ki):
        cur = [st_ref[c] for c in everyone]
        nxt = scores(everyone, ki + 1)
        finish(everyone, cur, ki, None)
        for c in everyone:
            st_ref[c] = nxt[c]

    per_trip = 2 if nq % 2 == 0 else 1

    def body(kp, carry):
        for u in range(per_trip):
            step(per_trip * kp + u)
        return carry

    lax.fori_loop(0, (nq // per_trip) * j, body, 0)
    cur = [st_ref[c] for c in everyone]
    for d in range(nq):
        nxt = scores(range(d + 1, nq), nq * j + d + 1) if d + 1 < nq else []
        finish(range(d, nq), cur, nq * j + d, d)
        cur = nxt
    for c in range(nq):
        acc = acc_ref[c]
        ot = acc[0:MLA_V_DIM] / acc[MLA_V_DIM:MLA_V_DIM + 1]
        o_ref[0, c * kb:(c + 1) * kb, :] = ot.T.astype(o_ref.dtype)


def _attn_prompt(q, k, vt, *, nq):
    hq, bsz, t, _ = q.shape
    tq = ATTN_KB * nq
    return pl.pallas_call(
        functools.partial(_attn_prompt_kernel, nq=nq),
        grid=(bsz, hq, t // tq),
        in_specs=[pl.BlockSpec((1, 1, tq, QK_PAD), lambda b, h, i: (h, b, i, 0)),
                  pl.BlockSpec((1, 1, t, QK_PAD), lambda b, h, i: (h, b, 0, 0)),
                  pl.BlockSpec((1, 1, t // ATTN_KB, VT_ROWS, ATTN_KB), lambda b, h, i: (h, b, 0, 0, 0))],
        out_specs=pl.BlockSpec((1, tq, MLA_V_DIM), lambda b, h, i: (b, i, h)),
        out_shape=jax.ShapeDtypeStruct((bsz, t, MLA_HEADS * MLA_V_DIM), BF16),
        scratch_shapes=[pltpu.VMEM((nq, 1, ATTN_KB), F32), pltpu.VMEM((nq, VT_ROWS, ATTN_KB), F32),
                        pltpu.VMEM((nq, ATTN_KB, ATTN_KB), F32)],
        compiler_params=_cparams(("arbitrary", "arbitrary", "arbitrary")),
        name="attn_prompt",
    )(q, k, vt)


def _attn_sample_kernel(q_ref, latc_ref, kpec_ref, latn_ref, kpen_ref, wuk_ref, wuv_ref, o_ref):
    tn = latn_ref.shape[1]
    past = latc_ref.shape[2]
    hq = MLA_HEADS
    qabs = jnp.concatenate(
        [_dot(q_ref[h, 0, :, 0:MLA_NOPE_DIM], wuk_ref[h]) for h in range(hq)], axis=0).astype(BF16)
    qrope = jnp.concatenate([q_ref[h, 0, :, MLA_NOPE_DIM:MLA_NOPE_DIM + MLA_ROPE_DIM] for h in range(hq)], axis=0)
    latc = latc_ref[0, 0].astype(BF16)
    latn = latn_ref[0].astype(BF16)
    sc = _dot_nt(qabs, latc) + _dot_nt(qrope, kpec_ref[0, 0].astype(BF16))
    sn = _dot_nt(qabs, latn) + _dot_nt(qrope, kpen_ref[0].astype(BF16))
    rows = hq * tn
    qpos = past + lax.broadcasted_iota(jnp.int32, (rows, 1), 0) % tn
    kc_pos = lax.broadcasted_iota(jnp.int32, (rows, past), 1)
    kn_pos = past + lax.broadcasted_iota(jnp.int32, (rows, tn), 1)
    sc = jnp.where(kc_pos // CHUNK <= qpos // CHUNK, sc, -jnp.inf)
    sn = jnp.where(kn_pos // CHUNK <= qpos // CHUNK, sn, -jnp.inf)
    m = jnp.maximum(jnp.max(sc, axis=-1, keepdims=True), jnp.max(sn, axis=-1, keepdims=True))
    pc = jnp.exp(sc - m)
    pn = jnp.exp(sn - m)
    l = jnp.sum(pc, axis=-1, keepdims=True) + jnp.sum(pn, axis=-1, keepdims=True)
    olat = (_dot(pc.astype(BF16), latc) + _dot(pn.astype(BF16), latn)) / l
    for h in range(hq):
        oh = _dot(olat[h * tn:(h + 1) * tn].astype(BF16), wuv_ref[h])
        o_ref[0, :, h * MLA_V_DIM:(h + 1) * MLA_V_DIM] = oh.astype(o_ref.dtype)


def _attn_sample(q, caches_lat, caches_kpe, layer, lat_new, kpe_new, wuk_t, wuv):
    hq, bsz, tn, _ = q.shape
    past = caches_lat.shape[2]
    return pl.pallas_call(
        _attn_sample_kernel,
        grid=(bsz,),
        in_specs=[pl.BlockSpec((hq, 1, tn, QK_PAD), lambda b: (0, b, 0, 0)),
                  pl.BlockSpec((1, 1, past, MLA_KV_LORA), lambda b: (layer, b, 0, 0)),
                  pl.BlockSpec((1, 1, past, MLA_ROPE_DIM), lambda b: (layer, b, 0, 0)),
                  pl.BlockSpec((1, tn, MLA_KV_LORA), lambda b: (b, 0, 0)),
                  pl.BlockSpec((1, tn, MLA_ROPE_DIM), lambda b: (b, 0, 0)),
                  pl.BlockSpec(wuk_t.shape, lambda b: (0, 0, 0)),
                  pl.BlockSpec(wuv.shape, lambda b: (0, 0, 0))],
        out_specs=pl.BlockSpec((1, tn, MLA_HEADS * MLA_V_DIM), lambda b: (b, 0, 0)),
        out_shape=jax.ShapeDtypeStruct((bsz, tn, MLA_HEADS * MLA_V_DIM), BF16),
        compiler_params=_cparams(("arbitrary",)),
        name="attn_sample",
    )(q, caches_lat, caches_kpe, lat_new, kpe_new, wuk_t, wuv)


def _merge_kernel(ya_ref, yb_ref, yc_ref, gate_ref, x_ref, g1_ref, g0_ref, b0_ref, wb_ref, wo_ref,
                  lg_ref, lb_ref, o_ref, *, ln0, alpha):
    nb, tb, d = x_ref.shape
    rows = nb * tb
    x = x_ref[...]
    if ln0:
        x = _ln(x, g0_ref[...], b0_ref[...])
    merged = None
    for i, y_ref in enumerate((ya_ref, yb_ref, yc_ref)):
        proj = _dot(y_ref[...].reshape(rows, BRANCH_WIDTH), wb_ref[i])
        gate = _sigmoid(gate_ref[:, :, i * d:(i + 1) * d].astype(F32)).reshape(rows, d)
        merged = gate * proj if merged is None else merged + gate * proj
    out = _dot(merged.astype(BF16), wo_ref[...]).reshape(nb, tb, d)
    o_ref[...] = _ln(alpha * x + (1.0 + g1_ref[...]) * out, lg_ref[...], lb_ref[...])


def _merge(ya, yb, yc, fat, x, g1, g0, b0, wb, wo, lg, lb, *, nb, tb, ln0, alpha):
    bsz, t, d = x.shape
    row = lambda bi, i: (bi, i, 0)
    const2 = lambda bi, i: (0, 0)
    return pl.pallas_call(
        functools.partial(_merge_kernel, ln0=ln0, alpha=alpha),
        grid=(bsz // nb, t // tb),
        in_specs=[pl.BlockSpec((nb, tb, BRANCH_WIDTH), row),
                  pl.BlockSpec((nb, tb, BRANCH_WIDTH), row),
                  pl.BlockSpec((nb, tb, BRANCH_WIDTH), row),
                  pl.BlockSpec((nb, tb, GATE_W), row),
                  pl.BlockSpec((nb, tb, d), row),
                  pl.BlockSpec((nb, 1, d), lambda bi, i: (bi, 0, 0)),
                  pl.BlockSpec((1, d), const2),
                  pl.BlockSpec((1, d), const2),
                  pl.BlockSpec(wb.shape, lambda bi, i: (0, 0, 0)),
                  pl.BlockSpec(wo.shape, const2),
                  pl.BlockSpec((1, d), const2),
                  pl.BlockSpec((1, d), const2)],
        out_specs=pl.BlockSpec((nb, tb, d), row),
        out_shape=jax.ShapeDtypeStruct((bsz, t, d), F32),
        compiler_params=_cparams(("arbitrary", "arbitrary")),
        name="merge",
    )(ya, yb, yc, fat, x, g1, g0, b0, wb, wo, lg, lb)


FFN_BLOCK = 1408


def _ffn_kernel(x_ref, sc_ref, sh_ref, g2_ref, wa_ref, wv_ref, wd_ref, cw_ref, cb_ref, lg_ref, lb_ref, hist_ref,
                o_ref, ho_ref, h_ref, acc_ref, carry_ref, ap_ref, *, alpha):
    nb, tb, d = x_ref.shape
    rows = nb * tb
    t = pl.program_id(1)
    f = pl.program_id(2)
    n_f = pl.num_programs(2)
    hl = SUBLANES
    fb = wa_ref.shape[1]

    @pl.when(f == 0)
    def _():
        h = x_ref[...] * (1.0 + sc_ref[...]) + sh_ref[...]
        h_ref[...] = h.reshape(rows, d).astype(BF16)

    @pl.when(t == 0)
    def _():
        carry_ref[f] = hist_ref[...]

    h = h_ref[...]
    a = _dot(h, wa_ref[...]).reshape(nb, tb, fb)
    v = _dot(h, wv_ref[...]).reshape(nb, tb, fb)
    ap_ref[:, 0:hl, :] = carry_ref[f]
    ap_ref[:, hl:hl + tb, :] = a
    ap = ap_ref[...]
    conv = cw_ref[0:1, :] * ap
    for k in range(1, FFN_CONV_WIDTH):
        conv = cw_ref[k:k + 1, :] * ap + pltpu.roll(conv, 1, 1)
    act = _silu(conv[:, hl:hl + tb, :] + cb_ref[...]) * v
    contrib = _dot(act.reshape(rows, fb).astype(BF16), wd_ref[...])
    last = ap_ref[:, tb:tb + hl, :]
    carry_ref[f] = last
    ho_ref[0] = last

    @pl.when(f == 0)
    def _():
        acc_ref[...] = contrib

    @pl.when(f > 0)
    def _():
        acc_ref[...] = acc_ref[...] + contrib

    @pl.when(f == n_f - 1)
    def _():
        y = acc_ref[...].reshape(nb, tb, d)
        o_ref[...] = _ln(alpha * x_ref[...] + (1.0 + g2_ref[...]) * y, lg_ref[...], lb_ref[...])


def _ffn(x, sc, sh, g2, wa, wv, wd, cw, cb, lg, lb, hist8, *, nb, tb, alpha):
    bsz, t, d = x.shape
    dff = wa.shape[1]
    fb = FFN_BLOCK
    n_f = dff // fb
    hl = SUBLANES
    rows = nb * tb
    row = lambda bi, i, f: (bi, i, 0)
    mod = lambda bi, i, f: (bi, 0, 0)
    const2 = lambda bi, i, f: (0, 0)
    return pl.pallas_call(
        functools.partial(_ffn_kernel, alpha=alpha),
        grid=(bsz // nb, t // tb, n_f),
        in_specs=[pl.BlockSpec((nb, tb, d), row),
                  pl.BlockSpec((nb, 1, d), mod),
                  pl.BlockSpec((nb, 1, d), mod),
                  pl.BlockSpec((nb, 1, d), mod),
                  pl.BlockSpec((d, fb), lambda bi, i, f: (0, f)),
                  pl.BlockSpec((d, fb), lambda bi, i, f: (0, f)),
                  pl.BlockSpec((fb, d), lambda bi, i, f: (f, 0)),
                  pl.BlockSpec((FFN_CONV_WIDTH, fb), lambda bi, i, f: (0, f)),
                  pl.BlockSpec((1, fb), lambda bi, i, f: (0, f)),
                  pl.BlockSpec((1, d), const2),
                  pl.BlockSpec((1, d), const2),
                  pl.BlockSpec((nb, hl, fb), lambda bi, i, f: (bi, 0, f))],
        out_specs=[pl.BlockSpec((nb, tb, d), row),
                   pl.BlockSpec((1, nb, hl, fb), lambda bi, i, f: (i, bi, 0, f))],
        out_shape=[jax.ShapeDtypeStruct((bsz, t, d), F32),
                   jax.ShapeDtypeStruct((t // tb, bsz, hl, dff), F32)],
        scratch_shapes=[pltpu.VMEM((rows, d), BF16),
                        pltpu.VMEM((rows, d), F32),
                        pltpu.VMEM((n_f, nb, hl, fb), F32),
                        pltpu.VMEM((nb, hl + tb, fb), F32)],
        compiler_params=_cparams(("arbitrary", "arbitrary", "arbitrary")),
        name="conv_ffn",
    )(x, sc, sh, g2, wa, wv, wd, cw, cb, lg, lb, hist8)


def _front_pad(hist, rows):
    b, r, c = hist.shape
    return jnp.concatenate([jnp.zeros((b, rows - r, c), hist.dtype), hist], axis=1)


def _rope_rows(pos):
    half = MLA_ROPE_DIM // 2
    inv_freq = ROPE_THETA ** (-jnp.arange(half, dtype=F32) / half)
    ang = pos.astype(F32)[:, None] * inv_freq[None, :]
    cos, sin = jnp.cos(ang), jnp.sin(ang)
    pad = jnp.zeros((pos.shape[0], LANES - MLA_ROPE_DIM), F32)
    return jnp.concatenate([cos, cos, pad], axis=1), jnp.concatenate([-sin, sin, pad], axis=1)


def _lane_row(vals, lane0):
    return jnp.zeros((1, LANES), F32).at[0, lane0:lane0 + vals.shape[0]].set(vals.astype(F32))


W_IN_OFFS = np.cumsum([0, 2 * CONV_A_CH, GDN_QKV_WIDTH, GDN_HEADS * GDN_DV, GDN_HEADS, GDN_HEADS,
                       MLA_Q_LORA, MLA_KV_LORA, MLA_ROPE_DIM, N_BRANCH * D_MODEL]).tolist()
W_PREP_ROWS = 128


def _w_in_prep_kernel(w_ref, o32_ref, o16_ref):
    w = w_ref[0]
    seg = lambda i: w[:, W_IN_OFFS[i]:W_IN_OFFS[i + 1]].astype(BF16)
    pad = jnp.zeros((w.shape[0], LANES - MLA_ROPE_DIM - 2 * GDN_HEADS), BF16)
    o32_ref[...] = jnp.concatenate([seg(1), seg(5), seg(7), seg(3), seg(4), pad, seg(6)], axis=1)
    o16_ref[...] = jnp.concatenate([seg(8), seg(0), seg(2)], axis=1)


def _w_in_prep(w_in, l):
    _, d, n = w_in.shape
    return pl.pallas_call(
        _w_in_prep_kernel,
        grid=(d // W_PREP_ROWS,),
        in_specs=[pl.BlockSpec((1, W_PREP_ROWS, n), lambda i: (l, i, 0))],
        out_specs=[pl.BlockSpec((W_PREP_ROWS, FAT32_W), lambda i: (i, 0)),
                   pl.BlockSpec((W_PREP_ROWS, FAT16_W), lambda i: (i, 0))],
        out_shape=[jax.ShapeDtypeStruct((d, FAT32_W), BF16), jax.ShapeDtypeStruct((d, FAT16_W), BF16)],
        compiler_params=_cparams(("arbitrary",)),
        name="w_in_prep",
    )(w_in)


def _layer_weights(l, w_in, mla_w_uq, mla_w_ukv, w_branch, w_out, w_up, w_down):
    w_fat32, w_fat16 = _w_in_prep(w_in, l)
    hd = MLA_NOPE_DIM + MLA_ROPE_DIM
    half = MLA_ROPE_DIM // 2
    wq = mla_w_uq[l]
    zq = jnp.zeros((MLA_Q_LORA, QK_PAD - hd), wq.dtype)
    zs = jnp.zeros((MLA_Q_LORA, LANES - MLA_ROPE_DIM), wq.dtype)
    wq_cols, wqs_cols = [], []
    for h in range(MLA_HEADS):
        wq_cols += [wq[:, h * hd:(h + 1) * hd], zq]
        r0 = h * hd + MLA_NOPE_DIM
        wqs_cols += [wq[:, r0 + half:r0 + 2 * half], wq[:, r0:r0 + half], zs]
    wkv = mla_w_ukv[l]
    kvw = MLA_NOPE_DIM + MLA_V_DIM
    wuk_t = jnp.stack([wkv[:, h * kvw:h * kvw + MLA_NOPE_DIM].T for h in range(MLA_HEADS)]).astype(BF16)
    wuv = jnp.stack([wkv[:, h * kvw + MLA_NOPE_DIM:(h + 1) * kvw] for h in range(MLA_HEADS)]).astype(BF16)
    return dict(
        w_fat32=w_fat32, w_fat16=w_fat16,
        wq=jnp.concatenate(wq_cols, axis=1).astype(BF16),
        wqs=jnp.concatenate(wqs_cols, axis=1).astype(BF16),
        wk=jnp.concatenate([wkv[:, h * kvw:h * kvw + MLA_NOPE_DIM] for h in range(MLA_HEADS)], axis=1).astype(BF16),
        wvt=jnp.transpose(wuv, (0, 2, 1)), wuk_t=wuk_t, wuv=wuv,
        wb=w_branch[l].astype(BF16), wo=w_out[l].astype(BF16),
        wa=w_up[l][:, :D_FF].astype(BF16), wv=w_up[l][:, D_FF:].astype(BF16), wd=w_down[l].astype(BF16),
    )


def _run_layer(x, mods, st, wts, p, *, ln0, g0, b0, alpha, cs, sn, tiles, prompt):
    bsz, t, d = x.shape
    sh1, sc1, g1, sh2, sc2, g2 = mods
    nb, tb, gdn_nb, gdn_tb = tiles
    if st is None:
        hist_a = jnp.zeros((bsz, CONV_A_WIDTH - 1, CONV_A_CH), F32)
        hist_b = jnp.zeros((bsz, GDN_CONV_WIDTH - 1, GDN_QKV_WIDTH), F32)
        state_b = jnp.zeros((bsz, GDN_HEADS, GDN_DK, GDN_DV), F32)
        hist_f = jnp.zeros((bsz, FFN_CONV_WIDTH - 1, D_FF), F32)
    else:
        caches_lat, caches_kpe, layer, hist_a, hist_b, state_b, hist_f = st
    row = lambda v: v.reshape(1, -1)

    fat = _inproj(x, sc1, sh1, g0, b0, wts['w_fat32'], F32, nb=nb, tb=tb, ln0=ln0)
    fat16 = _inproj(x, sc1, sh1, g0, b0, wts['w_fat16'], BF16, nb=nb, tb=tb, ln0=ln0)
    y_a, ha = _conv_module(fat16, _front_pad(hist_a, CONV_A_HALO), p['conv_a_w'], row(p['conv_a_b']),
                           row(p['ln_a_g']), row(p['ln_a_b']), nb=nb, tb=tb)
    y_b, hb, sb = _gated_deltanet(fat, fat16, _front_pad(hist_b, SUBLANES), state_b, p['gdn_conv_w'],
                                  _lane_row(p['gdn_a_log'], DEC_LANE), _lane_row(p['gdn_dt_bias'], DEC_LANE),
                                  row(p['gdn_norm_g']), nb=gdn_nb, tb=gdn_tb)
    if prompt:
        q, lat, kpe, k, v = _mla_prep(fat, cs, sn, row(p['mla_q_norm_g']), row(p['mla_kv_norm_g']),
                                      wts['wq'], wts['wqs'], wts['wk'], wts['wvt'], nb=nb, tb=tb, emit_kv=True,
                                      scale=ATTN_SCALE * math.log2(math.e))
        y_c = _attn_prompt(q, k, v, nq=math.gcd(ATTN_CHAINS, t // ATTN_KB))
    else:
        q, lat, kpe = _mla_prep(fat, cs, sn, row(p['mla_q_norm_g']), row(p['mla_kv_norm_g']),
                                wts['wq'], wts['wqs'], None, None, nb=nb, tb=tb, emit_kv=False, scale=ATTN_SCALE)
        y_c = _attn_sample(q, caches_lat, caches_kpe, layer, lat, kpe, wts['wuk_t'], wts['wuv'])
    x1 = _merge(y_a, y_b, y_c, fat16, x, g1, g0, b0, wts['wb'], wts['wo'], row(p['ln1_g']), row(p['ln1_b']),
                nb=nb, tb=tb, ln0=ln0, alpha=alpha)
    x2, hf = _ffn(x1, sc2, sh2, g2, wts['wa'], wts['wv'], wts['wd'], p['ffn_conv_w'], row(p['ffn_conv_b']),
                  row(p['ln2_g']), row(p['ln2_b']), _front_pad(hist_f, SUBLANES), nb=nb, tb=tb, alpha=alpha)
    states = (lat, kpe, ha[:, CONV_A_HALO - (CONV_A_WIDTH - 1):], hb[:, SUBLANES - (GDN_CONV_WIDTH - 1):],
              sb, hf[-1, :, SUBLANES - (FFN_CONV_WIDTH - 1):])
    return x2, states


def kernel(x_prompt, x_sample, cache_mla_latent, cache_mla_kpe, state_conv_a, state_gdn_conv, state_gdn, state_ffn_conv, c_prompt, c_sample, ln0_g, ln0_b, w_ada, b_ada, w_in, conv_a_w, conv_a_b, ln_a_g, ln_a_b, gdn_conv_w, gdn_a_log, gdn_dt_bias, gdn_norm_g, mla_q_norm_g, mla_kv_norm_g, mla_w_uq, mla_w_ukv, w_branch, w_out, ln1_g, ln1_b, w_up, ffn_conv_w, ffn_conv_b, w_down, ln2_g, ln2_b):
    depth = w_ada.shape[0]
    bp, seq, d = x_prompt.shape
    bs, dec_seq, _ = x_sample.shape
    past = cache_mla_latent.shape[2]
    alpha = (2 * depth) ** 0.25

    n_c = bp + bs
    c_rows = -(-n_c // SUBLANES) * SUBLANES
    c_all = jnp.concatenate([c_prompt, c_sample, jnp.zeros((c_rows - n_c, d), F32)], axis=0)
    mod = _modulation(c_all, w_ada, b_ada)

    cs_p, sn_p = _rope_rows(jnp.arange(seq, dtype=jnp.int32))
    cs_s, sn_s = _rope_rows(past + jnp.arange(dec_seq, dtype=jnp.int32))
    g0, b0 = ln0_g.reshape(1, d), ln0_b.reshape(1, d)

    tile_p = min(512, seq)
    tiles_p = (1, tile_p, 1, min(GDN_ROWS, seq))
    tiles_s = (bs, dec_seq, min(bs, GDN_ROWS // dec_seq), dec_seq)

    xp, xs = x_prompt, x_sample
    p_states, s_states = [], []
    for l in range(depth):
        p = {'conv_a_w': conv_a_w[l], 'conv_a_b': conv_a_b[l], 'ln_a_g': ln_a_g[l], 'ln_a_b': ln_a_b[l],
             'gdn_conv_w': gdn_conv_w[l], 'gdn_a_log': gdn_a_log[l], 'gdn_dt_bias': gdn_dt_bias[l],
             'gdn_norm_g': gdn_norm_g[l], 'mla_q_norm_g': mla_q_norm_g[l], 'mla_kv_norm_g': mla_kv_norm_g[l],
             'ln1_g': ln1_g[l], 'ln1_b': ln1_b[l], 'ffn_conv_w': ffn_conv_w[l], 'ffn_conv_b': ffn_conv_b[l],
             'ln2_g': ln2_g[l], 'ln2_b': ln2_b[l]}
        wts = _layer_weights(l, w_in, mla_w_uq, mla_w_ukv, w_branch, w_out, w_up, w_down)
        mods_p = [m.reshape(bp, 1, d) for m in jnp.split(mod[l, :bp], 6, axis=-1)]
        mods_s = [m.reshape(bs, 1, d) for m in jnp.split(mod[l, bp:n_c], 6, axis=-1)]
        xp, st_p = _run_layer(xp, mods_p, None, wts, p, ln0=(l == 0), g0=g0, b0=b0, alpha=alpha,
                              cs=cs_p, sn=sn_p, tiles=tiles_p, prompt=True)
        st_in = (cache_mla_latent, cache_mla_kpe, l, state_conv_a[l], state_gdn_conv[l], state_gdn[l],
                 state_ffn_conv[l])
        xs, st_s = _run_layer(xs, mods_s, st_in, wts, p, ln0=(l == 0), g0=g0, b0=b0, alpha=alpha,
                              cs=cs_s, sn=sn_s, tiles=tiles_s, prompt=False)
        p_states.append(st_p)
        s_states.append(st_s)
    p_out = [jnp.stack(z, axis=0) for z in zip(*p_states)]
    s_out = [jnp.stack(z, axis=0) for z in zip(*s_states)]
    return (xp, xs, *p_out, *s_out)
```

```python
import functools
import math

import jax
import jax.numpy as jnp
import numpy as np
from jax import lax
from jax.experimental import pallas as pl
from jax.experimental.pallas import tpu as pltpu

F32 = jnp.float32
BF16 = jnp.bfloat16

D_MODEL = 1024
CHUNK = 64
CONV_A_CH = 512
CONV_A_WIDTH = 31
GDN_HEADS = 4
GDN_DK = 128
GDN_DV = 128
GDN_CONV_WIDTH = 4
GDN_QKV_WIDTH = 2 * GDN_HEADS * GDN_DK + GDN_HEADS * GDN_DV
MLA_HEADS = 4
MLA_Q_LORA = 384
MLA_KV_LORA = 256
MLA_NOPE_DIM = 128
MLA_ROPE_DIM = 64
MLA_V_DIM = 128
ROPE_THETA = 10000.0
N_BRANCH = 3
BRANCH_WIDTH = 512
D_FF = 2816
FFN_CONV_WIDTH = 3
LN_EPS = 1e-5
RMS_EPS = 1e-6

LANES = 128
SUBLANES = 8
VMEM_LIMIT = 48 * 1024 * 1024

QKV_OFF = 0
QLAT_OFF = QKV_OFF + GDN_QKV_WIDTH
SMALL_OFF = QLAT_OFF + MLA_Q_LORA
KVLAT_OFF = SMALL_OFF + LANES
FAT32_W = KVLAT_OFF + MLA_KV_LORA
MLA_BLK_W = FAT32_W - QLAT_OFF
GATE_OFF, GATE_W = 0, N_BRANCH * D_MODEL
PREA_OFF = GATE_OFF + GATE_W
Z_OFF = PREA_OFF + 2 * CONV_A_CH
FAT16_W = Z_OFF + GDN_HEADS * GDN_DV
INPROJ_TN = FAT32_W
BETA_LANE = MLA_ROPE_DIM
DEC_LANE = MLA_ROPE_DIM + GDN_HEADS
QK_PAD = 2 * LANES
ATTN_KB = 256
ATTN_CHAINS = 8
VT_ROWS = MLA_V_DIM + 16
ATTN_SCALE = (MLA_NOPE_DIM + MLA_ROPE_DIM) ** -0.5


def _cparams(sem):
    return pltpu.CompilerParams(dimension_semantics=sem, vmem_limit_bytes=VMEM_LIMIT)


def _sigmoid(x):
    return jax.nn.sigmoid(x)


def _silu(x):
    return x * jax.nn.sigmoid(x)


def _ln(x, g, b):
    mu = jnp.mean(x, axis=-1, keepdims=True)
    xc = x - mu
    var = jnp.mean(xc * xc, axis=-1, keepdims=True)
    return xc * lax.rsqrt(var + LN_EPS) * g + b


def _rms(x, g):
    return x * lax.rsqrt(jnp.mean(x * x, axis=-1, keepdims=True) + RMS_EPS) * g


def _dot(a, b):
    return jnp.dot(a, b, preferred_element_type=F32)


def _dot_nt(a, b):
    return lax.dot_general(a, b, (((1,), (1,)), ((), ())), preferred_element_type=F32)


def _dot_tn(a, b):
    return lax.dot_general(a, b, (((0,), (0,)), ((), ())), preferred_element_type=F32)


def _dot_split(a, b):
    ah = a.astype(BF16)
    al = (a - ah.astype(F32)).astype(BF16)
    bh = b.astype(BF16)
    bl = (b - bh.astype(F32)).astype(BF16)
    return _dot(ah, bh) + _dot(ah, bl) + _dot(al, bh)


def _split3(x):
    hi = x.astype(BF16)
    r = x - hi.astype(F32)
    mid = r.astype(BF16)
    lo = (r - mid.astype(F32)).astype(BF16)
    return hi, mid, lo


def _mod_kernel(c_ref, w_ref, b_ref, o_ref):
    s = _silu(c_ref[...])
    o_ref[0] = _dot(s.astype(BF16), w_ref[0].astype(BF16)) + b_ref[0]


def _modulation(c_all, w_ada, b_ada):
    depth, d, n = w_ada.shape
    rows = c_all.shape[0]
    tn = 1536
    return pl.pallas_call(
        _mod_kernel,
        grid=(depth, n // tn),
        in_specs=[pl.BlockSpec((rows, d), lambda l, j: (0, 0)),
                  pl.BlockSpec((1, d, tn), lambda l, j: (l, 0, j)),
                  pl.BlockSpec((1, 1, tn), lambda l, j: (l, 0, j))],
        out_specs=pl.BlockSpec((1, rows, tn), lambda l, j: (l, 0, j)),
        out_shape=jax.ShapeDtypeStruct((depth, rows, n), F32),
        compiler_params=_cparams(("arbitrary", "arbitrary")),
        name="modulation",
    )(c_all, w_ada, b_ada.reshape(depth, 1, n))


def _inproj_kernel(x_ref, sc_ref, sh_ref, g0_ref, b0_ref, w_ref, o_ref, *, ln0):
    nb, tb, d = x_ref.shape
    x = x_ref[...]
    if ln0:
        x = _ln(x, g0_ref[...], b0_ref[...])
    h = x * (1.0 + sc_ref[...]) + sh_ref[...]
    o = _dot(h.reshape(nb * tb, d).astype(BF16), w_ref[...])
    o_ref[...] = o.reshape(nb, tb, o.shape[-1]).astype(o_ref.dtype)


def _inproj(x, sc, sh, g0, b0, w_fat, out_dtype, *, nb, tb, ln0):
    bsz, t, d = x.shape
    tn = INPROJ_TN
    width = w_fat.shape[1]
    return pl.pallas_call(
        functools.partial(_inproj_kernel, ln0=ln0),
        grid=(width // tn, bsz // nb, t // tb),
        in_specs=[pl.BlockSpec((nb, tb, d), lambda n, b, i: (b, i, 0)),
                  pl.BlockSpec((nb, 1, d), lambda n, b, i: (b, 0, 0)),
                  pl.BlockSpec((nb, 1, d), lambda n, b, i: (b, 0, 0)),
                  pl.BlockSpec((1, d), lambda n, b, i: (0, 0)),
                  pl.BlockSpec((1, d), lambda n, b, i: (0, 0)),
                  pl.BlockSpec((d, tn), lambda n, b, i: (0, n))],
        out_specs=pl.BlockSpec((nb, tb, tn), lambda n, b, i: (b, i, n)),
        out_shape=jax.ShapeDtypeStruct((bsz, t, width), out_dtype),
        compiler_params=_cparams(("arbitrary", "arbitrary", "arbitrary")),
        name="inproj",
    )(x, sc, sh, g0, b0, w_fat)


CONV_A_HALO = 32


def _glu(pre):
    pre = pre.astype(F32)
    return pre[..., :CONV_A_CH] * _sigmoid(pre[..., CONV_A_CH:])


def _conva_kernel(*refs, n_t, nbc, rc):
    if n_t > 1:
        pre_ref, halo_ref, hist_ref, w_ref, cb_ref, g_ref, b_ref, y_ref, ho_ref, xp_ref, xs_ref = refs
    else:
        pre_ref, hist_ref, w_ref, cb_ref, g_ref, b_ref, y_ref, ho_ref, xp_ref, xs_ref = refs
    nb, tb, _ = pre_ref.shape
    t = pl.program_id(1)
    hl = CONV_A_HALO

    @pl.when(t == 0)
    def _():
        xp_ref[:, 0:hl, :] = hist_ref[...]

    if n_t > 1:
        @pl.when(t > 0)
        def _():
            xp_ref[:, 0:hl, :] = _glu(halo_ref[...])

    xp_ref[:, hl:hl + tb, :] = _glu(pre_ref[...])
    xp = xp_ref[...]
    xs_ref[0] = xp
    for b in range(1, SUBLANES):
        xs_ref[b] = pltpu.roll(xp, b, 1)
    for b0 in range(0, nb, nbc):
        for r0 in range(0, tb, rc):
            acc = None
            for k in range(CONV_A_WIDTH):
                back = CONV_A_WIDTH - 1 - k
                whole, part = back // SUBLANES, back % SUBLANES
                start = hl - whole * SUBLANES + r0
                term = w_ref[k:k + 1, :] * xs_ref[part, b0:b0 + nbc, start:start + rc, :]
                acc = term if acc is None else acc + term
            y = _silu(_ln(acc + cb_ref[...], g_ref[...], b_ref[...]))
            y_ref[b0:b0 + nbc, r0:r0 + rc, :] = y.astype(y_ref.dtype)
    ho_ref[...] = xp_ref[:, tb:tb + hl, :]


def _conv_module(fat, hist32, w, cb, g, b, *, nb, tb):
    bsz, t, _ = fat.shape
    n_t = t // tb
    pre_blk = PREA_OFF // (2 * CONV_A_CH)
    hl = CONV_A_HALO
    in_specs = [pl.BlockSpec((nb, tb, 2 * CONV_A_CH), lambda bi, i: (bi, i, pre_blk))]
    args = [fat]
    if n_t > 1:
        per = tb // hl
        in_specs.append(pl.BlockSpec((nb, hl, 2 * CONV_A_CH),
                                     lambda bi, i: (bi, jnp.maximum(i * per - 1, 0), pre_blk)))
        args.append(fat)
    in_specs += [pl.BlockSpec((nb, hl, CONV_A_CH), lambda bi, i: (bi, 0, 0)),
                 pl.BlockSpec((CONV_A_WIDTH, CONV_A_CH), lambda bi, i: (0, 0)),
                 pl.BlockSpec((1, CONV_A_CH), lambda bi, i: (0, 0)),
                 pl.BlockSpec((1, CONV_A_CH), lambda bi, i: (0, 0)),
                 pl.BlockSpec((1, CONV_A_CH), lambda bi, i: (0, 0))]
    args += [hist32, w, cb, g, b]
    if n_t > 1:
        nbc, rc = 1, 64
    else:
        nbc, rc = 4, tb
    return pl.pallas_call(
        functools.partial(_conva_kernel, n_t=n_t, nbc=nbc, rc=rc),
        grid=(bsz // nb, n_t),
        in_specs=in_specs,
        out_specs=[pl.BlockSpec((nb, tb, CONV_A_CH), lambda bi, i: (bi, i, 0)),
                   pl.BlockSpec((nb, hl, CONV_A_CH), lambda bi, i: (bi, 0, 0))],
        out_shape=[jax.ShapeDtypeStruct((bsz, t, CONV_A_CH), BF16),
                   jax.ShapeDtypeStruct((bsz, hl, CONV_A_CH), F32)],
        scratch_shapes=[pltpu.VMEM((nb, hl + tb, CONV_A_CH), F32),
                        pltpu.VMEM((SUBLANES, nb, hl + tb, CONV_A_CH), F32)],
        compiler_params=_cparams(("arbitrary", "arbitrary")),
        name="conv_module",
    )(*args)


GDN_ROWS = 512
GDN_SUB = 128


def _gdn_kernel(*refs, n_t, chunk):
    if n_t > 1:
        (qkv_ref, halo_ref, hist_ref, z_ref, small_ref, st_ref, cw_ref, alog_ref, dt_ref, ng_ref,
         y_ref, ho_ref, so_ref, xp_ref, s_ref, vn_ref, os_ref) = refs
    else:
        (qkv_ref, hist_ref, z_ref, small_ref, st_ref, cw_ref, alog_ref, dt_ref, ng_ref,
         y_ref, ho_ref, so_ref, xp_ref, s_ref, vn_ref, os_ref) = refs
    nb, tb, _ = qkv_ref.shape
    rows = nb * tb
    t = pl.program_id(1)
    hl = SUBLANES
    nk = GDN_HEADS * GDN_DK

    @pl.when(t == 0)
    def _():
        xp_ref[:, 0:hl, :] = hist_ref[...]
        s_ref[...] = st_ref[...]

    if n_t > 1:
        @pl.when(t > 0)
        def _():
            xp_ref[:, 0:hl, :] = halo_ref[...]

    xp_ref[:, hl:hl + tb, :] = qkv_ref[...]
    xp = xp_ref[...]
    acc = cw_ref[0:1, :] * xp
    for k in range(1, GDN_CONV_WIDTH):
        acc = cw_ref[k:k + 1, :] * xp + pltpu.roll(acc, 1, 1)
    c = _silu(acc[:, hl:hl + tb, :]).reshape(rows, GDN_QKV_WIDTH)
    ho_ref[...] = xp_ref[:, tb:tb + hl, :]

    sm = small_ref[...].reshape(rows, LANES)
    beta_all = _sigmoid(sm)
    xg = sm + dt_ref[...]
    softplus = jnp.maximum(xg, 0.0) + jnp.log1p(jnp.exp(-jnp.abs(xg)))
    g_all = -jnp.exp(alog_ref[...]) * softplus

    rs = GDN_SUB
    n_sub = rows // rs
    shift = int(math.log2(chunk))
    ri = lax.broadcasted_iota(jnp.int32, (rs, rs), 0)
    ci = lax.broadcasted_iota(jnp.int32, (rs, rs), 1)
    same = (ri >> shift) == (ci >> shift)
    tri = same & (ci <= ri)
    strict = same & (ci < ri)
    tri_b = jnp.where(tri, 1.0, 0.0).astype(BF16)
    triu_b = jnp.where(same & (ri <= ci), 1.0, 0.0).astype(BF16)
    same_b = jnp.where(same, 1.0, 0.0).astype(BF16)
    gc_col, gc_row, gl_col = [], [], []
    for s in range(n_sub):
        g3 = _split3(g_all[s * rs:(s + 1) * rs])
        gc_col.append(_dot(tri_b, g3[0]) + _dot(tri_b, g3[1]) + _dot(tri_b, g3[2]))
        gc_row.append(_dot_tn(g3[0], triu_b) + _dot_tn(g3[1], triu_b) + _dot_tn(g3[2], triu_b))
        gl_col.append(_dot(same_b, g3[0]) + _dot(same_b, g3[1]) + _dot(same_b, g3[2]))

    chains = [(s, h) for s in range(n_sub) for h in range(GDN_HEADS)]
    each = lambda f, *ls: [f(*xs) for xs in zip(*ls)]
    qs, ks, kbs, rhss, decays, betas, egs, gcs, gls = [], [], [], [], [], [], [], [], []
    for s, h in chains:
        r = slice(s * rs, (s + 1) * rs)
        qh = c[r, h * GDN_DK:(h + 1) * GDN_DK]
        kh = c[r, nk + h * GDN_DK:nk + (h + 1) * GDN_DK]
        vh = c[r, 2 * nk + h * GDN_DV:2 * nk + (h + 1) * GDN_DV]
        qh = qh * lax.rsqrt(jnp.sum(qh * qh, axis=-1, keepdims=True) + RMS_EPS) * (GDN_DK ** -0.5)
        kh = kh * lax.rsqrt(jnp.sum(kh * kh, axis=-1, keepdims=True) + RMS_EPS)
        beta = beta_all[r, BETA_LANE + h:BETA_LANE + h + 1]
        gc = gc_col[s][:, DEC_LANE + h:DEC_LANE + h + 1]
        gr = gc_row[s][DEC_LANE + h:DEC_LANE + h + 1, :]
        eg = jnp.exp(gc)
        qs.append(qh)
        ks.append(kh)
        kbs.append(kh.astype(BF16))
        betas.append(beta)
        gcs.append(gc)
        gls.append(gl_col[s][:, DEC_LANE + h:DEC_LANE + h + 1])
        egs.append(eg)
        decays.append(jnp.where(tri, jnp.exp(jnp.where(tri, gc - gr, 0.0)), 0.0))
        rhss.append(jnp.concatenate([vh * beta, kh * (beta * eg)], axis=1))
    kks = each(lambda kb: _dot_nt(kb, kb), kbs)
    a0s = each(lambda beta, kk, decay: jnp.where(strict, -(beta * kk * decay), 0.0), betas, kks, decays)
    ms = a0s
    pows = a0s
    for _ in range(1, shift):
        pows = each(lambda a: _dot(a.astype(BF16), a.astype(BF16)), pows)
        ms = each(lambda m, a: m + a + _dot(m.astype(BF16), a.astype(BF16)), ms, pows)
    mbs = each(lambda m: m.astype(BF16), ms)
    x0s = each(lambda rhs, mb: rhs + _dot(mb, rhs.astype(BF16)), rhss, mbs)
    ress = each(lambda rhs, x0, a0: rhs - x0 + _dot_split(a0, x0), rhss, x0s, a0s)
    sols = each(lambda x0, res, mb: x0 + res + _dot(mb, res.astype(BF16)), x0s, ress, mbs)
    qkms = each(lambda qh, kb, decay: jnp.where(tri, _dot_nt(qh.astype(BF16), kb) * decay, 0.0).astype(BF16),
                qs, kbs, decays)
    us = each(lambda sol: sol[:, :GDN_DV], sols)
    wbs = each(lambda sol: sol[:, GDN_DV:].astype(BF16), sols)
    qds = each(lambda qh, eg: (qh * eg).astype(BF16), qs, egs)
    kds = each(lambda kh, gl, gc: (kh * jnp.exp(gl - gc)).astype(BF16), ks, gls, gcs)
    egls = each(jnp.exp, gls)
    for j in range(rows // chunk):
        bi = (j * chunk) // tb
        s = (j * chunk) // rs
        r0 = j * chunk - s * rs
        for h in range(GDN_HEADS):
            i = s * GDN_HEADS + h
            sc = s_ref[bi, h]
            sb = sc.astype(BF16)
            vnew = us[i][r0:r0 + chunk] - _dot(wbs[i][r0:r0 + chunk], sb)
            os_ref[h, j * chunk:(j + 1) * chunk, :] = _dot(qds[i][r0:r0 + chunk], sb)
            vn_ref[h, j * chunk:(j + 1) * chunk, :] = vnew
            s_ref[bi, h] = sc * egls[i][r0:r0 + 1, :] + _dot_tn(kds[i][r0:r0 + chunk], vnew.astype(BF16))
    z = z_ref[...].astype(F32).reshape(rows, GDN_HEADS * GDN_DV)
    for i, (s, h) in enumerate(chains):
        r = slice(s * rs, (s + 1) * rs)
        o = os_ref[h, r, :] + _dot(qkms[i], vn_ref[h, r, :].astype(BF16))
        on = _rms(o, ng_ref[...])
        y = (on * _silu(z[r, h * GDN_DV:(h + 1) * GDN_DV])).astype(y_ref.dtype)
        if tb >= rs:
            y_ref[0, s * rs:(s + 1) * rs, h * GDN_DV:(h + 1) * GDN_DV] = y
        else:
            y_ref[s * (rs // tb):(s + 1) * (rs // tb), :, h * GDN_DV:(h + 1) * GDN_DV] = y.reshape(rs // tb, tb, GDN_DV)
    so_ref[...] = s_ref[...]


def _gated_deltanet(fat, fat16, hist8, state, cw, alog_row, dt_row, ng, *, nb, tb):
    bsz, t, _ = fat.shape
    n_t = t // tb
    chunk = min(CHUNK, t)
    hl = SUBLANES
    qkv_blk = QKV_OFF // GDN_QKV_WIDTH
    z_blk = Z_OFF // (GDN_HEADS * GDN_DV)
    small_blk = SMALL_OFF // LANES
    in_specs = [pl.BlockSpec((nb, tb, GDN_QKV_WIDTH), lambda bi, i: (bi, i, qkv_blk))]
    args = [fat]
    if n_t > 1:
        per = tb // hl
        in_specs.append(pl.BlockSpec((nb, hl, GDN_QKV_WIDTH), lambda bi, i: (bi, jnp.maximum(i * per - 1, 0), qkv_blk)))
        args.append(fat)
    in_specs += [pl.BlockSpec((nb, hl, GDN_QKV_WIDTH), lambda bi, i: (bi, 0, 0)),
                 pl.BlockSpec((nb, tb, GDN_HEADS * GDN_DV), lambda bi, i: (bi, i, z_blk)),
                 pl.BlockSpec((nb, tb, LANES), lambda bi, i: (bi, i, small_blk)),
                 pl.BlockSpec((nb, GDN_HEADS, GDN_DK, GDN_DV), lambda bi, i: (bi, 0, 0, 0)),
                 pl.BlockSpec((GDN_CONV_WIDTH, GDN_QKV_WIDTH), lambda bi, i: (0, 0)),
                 pl.BlockSpec((1, LANES), lambda bi, i: (0, 0)),
                 pl.BlockSpec((1, LANES), lambda bi, i: (0, 0)),
                 pl.BlockSpec((1, GDN_DV), lambda bi, i: (0, 0))]
    args += [hist8, fat16, fat, state, cw, alog_row, dt_row, ng]
    rows = nb * tb
    return pl.pallas_call(
        functools.partial(_gdn_kernel, n_t=n_t, chunk=chunk),
        grid=(bsz // nb, n_t),
        in_specs=in_specs,
        out_specs=[pl.BlockSpec((nb, tb, GDN_HEADS * GDN_DV), lambda bi, i: (bi, i, 0)),
                   pl.BlockSpec((nb, hl, GDN_QKV_WIDTH), lambda bi, i: (bi, 0, 0)),
                   pl.BlockSpec((nb, GDN_HEADS, GDN_DK, GDN_DV), lambda bi, i: (bi, 0, 0, 0))],
        out_shape=[jax.ShapeDtypeStruct((bsz, t, GDN_HEADS * GDN_DV), BF16),
                   jax.ShapeDtypeStruct((bsz, hl, GDN_QKV_WIDTH), F32),
                   jax.ShapeDtypeStruct((bsz, GDN_HEADS, GDN_DK, GDN_DV), F32)],
        scratch_shapes=[pltpu.VMEM((nb, hl + tb, GDN_QKV_WIDTH), F32),
                        pltpu.VMEM((nb, GDN_HEADS, GDN_DK, GDN_DV), F32),
                        pltpu.VMEM((GDN_HEADS, rows, GDN_DV), F32),
                        pltpu.VMEM((GDN_HEADS, rows, GDN_DV), F32)],
        compiler_params=_cparams(("arbitrary", "arbitrary")),
        name="gated_deltanet",
    )(*args)


def _mla_prep_kernel(*refs, emit_kv, scale):
    if emit_kv:
        (blk_ref, cs_ref, sn_ref, qg_ref, kg_ref, wq_ref, wqs_ref, wk_ref, wvt_ref,
         q_ref, lat_ref, kpe_ref, k_ref, vt_ref) = refs
    else:
        (blk_ref, cs_ref, sn_ref, qg_ref, kg_ref, wq_ref, wqs_ref,
         q_ref, lat_ref, kpe_ref) = refs
    nb, tb, _ = blk_ref.shape
    rows = nb * tb
    blk = blk_ref[...]
    cs = cs_ref[...]
    sn = sn_ref[...]
    qn = _rms(blk[..., :MLA_Q_LORA], qg_ref[...]).reshape(rows, MLA_Q_LORA).astype(BF16)
    qf = _dot(qn, wq_ref[...]).reshape(nb, tb, MLA_HEADS * QK_PAD)
    qs = _dot(qn, wqs_ref[...]).reshape(nb, tb, MLA_HEADS * LANES)
    for h in range(MLA_HEADS):
        nope = qf[..., h * QK_PAD:h * QK_PAD + MLA_NOPE_DIM]
        rope = qf[..., h * QK_PAD + MLA_NOPE_DIM:(h + 1) * QK_PAD] * cs + qs[..., h * LANES:(h + 1) * LANES] * sn
        q_ref[h, :, :, 0:MLA_NOPE_DIM] = (nope * scale).astype(q_ref.dtype)
        q_ref[h, :, :, MLA_NOPE_DIM:QK_PAD] = (rope * scale).astype(q_ref.dtype)

    lat = _rms(blk[..., MLA_Q_LORA + LANES:], kg_ref[...])
    lat_ref[...] = lat

    small = blk[..., MLA_Q_LORA:MLA_Q_LORA + LANES].reshape(rows, LANES)
    half = MLA_ROPE_DIM // 2
    lane = lax.broadcasted_iota(jnp.int32, (rows, LANES), 1)
    swapped = jnp.where(lane < half, pltpu.roll(small, LANES - half, 1), pltpu.roll(small, half, 1))
    kpe = small.reshape(nb, tb, LANES) * cs + swapped.reshape(nb, tb, LANES) * sn
    kpe_ref[...] = kpe[..., :MLA_ROPE_DIM]

    if emit_kv:
        latb = lat.reshape(rows, MLA_KV_LORA).astype(BF16)
        kn = _dot(latb, wk_ref[...]).reshape(nb, tb, MLA_HEADS * MLA_NOPE_DIM)
        ones_rows = jnp.where(lax.broadcasted_iota(jnp.int32, (VT_ROWS - MLA_V_DIM, ATTN_KB), 0) == 0,
                              1.0, 0.0).astype(vt_ref.dtype)
        for h in range(MLA_HEADS):
            k_ref[h, :, :, 0:MLA_NOPE_DIM] = kn[..., h * MLA_NOPE_DIM:(h + 1) * MLA_NOPE_DIM].astype(k_ref.dtype)
            k_ref[h, :, :, MLA_NOPE_DIM:QK_PAD] = kpe.astype(k_ref.dtype)
            vt = _dot_nt(wvt_ref[h], latb).astype(vt_ref.dtype)
            for kb in range(rows // ATTN_KB):
                vt_ref[h, 0, kb, 0:MLA_V_DIM, :] = vt[:, kb * ATTN_KB:(kb + 1) * ATTN_KB]
                vt_ref[h, 0, kb, MLA_V_DIM:VT_ROWS, :] = ones_rows


def _mla_prep(fat, cs, sn, qg, kg, wq, wqs, wk, wvt, *, nb, tb, emit_kv, scale):
    bsz, t, _ = fat.shape
    blk = QLAT_OFF // MLA_BLK_W
    hq = MLA_HEADS
    in_specs = [pl.BlockSpec((nb, tb, MLA_BLK_W), lambda bi, i: (bi, i, blk)),
                pl.BlockSpec((tb, LANES), lambda bi, i: (i, 0)),
                pl.BlockSpec((tb, LANES), lambda bi, i: (i, 0)),
                pl.BlockSpec((1, MLA_Q_LORA), lambda bi, i: (0, 0)),
                pl.BlockSpec((1, MLA_KV_LORA), lambda bi, i: (0, 0)),
                pl.BlockSpec(wq.shape, lambda bi, i: (0, 0)),
                pl.BlockSpec(wqs.shape, lambda bi, i: (0, 0))]
    args = [fat, cs, sn, qg, kg, wq, wqs]
    out_specs = [pl.BlockSpec((hq, nb, tb, QK_PAD), lambda bi, i: (0, bi, i, 0)),
                 pl.BlockSpec((nb, tb, MLA_KV_LORA), lambda bi, i: (bi, i, 0)),
                 pl.BlockSpec((nb, tb, MLA_ROPE_DIM), lambda bi, i: (bi, i, 0))]
    out_shape = [jax.ShapeDtypeStruct((hq, bsz, t, QK_PAD), BF16),
                 jax.ShapeDtypeStruct((bsz, t, MLA_KV_LORA), F32),
                 jax.ShapeDtypeStruct((bsz, t, MLA_ROPE_DIM), F32)]
    if emit_kv:
        assert nb == 1 and tb % ATTN_KB == 0
        in_specs += [pl.BlockSpec(wk.shape, lambda bi, i: (0, 0)),
                     pl.BlockSpec(wvt.shape, lambda bi, i: (0, 0, 0))]
        args += [wk, wvt]
        per = tb // ATTN_KB
        out_specs += [pl.BlockSpec((hq, nb, tb, QK_PAD), lambda bi, i: (0, bi, i, 0)),
                      pl.BlockSpec((hq, nb, per, VT_ROWS, ATTN_KB), lambda bi, i: (0, bi, i, 0, 0))]
        out_shape += [jax.ShapeDtypeStruct((hq, bsz, t, QK_PAD), BF16),
                      jax.ShapeDtypeStruct((hq, bsz, t // ATTN_KB, VT_ROWS, ATTN_KB), BF16)]
    return pl.pallas_call(
        functools.partial(_mla_prep_kernel, emit_kv=emit_kv, scale=scale),
        grid=(bsz // nb, t // tb),
        in_specs=in_specs, out_specs=out_specs, out_shape=out_shape,
        compiler_params=_cparams(("arbitrary", "arbitrary")),
        name="mla_prep",
    )(*args)


def _attn_prompt_kernel(q_ref, k_ref, vt_ref, o_ref, m_ref, acc_ref, st_ref, *, nq):
    kb = ATTN_KB
    j = pl.program_id(2)
    m_ref[...] = jnp.full(m_ref.shape, -jnp.inf, F32)
    acc_ref[...] = jnp.zeros(acc_ref.shape, F32)

    def scores(chains, ki):
        k0 = pl.multiple_of(ki * kb, kb)
        kblk = k_ref[0, 0, pl.ds(k0, kb), :]
        return [_dot_nt(kblk, q_ref[0, 0, c * kb:(c + 1) * kb, :]) for c in chains]

    def finish(chains, sts, ki, masked_chain):
        pts, alphas = [], []
        for c, st in zip(chains, sts):
            if c == masked_chain:
                key_chunk = lax.broadcasted_iota(jnp.int32, (kb, kb), 0) // CHUNK
                qry_chunk = lax.broadcasted_iota(jnp.int32, (kb, kb), 1) // CHUNK
                st = jnp.where(key_chunk <= qry_chunk, st, -jnp.inf)
            m_old = m_ref[c]
            m_new = jnp.maximum(m_old, jnp.max(st, axis=0, keepdims=True))
            pts.append(jnp.exp2(st - m_new).astype(BF16))
            alphas.append(jnp.exp2(m_old - m_new))
            m_ref[c] = m_new
        vblk = vt_ref[0, 0, ki]
        for c, pt, alpha in zip(chains, pts, alphas):
            acc_ref[c] = alpha * acc_ref[c] + _dot(vblk, pt)

    everyone = range(nq)
    for c, st in zip(everyone, scores(everyone, 0)):
        st_ref[c] = st

    def step(ki):
        cur = [st_ref[c] for c in everyone]
        nxt = scores(everyone, ki + 1)
        finish(everyone, cur, ki, None)
        for c in everyone:
            st_ref[c] = nxt[c]

    per_trip = nq

    def body(kp, carry):
        for u in range(per_trip):
            step(per_trip * kp + u)
        return carry

    lax.fori_loop(0, (nq // per_trip) * j, body, 0)
    cur = [st_ref[c] for c in everyone]
    for d in range(nq):
        nxt = scores(range(d + 1, nq), nq * j + d + 1) if d + 1 < nq else []
        finish(range(d, nq), cur, nq * j + d, d)
        cur = nxt
    for c in range(nq):
        acc = acc_ref[c]
        ot = acc[0:MLA_V_DIM] / acc[MLA_V_DIM:MLA_V_DIM + 1]
        o_ref[0, c * kb:(c + 1) * kb, :] = ot.T.astype(o_ref.dtype)


def _attn_prompt(q, k, vt, *, nq):
    hq, bsz, t, _ = q.shape
    tq = ATTN_KB * nq
    return pl.pallas_call(
        functools.partial(_attn_prompt_kernel, nq=nq),
        grid=(bsz, hq, t // tq),
        in_specs=[pl.BlockSpec((1, 1, tq, QK_PAD), lambda b, h, i: (h, b, i, 0)),
                  pl.BlockSpec((1, 1, t, QK_PAD), lambda b, h, i: (h, b, 0, 0)),
                  pl.BlockSpec((1, 1, t // ATTN_KB, VT_ROWS, ATTN_KB), lambda b, h, i: (h, b, 0, 0, 0))],
        out_specs=pl.BlockSpec((1, tq, MLA_V_DIM), lambda b, h, i: (b, i, h)),
        out_shape=jax.ShapeDtypeStruct((bsz, t, MLA_HEADS * MLA_V_DIM), BF16),
        scratch_shapes=[pltpu.VMEM((nq, 1, ATTN_KB), F32), pltpu.VMEM((nq, VT_ROWS, ATTN_KB), F32),
                        pltpu.VMEM((nq, ATTN_KB, ATTN_KB), F32)],
        compiler_params=_cparams(("arbitrary", "arbitrary", "arbitrary")),
        name="attn_prompt",
    )(q, k, vt)


def _attn_sample_kernel(q_ref, latc_ref, kpec_ref, latn_ref, kpen_ref, wuk_ref, wuv_ref, o_ref):
    tn = latn_ref.shape[1]
    past = latc_ref.shape[2]
    hq = MLA_HEADS
    qabs = jnp.concatenate(
        [_dot(q_ref[h, 0, :, 0:MLA_NOPE_DIM], wuk_ref[h]) for h in range(hq)], axis=0).astype(BF16)
    qrope = jnp.concatenate([q_ref[h, 0, :, MLA_NOPE_DIM:MLA_NOPE_DIM + MLA_ROPE_DIM] for h in range(hq)], axis=0)
    latc = latc_ref[0, 0].astype(BF16)
    latn = latn_ref[0].astype(BF16)
    sc = _dot_nt(qabs, latc) + _dot_nt(qrope, kpec_ref[0, 0].astype(BF16))
    sn = _dot_nt(qabs, latn) + _dot_nt(qrope, kpen_ref[0].astype(BF16))
    rows = hq * tn
    qpos = past + lax.broadcasted_iota(jnp.int32, (rows, 1), 0) % tn
    kc_pos = lax.broadcasted_iota(jnp.int32, (rows, past), 1)
    kn_pos = past + lax.broadcasted_iota(jnp.int32, (rows, tn), 1)
    sc = jnp.where(kc_pos // CHUNK <= qpos // CHUNK, sc, -jnp.inf)
    sn = jnp.where(kn_pos // CHUNK <= qpos // CHUNK, sn, -jnp.inf)
    m = jnp.maximum(jnp.max(sc, axis=-1, keepdims=True), jnp.max(sn, axis=-1, keepdims=True))
    pc = jnp.exp(sc - m)
    pn = jnp.exp(sn - m)
    l = jnp.sum(pc, axis=-1, keepdims=True) + jnp.sum(pn, axis=-1, keepdims=True)
    olat = (_dot(pc.astype(BF16), latc) + _dot(pn.astype(BF16), latn)) / l
    for h in range(hq):
        oh = _dot(olat[h * tn:(h + 1) * tn].astype(BF16), wuv_ref[h])
        o_ref[0, :, h * MLA_V_DIM:(h + 1) * MLA_V_DIM] = oh.astype(o_ref.dtype)


def _attn_sample(q, caches_lat, caches_kpe, layer, lat_new, kpe_new, wuk_t, wuv):
    hq, bsz, tn, _ = q.shape
    past = caches_lat.shape[2]
    return pl.pallas_call(
        _attn_sample_kernel,
        grid=(bsz,),
        in_specs=[pl.BlockSpec((hq, 1, tn, QK_PAD), lambda b: (0, b, 0, 0)),
                  pl.BlockSpec((1, 1, past, MLA_KV_LORA), lambda b: (layer, b, 0, 0)),
                  pl.BlockSpec((1, 1, past, MLA_ROPE_DIM), lambda b: (layer, b, 0, 0)),
                  pl.BlockSpec((1, tn, MLA_KV_LORA), lambda b: (b, 0, 0)),
                  pl.BlockSpec((1, tn, MLA_ROPE_DIM), lambda b: (b, 0, 0)),
                  pl.BlockSpec(wuk_t.shape, lambda b: (0, 0, 0)),
                  pl.BlockSpec(wuv.shape, lambda b: (0, 0, 0))],
        out_specs=pl.BlockSpec((1, tn, MLA_HEADS * MLA_V_DIM), lambda b: (b, 0, 0)),
        out_shape=jax.ShapeDtypeStruct((bsz, tn, MLA_HEADS * MLA_V_DIM), BF16),
        compiler_params=_cparams(("arbitrary",)),
        name="attn_sample",
    )(q, caches_lat, caches_kpe, lat_new, kpe_new, wuk_t, wuv)


def _merge_kernel(ya_ref, yb_ref, yc_ref, gate_ref, x_ref, g1_ref, g0_ref, b0_ref, wb_ref, wo_ref,
                  lg_ref, lb_ref, o_ref, *, ln0, alpha):
    nb, tb, d = x_ref.shape
    rows = nb * tb
    x = x_ref[...]
    if ln0:
        x = _ln(x, g0_ref[...], b0_ref[...])
    merged = None
    for i, y_ref in enumerate((ya_ref, yb_ref, yc_ref)):
        proj = _dot(y_ref[...].reshape(rows, BRANCH_WIDTH), wb_ref[i])
        gate = _sigmoid(gate_ref[:, :, i * d:(i + 1) * d].astype(F32)).reshape(rows, d)
        merged = gate * proj if merged is None else merged + gate * proj
    out = _dot(merged.astype(BF16), wo_ref[...]).reshape(nb, tb, d)
    o_ref[...] = _ln(alpha * x + (1.0 + g1_ref[...]) * out, lg_ref[...], lb_ref[...])


def _merge(ya, yb, yc, fat, x, g1, g0, b0, wb, wo, lg, lb, *, nb, tb, ln0, alpha):
    bsz, t, d = x.shape
    row = lambda bi, i: (bi, i, 0)
    const2 = lambda bi, i: (0, 0)
    return pl.pallas_call(
        functools.partial(_merge_kernel, ln0=ln0, alpha=alpha),
        grid=(bsz // nb, t // tb),
        in_specs=[pl.BlockSpec((nb, tb, BRANCH_WIDTH), row),
                  pl.BlockSpec((nb, tb, BRANCH_WIDTH), row),
                  pl.BlockSpec((nb, tb, BRANCH_WIDTH), row),
                  pl.BlockSpec((nb, tb, GATE_W), row),
                  pl.BlockSpec((nb, tb, d), row),
                  pl.BlockSpec((nb, 1, d), lambda bi, i: (bi, 0, 0)),
                  pl.BlockSpec((1, d), const2),
                  pl.BlockSpec((1, d), const2),
                  pl.BlockSpec(wb.shape, lambda bi, i: (0, 0, 0)),
                  pl.BlockSpec(wo.shape, const2),
                  pl.BlockSpec((1, d), const2),
                  pl.BlockSpec((1, d), const2)],
        out_specs=pl.BlockSpec((nb, tb, d), row),
        out_shape=jax.ShapeDtypeStruct((bsz, t, d), F32),
        compiler_params=_cparams(("arbitrary", "arbitrary")),
        name="merge",
    )(ya, yb, yc, fat, x, g1, g0, b0, wb, wo, lg, lb)


FFN_BLOCK = 1408


def _ffn_kernel(x_ref, sc_ref, sh_ref, g2_ref, wa_ref, wv_ref, wd_ref, cw_ref, cb_ref, lg_ref, lb_ref, hist_ref,
                o_ref, ho_ref, h_ref, acc_ref, carry_ref, ap_ref, *, alpha):
    nb, tb, d = x_ref.shape
    rows = nb * tb
    t = pl.program_id(1)
    f = pl.program_id(2)
    n_f = pl.num_programs(2)
    hl = SUBLANES
    fb = wa_ref.shape[1]

    @pl.when(f == 0)
    def _():
        h = x_ref[...] * (1.0 + sc_ref[...]) + sh_ref[...]
        h_ref[...] = h.reshape(rows, d).astype(BF16)

    @pl.when(t == 0)
    def _():
        carry_ref[f] = hist_ref[...]

    h = h_ref[...]
    a = _dot(h, wa_ref[...]).reshape(nb, tb, fb)
    v = _dot(h, wv_ref[...]).reshape(nb, tb, fb)
    ap_ref[:, 0:hl, :] = carry_ref[f]
    ap_ref[:, hl:hl + tb, :] = a
    ap = ap_ref[...]
    conv = cw_ref[0:1, :] * ap
    for k in range(1, FFN_CONV_WIDTH):
        conv = cw_ref[k:k + 1, :] * ap + pltpu.roll(conv, 1, 1)
    act = _silu(conv[:, hl:hl + tb, :] + cb_ref[...]) * v
    contrib = _dot(act.reshape(rows, fb).astype(BF16), wd_ref[...])
    last = ap_ref[:, tb:tb + hl, :]
    carry_ref[f] = last
    ho_ref[0] = last

    @pl.when(f == 0)
    def _():
        acc_ref[...] = contrib

    @pl.when(f > 0)
    def _():
        acc_ref[...] = acc_ref[...] + contrib

    @pl.when(f == n_f - 1)
    def _():
        y = acc_ref[...].reshape(nb, tb, d)
        o_ref[...] = _ln(alpha * x_ref[...] + (1.0 + g2_ref[...]) * y, lg_ref[...], lb_ref[...])


def _ffn(x, sc, sh, g2, wa, wv, wd, cw, cb, lg, lb, hist8, *, nb, tb, alpha):
    bsz, t, d = x.shape
    dff = wa.shape[1]
    fb = FFN_BLOCK
    n_f = dff // fb
    hl = SUBLANES
    rows = nb * tb
    row = lambda bi, i, f: (bi, i, 0)
    mod = lambda bi, i, f: (bi, 0, 0)
    const2 = lambda bi, i, f: (0, 0)
    return pl.pallas_call(
        functools.partial(_ffn_kernel, alpha=alpha),
        grid=(bsz // nb, t // tb, n_f),
        in_specs=[pl.BlockSpec((nb, tb, d), row),
                  pl.BlockSpec((nb, 1, d), mod),
                  pl.BlockSpec((nb, 1, d), mod),
                  pl.BlockSpec((nb, 1, d), mod),
                  pl.BlockSpec((d, fb), lambda bi, i, f: (0, f)),
                  pl.BlockSpec((d, fb), lambda bi, i, f: (0, f)),
                  pl.BlockSpec((fb, d), lambda bi, i, f: (f, 0)),
                  pl.BlockSpec((FFN_CONV_WIDTH, fb), lambda bi, i, f: (0, f)),
                  pl.BlockSpec((1, fb), lambda bi, i, f: (0, f)),
                  pl.BlockSpec((1, d), const2),
                  pl.BlockSpec((1, d), const2),
                  pl.BlockSpec((nb, hl, fb), lambda bi, i, f: (bi, 0, f))],
        out_specs=[pl.BlockSpec((nb, tb, d), row),
                   pl.BlockSpec((1, nb, hl, fb), lambda bi, i, f: (i, bi, 0, f))],
        out_shape=[jax.ShapeDtypeStruct((bsz, t, d), F32),
                   jax.ShapeDtypeStruct((t // tb, bsz, hl, dff), F32)],
        scratch_shapes=[pltpu.VMEM((rows, d), BF16),
                        pltpu.VMEM((rows, d), F32),
                        pltpu.VMEM((n_f, nb, hl, fb), F32),
                        pltpu.VMEM((nb, hl + tb, fb), F32)],
        compiler_params=_cparams(("arbitrary", "arbitrary", "arbitrary")),
        name="conv_ffn",
    )(x, sc, sh, g2, wa, wv, wd, cw, cb, lg, lb, hist8)


def _front_pad(hist, rows):
    b, r, c = hist.shape
    return jnp.concatenate([jnp.zeros((b, rows - r, c), hist.dtype), hist], axis=1)


def _rope_rows(pos):
    half = MLA_ROPE_DIM // 2
    inv_freq = ROPE_THETA ** (-jnp.arange(half, dtype=F32) / half)
    ang = pos.astype(F32)[:, None] * inv_freq[None, :]
    cos, sin = jnp.cos(ang), jnp.sin(ang)
    pad = jnp.zeros((pos.shape[0], LANES - MLA_ROPE_DIM), F32)
    return jnp.concatenate([cos, cos, pad], axis=1), jnp.concatenate([-sin, sin, pad], axis=1)


def _lane_row(vals, lane0):
    return jnp.zeros((1, LANES), F32).at[0, lane0:lane0 + vals.shape[0]].set(vals.astype(F32))


W_IN_OFFS = np.cumsum([0, 2 * CONV_A_CH, GDN_QKV_WIDTH, GDN_HEADS * GDN_DV, GDN_HEADS, GDN_HEADS,
                       MLA_Q_LORA, MLA_KV_LORA, MLA_ROPE_DIM, N_BRANCH * D_MODEL]).tolist()
W_PREP_ROWS = 128


def _w_in_prep_kernel(w_ref, o32_ref, o16_ref):
    w = w_ref[0]
    seg = lambda i: w[:, W_IN_OFFS[i]:W_IN_OFFS[i + 1]].astype(BF16)
    pad = jnp.zeros((w.shape[0], LANES - MLA_ROPE_DIM - 2 * GDN_HEADS), BF16)
    o32_ref[...] = jnp.concatenate([seg(1), seg(5), seg(7), seg(3), seg(4), pad, seg(6)], axis=1)
    o16_ref[...] = jnp.concatenate([seg(8), seg(0), seg(2)], axis=1)


def _w_in_prep(w_in, l):
    _, d, n = w_in.shape
    return pl.pallas_call(
        _w_in_prep_kernel,
        grid=(d // W_PREP_ROWS,),
        in_specs=[pl.BlockSpec((1, W_PREP_ROWS, n), lambda i: (l, i, 0))],
        out_specs=[pl.BlockSpec((W_PREP_ROWS, FAT32_W), lambda i: (i, 0)),
                   pl.BlockSpec((W_PREP_ROWS, FAT16_W), lambda i: (i, 0))],
        out_shape=[jax.ShapeDtypeStruct((d, FAT32_W), BF16), jax.ShapeDtypeStruct((d, FAT16_W), BF16)],
        compiler_params=_cparams(("arbitrary",)),
        name="w_in_prep",
    )(w_in)


def _layer_weights(l, w_in, mla_w_uq, mla_w_ukv, w_branch, w_out, w_up, w_down):
    w_fat32, w_fat16 = _w_in_prep(w_in, l)
    hd = MLA_NOPE_DIM + MLA_ROPE_DIM
    half = MLA_ROPE_DIM // 2
    wq = mla_w_uq[l]
    zq = jnp.zeros((MLA_Q_LORA, QK_PAD - hd), wq.dtype)
    zs = jnp.zeros((MLA_Q_LORA, LANES - MLA_ROPE_DIM), wq.dtype)
    wq_cols, wqs_cols = [], []
    for h in range(MLA_HEADS):
        wq_cols += [wq[:, h * hd:(h + 1) * hd], zq]
        r0 = h * hd + MLA_NOPE_DIM
        wqs_cols += [wq[:, r0 + half:r0 + 2 * half], wq[:, r0:r0 + half], zs]
    wkv = mla_w_ukv[l]
    kvw = MLA_NOPE_DIM + MLA_V_DIM
    wuk_t = jnp.stack([wkv[:, h * kvw:h * kvw + MLA_NOPE_DIM].T for h in range(MLA_HEADS)]).astype(BF16)
    wuv = jnp.stack([wkv[:, h * kvw + MLA_NOPE_DIM:(h + 1) * kvw] for h in range(MLA_HEADS)]).astype(BF16)
    return dict(
        w_fat32=w_fat32, w_fat16=w_fat16,
        wq=jnp.concatenate(wq_cols, axis=1).astype(BF16),
        wqs=jnp.concatenate(wqs_cols, axis=1).astype(BF16),
        wk=jnp.concatenate([wkv[:, h * kvw:h * kvw + MLA_NOPE_DIM] for h in range(MLA_HEADS)], axis=1).astype(BF16),
        wvt=jnp.transpose(wuv, (0, 2, 1)), wuk_t=wuk_t, wuv=wuv,
        wb=w_branch[l].astype(BF16), wo=w_out[l].astype(BF16),
        wa=w_up[l][:, :D_FF].astype(BF16), wv=w_up[l][:, D_FF:].astype(BF16), wd=w_down[l].astype(BF16),
    )


def _run_layer(x, mods, st, wts, p, *, ln0, g0, b0, alpha, cs, sn, tiles, prompt):
    bsz, t, d = x.shape
    sh1, sc1, g1, sh2, sc2, g2 = mods
    nb, tb, gdn_nb, gdn_tb = tiles
    if st is None:
        hist_a = jnp.zeros((bsz, CONV_A_WIDTH - 1, CONV_A_CH), F32)
        hist_b = jnp.zeros((bsz, GDN_CONV_WIDTH - 1, GDN_QKV_WIDTH), F32)
        state_b = jnp.zeros((bsz, GDN_HEADS, GDN_DK, GDN_DV), F32)
        hist_f = jnp.zeros((bsz, FFN_CONV_WIDTH - 1, D_FF), F32)
    else:
        caches_lat, caches_kpe, layer, hist_a, hist_b, state_b, hist_f = st
    row = lambda v: v.reshape(1, -1)

    fat = _inproj(x, sc1, sh1, g0, b0, wts['w_fat32'], F32, nb=nb, tb=tb, ln0=ln0)
    fat16 = _inproj(x, sc1, sh1, g0, b0, wts['w_fat16'], BF16, nb=nb, tb=tb, ln0=ln0)
    y_a, ha = _conv_module(fat16, _front_pad(hist_a, CONV_A_HALO), p['conv_a_w'], row(p['conv_a_b']),
                           row(p['ln_a_g']), row(p['ln_a_b']), nb=nb, tb=tb)
    y_b, hb, sb = _gated_deltanet(fat, fat16, _front_pad(hist_b, SUBLANES), state_b, p['gdn_conv_w'],
                                  _lane_row(p['gdn_a_log'], DEC_LANE), _lane_row(p['gdn_dt_bias'], DEC_LANE),
                                  row(p['gdn_norm_g']), nb=gdn_nb, tb=gdn_tb)
    if prompt:
        q, lat, kpe, k, v = _mla_prep(fat, cs, sn, row(p['mla_q_norm_g']), row(p['mla_kv_norm_g']),
                                      wts['wq'], wts['wqs'], wts['wk'], wts['wvt'], nb=nb, tb=tb, emit_kv=True,
                                      scale=ATTN_SCALE * math.log2(math.e))
        y_c = _attn_prompt(q, k, v, nq=math.gcd(ATTN_CHAINS, t // ATTN_KB))
    else:
        q, lat, kpe = _mla_prep(fat, cs, sn, row(p['mla_q_norm_g']), row(p['mla_kv_norm_g']),
                                wts['wq'], wts['wqs'], None, None, nb=nb, tb=tb, emit_kv=False, scale=ATTN_SCALE)
        y_c = _attn_sample(q, caches_lat, caches_kpe, layer, lat, kpe, wts['wuk_t'], wts['wuv'])
    x1 = _merge(y_a, y_b, y_c, fat16, x, g1, g0, b0, wts['wb'], wts['wo'], row(p['ln1_g']), row(p['ln1_b']),
                nb=nb, tb=tb, ln0=ln0, alpha=alpha)
    x2, hf = _ffn(x1, sc2, sh2, g2, wts['wa'], wts['wv'], wts['wd'], p['ffn_conv_w'], row(p['ffn_conv_b']),
                  row(p['ln2_g']), row(p['ln2_b']), _front_pad(hist_f, SUBLANES), nb=nb, tb=tb, alpha=alpha)
    states = (lat, kpe, ha[:, CONV_A_HALO - (CONV_A_WIDTH - 1):], hb[:, SUBLANES - (GDN_CONV_WIDTH - 1):],
              sb, hf[-1, :, SUBLANES - (FFN_CONV_WIDTH - 1):])
    return x2, states


def kernel(x_prompt, x_sample, cache_mla_latent, cache_mla_kpe, state_conv_a, state_gdn_conv, state_gdn, state_ffn_conv, c_prompt, c_sample, ln0_g, ln0_b, w_ada, b_ada, w_in, conv_a_w, conv_a_b, ln_a_g, ln_a_b, gdn_conv_w, gdn_a_log, gdn_dt_bias, gdn_norm_g, mla_q_norm_g, mla_kv_norm_g, mla_w_uq, mla_w_ukv, w_branch, w_out, ln1_g, ln1_b, w_up, ffn_conv_w, ffn_conv_b, w_down, ln2_g, ln2_b):
    depth = w_ada.shape[0]
    bp, seq, d = x_prompt.shape
    bs, dec_seq, _ = x_sample.shape
    past = cache_mla_latent.shape[2]
    alpha = (2 * depth) ** 0.25

    n_c = bp + bs
    c_rows = -(-n_c // SUBLANES) * SUBLANES
    c_all = jnp.concatenate([c_prompt, c_sample, jnp.zeros((c_rows - n_c, d), F32)], axis=0)
    mod = _modulation(c_all, w_ada, b_ada)

    cs_p, sn_p = _rope_rows(jnp.arange(seq, dtype=jnp.int32))
    cs_s, sn_s = _rope_rows(past + jnp.arange(dec_seq, dtype=jnp.int32))
    g0, b0 = ln0_g.reshape(1, d), ln0_b.reshape(1, d)

    tile_p = min(512, seq)
    tiles_p = (1, tile_p, 1, min(GDN_ROWS, seq))
    tiles_s = (bs, dec_seq, min(bs, GDN_ROWS // dec_seq), dec_seq)

    xp, xs = x_prompt, x_sample
    p_states, s_states = [], []
    for l in range(depth):
        p = {'conv_a_w': conv_a_w[l], 'conv_a_b': conv_a_b[l], 'ln_a_g': ln_a_g[l], 'ln_a_b': ln_a_b[l],
             'gdn_conv_w': gdn_conv_w[l], 'gdn_a_log': gdn_a_log[l], 'gdn_dt_bias': gdn_dt_bias[l],
             'gdn_norm_g': gdn_norm_g[l], 'mla_q_norm_g': mla_q_norm_g[l], 'mla_kv_norm_g': mla_kv_norm_g[l],
             'ln1_g': ln1_g[l], 'ln1_b': ln1_b[l], 'ffn_conv_w': ffn_conv_w[l], 'ffn_conv_b': ffn_conv_b[l],
             'ln2_g': ln2_g[l], 'ln2_b': ln2_b[l]}
        wts = _layer_weights(l, w_in, mla_w_uq, mla_w_ukv, w_branch, w_out, w_up, w_down)
        mods_p = [m.reshape(bp, 1, d) for m in jnp.split(mod[l, :bp], 6, axis=-1)]
        mods_s = [m.reshape(bs, 1, d) for m in jnp.split(mod[l, bp:n_c], 6, axis=-1)]
        xp, st_p = _run_layer(xp, mods_p, None, wts, p, ln0=(l == 0), g0=g0, b0=b0, alpha=alpha,
                              cs=cs_p, sn=sn_p, tiles=tiles_p, prompt=True)
        st_in = (cache_mla_latent, cache_mla_kpe, l, state_conv_a[l], state_gdn_conv[l], state_gdn[l],
                 state_ffn_conv[l])
        xs, st_s = _run_layer(xs, mods_s, st_in, wts, p, ln0=(l == 0), g0=g0, b0=b0, alpha=alpha,
                              cs=cs_s, sn=sn_s, tiles=tiles_s, prompt=False)
        p_states.append(st_p)
        s_states.append(st_s)
    p_out = [jnp.stack(z, axis=0) for z in zip(*p_states)]
    s_out = [jnp.stack(z, axis=0) for z in zip(*s_states)]
    return (xp, xs, *p_out, *s_out)
```

```python
import functools
import math

import jax
import jax.numpy as jnp
import numpy as np
from jax import lax
from jax.experimental import pallas as pl
from jax.experimental.pallas import tpu as pltpu

F32 = jnp.float32
BF16 = jnp.bfloat16

D_MODEL = 1024
CHUNK = 64
CONV_A_CH = 512
CONV_A_WIDTH = 31
GDN_HEADS = 4
GDN_DK = 128
GDN_DV = 128
GDN_CONV_WIDTH = 4
GDN_QKV_WIDTH = 2 * GDN_HEADS * GDN_DK + GDN_HEADS * GDN_DV
MLA_HEADS = 4
MLA_Q_LORA = 384
MLA_KV_LORA = 256
MLA_NOPE_DIM = 128
MLA_ROPE_DIM = 64
MLA_V_DIM = 128
ROPE_THETA = 10000.0
N_BRANCH = 3
BRANCH_WIDTH = 512
D_FF = 2816
FFN_CONV_WIDTH = 3
LN_EPS = 1e-5
RMS_EPS = 1e-6

LANES = 128
SUBLANES = 8
VMEM_LIMIT = 48 * 1024 * 1024

QKV_OFF = 0
QLAT_OFF = QKV_OFF + GDN_QKV_WIDTH
SMALL_OFF = QLAT_OFF + MLA_Q_LORA
KVLAT_OFF = SMALL_OFF + LANES
FAT32_W = KVLAT_OFF + MLA_KV_LORA
MLA_BLK_W = FAT32_W - QLAT_OFF
GATE_OFF, GATE_W = 0, N_BRANCH * D_MODEL
PREA_OFF = GATE_OFF + GATE_W
Z_OFF = PREA_OFF + 2 * CONV_A_CH
FAT16_W = Z_OFF + GDN_HEADS * GDN_DV
BETA_LANE = MLA_ROPE_DIM
DEC_LANE = MLA_ROPE_DIM + GDN_HEADS
QK_PAD = 2 * LANES
ATTN_KB = 256
ATTN_CHAINS = 8
VT_ROWS = MLA_V_DIM + 16
ATTN_SCALE = (MLA_NOPE_DIM + MLA_ROPE_DIM) ** -0.5


def _cparams(sem):
    return pltpu.CompilerParams(dimension_semantics=sem, vmem_limit_bytes=VMEM_LIMIT)


def _sigmoid(x):
    return jax.nn.sigmoid(x)


def _silu(x):
    return x * jax.nn.sigmoid(x)


def _ln(x, g, b):
    mu = jnp.mean(x, axis=-1, keepdims=True)
    xc = x - mu
    var = jnp.mean(xc * xc, axis=-1, keepdims=True)
    return xc * lax.rsqrt(var + LN_EPS) * g + b


def _rms(x, g):
    return x * lax.rsqrt(jnp.mean(x * x, axis=-1, keepdims=True) + RMS_EPS) * g


def _dot(a, b):
    return jnp.dot(a, b, preferred_element_type=F32)


def _dot_nt(a, b):
    return lax.dot_general(a, b, (((1,), (1,)), ((), ())), preferred_element_type=F32)


def _dot_tn(a, b):
    return lax.dot_general(a, b, (((0,), (0,)), ((), ())), preferred_element_type=F32)


def _dot_split(a, b):
    ah = a.astype(BF16)
    al = (a - ah.astype(F32)).astype(BF16)
    bh = b.astype(BF16)
    bl = (b - bh.astype(F32)).astype(BF16)
    return _dot(ah, bh) + _dot(ah, bl) + _dot(al, bh)


def _split3(x):
    hi = x.astype(BF16)
    r = x - hi.astype(F32)
    mid = r.astype(BF16)
    lo = (r - mid.astype(F32)).astype(BF16)
    return hi, mid, lo


def _mod_kernel(c_ref, w_ref, b_ref, o_ref):
    s = _silu(c_ref[...])
    o_ref[0] = _dot(s.astype(BF16), w_ref[0].astype(BF16)) + b_ref[0]


def _modulation(c_all, w_ada, b_ada):
    depth, d, n = w_ada.shape
    rows = c_all.shape[0]
    tn = 1536
    return pl.pallas_call(
        _mod_kernel,
        grid=(depth, n // tn),
        in_specs=[pl.BlockSpec((rows, d), lambda l, j: (0, 0)),
                  pl.BlockSpec((1, d, tn), lambda l, j: (l, 0, j)),
                  pl.BlockSpec((1, 1, tn), lambda l, j: (l, 0, j))],
        out_specs=pl.BlockSpec((1, rows, tn), lambda l, j: (l, 0, j)),
        out_shape=jax.ShapeDtypeStruct((depth, rows, n), F32),
        compiler_params=_cparams(("arbitrary", "arbitrary")),
        name="modulation",
    )(c_all, w_ada, b_ada.reshape(depth, 1, n))


def _inproj_kernel(x_ref, sc_ref, sh_ref, g0_ref, b0_ref, w_ref, o_ref, *, ln0):
    nb, tb, d = x_ref.shape
    x = x_ref[...]
    if ln0:
        x = _ln(x, g0_ref[...], b0_ref[...])
    h = x * (1.0 + sc_ref[...]) + sh_ref[...]
    o = _dot(h.reshape(nb * tb, d).astype(BF16), w_ref[...])
    o_ref[...] = o.reshape(nb, tb, o.shape[-1]).astype(o_ref.dtype)


def _inproj(x, sc, sh, g0, b0, w_fat, out_dtype, *, nb, tb, ln0):
    bsz, t, d = x.shape
    width = w_fat.shape[1]
    tn = width
    return pl.pallas_call(
        functools.partial(_inproj_kernel, ln0=ln0),
        grid=(width // tn, bsz // nb, t // tb),
        in_specs=[pl.BlockSpec((nb, tb, d), lambda n, b, i: (b, i, 0)),
                  pl.BlockSpec((nb, 1, d), lambda n, b, i: (b, 0, 0)),
                  pl.BlockSpec((nb, 1, d), lambda n, b, i: (b, 0, 0)),
                  pl.BlockSpec((1, d), lambda n, b, i: (0, 0)),
                  pl.BlockSpec((1, d), lambda n, b, i: (0, 0)),
                  pl.BlockSpec((d, tn), lambda n, b, i: (0, n))],
        out_specs=pl.BlockSpec((nb, tb, tn), lambda n, b, i: (b, i, n)),
        out_shape=jax.ShapeDtypeStruct((bsz, t, width), out_dtype),
        compiler_params=_cparams(("arbitrary", "arbitrary", "arbitrary")),
        name="inproj",
    )(x, sc, sh, g0, b0, w_fat)


CONV_A_HALO = 32


def _glu(pre):
    pre = pre.astype(F32)
    return pre[..., :CONV_A_CH] * _sigmoid(pre[..., CONV_A_CH:])


def _conva_kernel(*refs, n_t, nbc, rc):
    if n_t > 1:
        pre_ref, halo_ref, hist_ref, w_ref, cb_ref, g_ref, b_ref, y_ref, ho_ref, xp_ref, xs_ref = refs
    else:
        pre_ref, hist_ref, w_ref, cb_ref, g_ref, b_ref, y_ref, ho_ref, xp_ref, xs_ref = refs
    nb, tb, _ = pre_ref.shape
    t = pl.program_id(1)
    hl = CONV_A_HALO

    @pl.when(t == 0)
    def _():
        xp_ref[:, 0:hl, :] = hist_ref[...]

    if n_t > 1:
        @pl.when(t > 0)
        def _():
            xp_ref[:, 0:hl, :] = _glu(halo_ref[...])

    xp_ref[:, hl:hl + tb, :] = _glu(pre_ref[...])
    xp = xp_ref[...]
    xs_ref[0] = xp
    for b in range(1, SUBLANES):
        xs_ref[b] = pltpu.roll(xp, b, 1)
    for b0 in range(0, nb, nbc):
        for r0 in range(0, tb, rc):
            acc = None
            for k in range(CONV_A_WIDTH):
                back = CONV_A_WIDTH - 1 - k
                whole, part = back // SUBLANES, back % SUBLANES
                start = hl - whole * SUBLANES + r0
                term = w_ref[k:k + 1, :] * xs_ref[part, b0:b0 + nbc, start:start + rc, :]
                acc = term if acc is None else acc + term
            y = _silu(_ln(acc + cb_ref[...], g_ref[...], b_ref[...]))
            y_ref[b0:b0 + nbc, r0:r0 + rc, :] = y.astype(y_ref.dtype)
    ho_ref[...] = xp_ref[:, tb:tb + hl, :]


def _conv_module(fat, hist32, w, cb, g, b, *, nb, tb):
    bsz, t, _ = fat.shape
    n_t = t // tb
    pre_blk = PREA_OFF // (2 * CONV_A_CH)
    hl = CONV_A_HALO
    in_specs = [pl.BlockSpec((nb, tb, 2 * CONV_A_CH), lambda bi, i: (bi, i, pre_blk))]
    args = [fat]
    if n_t > 1:
        per = tb // hl
        in_specs.append(pl.BlockSpec((nb, hl, 2 * CONV_A_CH),
                                     lambda bi, i: (bi, jnp.maximum(i * per - 1, 0), pre_blk)))
        args.append(fat)
    in_specs += [pl.BlockSpec((nb, hl, CONV_A_CH), lambda bi, i: (bi, 0, 0)),
                 pl.BlockSpec((CONV_A_WIDTH, CONV_A_CH), lambda bi, i: (0, 0)),
                 pl.BlockSpec((1, CONV_A_CH), lambda bi, i: (0, 0)),
                 pl.BlockSpec((1, CONV_A_CH), lambda bi, i: (0, 0)),
                 pl.BlockSpec((1, CONV_A_CH), lambda bi, i: (0, 0))]
    args += [hist32, w, cb, g, b]
    if n_t > 1:
        nbc, rc = 1, 64
    else:
        nbc, rc = 4, tb
    return pl.pallas_call(
        functools.partial(_conva_kernel, n_t=n_t, nbc=nbc, rc=rc),
        grid=(bsz // nb, n_t),
        in_specs=in_specs,
        out_specs=[pl.BlockSpec((nb, tb, CONV_A_CH), lambda bi, i: (bi, i, 0)),
                   pl.BlockSpec((nb, hl, CONV_A_CH), lambda bi, i: (bi, 0, 0))],
        out_shape=[jax.ShapeDtypeStruct((bsz, t, CONV_A_CH), BF16),
                   jax.ShapeDtypeStruct((bsz, hl, CONV_A_CH), F32)],
        scratch_shapes=[pltpu.VMEM((nb, hl + tb, CONV_A_CH), F32),
                        pltpu.VMEM((SUBLANES, nb, hl + tb, CONV_A_CH), F32)],
        compiler_params=_cparams(("arbitrary", "arbitrary")),
        name="conv_module",
    )(*args)


GDN_ROWS = 512
GDN_SUB = 128


def _gdn_kernel(*refs, n_t, chunk):
    if n_t > 1:
        (qkv_ref, halo_ref, hist_ref, z_ref, small_ref, st_ref, cw_ref, alog_ref, dt_ref, ng_ref,
         y_ref, ho_ref, so_ref, xp_ref, s_ref, vn_ref, os_ref) = refs
    else:
        (qkv_ref, hist_ref, z_ref, small_ref, st_ref, cw_ref, alog_ref, dt_ref, ng_ref,
         y_ref, ho_ref, so_ref, xp_ref, s_ref, vn_ref, os_ref) = refs
    nb, tb, _ = qkv_ref.shape
    rows = nb * tb
    t = pl.program_id(1)
    hl = SUBLANES
    nk = GDN_HEADS * GDN_DK

    @pl.when(t == 0)
    def _():
        xp_ref[:, 0:hl, :] = hist_ref[...]
        s_ref[...] = st_ref[...]

    if n_t > 1:
        @pl.when(t > 0)
        def _():
            xp_ref[:, 0:hl, :] = halo_ref[...]

    xp_ref[:, hl:hl + tb, :] = qkv_ref[...]
    xp = xp_ref[...]
    acc = cw_ref[0:1, :] * xp
    for k in range(1, GDN_CONV_WIDTH):
        acc = cw_ref[k:k + 1, :] * xp + pltpu.roll(acc, 1, 1)
    c = _silu(acc[:, hl:hl + tb, :]).reshape(rows, GDN_QKV_WIDTH)
    ho_ref[...] = xp_ref[:, tb:tb + hl, :]

    sm = small_ref[...].reshape(rows, LANES)
    beta_all = _sigmoid(sm)
    xg = sm + dt_ref[...]
    softplus = jnp.maximum(xg, 0.0) + jnp.log1p(jnp.exp(-jnp.abs(xg)))
    g_all = -jnp.exp(alog_ref[...]) * softplus

    rs = GDN_SUB
    n_sub = rows // rs
    shift = int(math.log2(chunk))
    ri = lax.broadcasted_iota(jnp.int32, (rs, rs), 0)
    ci = lax.broadcasted_iota(jnp.int32, (rs, rs), 1)
    same = (ri >> shift) == (ci >> shift)
    tri = same & (ci <= ri)
    strict = same & (ci < ri)
    tri_b = jnp.where(tri, 1.0, 0.0).astype(BF16)
    triu_b = jnp.where(same & (ri <= ci), 1.0, 0.0).astype(BF16)
    same_b = jnp.where(same, 1.0, 0.0).astype(BF16)
    gc_col, gc_row, gl_col = [], [], []
    for s in range(n_sub):
        g3 = _split3(g_all[s * rs:(s + 1) * rs])
        gc_col.append(_dot(tri_b, g3[0]) + _dot(tri_b, g3[1]) + _dot(tri_b, g3[2]))
        gc_row.append(_dot_tn(g3[0], triu_b) + _dot_tn(g3[1], triu_b) + _dot_tn(g3[2], triu_b))
        gl_col.append(_dot(same_b, g3[0]) + _dot(same_b, g3[1]) + _dot(same_b, g3[2]))

    chains = [(s, h) for s in range(n_sub) for h in range(GDN_HEADS)]
    each = lambda f, *ls: [f(*xs) for xs in zip(*ls)]
    qs, ks, kbs, rhss, decays, betas, egs, gcs, gls = [], [], [], [], [], [], [], [], []
    for s, h in chains:
        r = slice(s * rs, (s + 1) * rs)
        qh = c[r, h * GDN_DK:(h + 1) * GDN_DK]
        kh = c[r, nk + h * GDN_DK:nk + (h + 1) * GDN_DK]
        vh = c[r, 2 * nk + h * GDN_DV:2 * nk + (h + 1) * GDN_DV]
        qh = qh * lax.rsqrt(jnp.sum(qh * qh, axis=-1, keepdims=True) + RMS_EPS) * (GDN_DK ** -0.5)
        kh = kh * lax.rsqrt(jnp.sum(kh * kh, axis=-1, keepdims=True) + RMS_EPS)
        beta = beta_all[r, BETA_LANE + h:BETA_LANE + h + 1]
        gc = gc_col[s][:, DEC_LANE + h:DEC_LANE + h + 1]
        gr = gc_row[s][DEC_LANE + h:DEC_LANE + h + 1, :]
        eg = jnp.exp(gc)
        qs.append(qh)
        ks.append(kh)
        kbs.append(kh.astype(BF16))
        betas.append(beta)
        gcs.append(gc)
        gls.append(gl_col[s][:, DEC_LANE + h:DEC_LANE + h + 1])
        egs.append(eg)
        decays.append(jnp.where(tri, jnp.exp(jnp.where(tri, gc - gr, 0.0)), 0.0))
        rhss.append(jnp.concatenate([vh * beta, kh * (beta * eg)], axis=1))
    kks = each(lambda kb: _dot_nt(kb, kb), kbs)
    a0s = each(lambda beta, kk, decay: jnp.where(strict, -(beta * kk * decay), 0.0), betas, kks, decays)
    ms = a0s
    pows = a0s
    for _ in range(1, shift):
        pows = each(lambda a: _dot(a.astype(BF16), a.astype(BF16)), pows)
        ms = each(lambda m, a: m + a + _dot(m.astype(BF16), a.astype(BF16)), ms, pows)
    mbs = each(lambda m: m.astype(BF16), ms)
    x0s = each(lambda rhs, mb: rhs + _dot(mb, rhs.astype(BF16)), rhss, mbs)
    ress = each(lambda rhs, x0, a0: rhs - x0 + _dot_split(a0, x0), rhss, x0s, a0s)
    sols = each(lambda x0, res, mb: x0 + res + _dot(mb, res.astype(BF16)), x0s, ress, mbs)
    qkms = each(lambda qh, kb, decay: jnp.where(tri, _dot_nt(qh.astype(BF16), kb) * decay, 0.0).astype(BF16),
                qs, kbs, decays)
    us = each(lambda sol: sol[:, :GDN_DV], sols)
    wbs = each(lambda sol: sol[:, GDN_DV:].astype(BF16), sols)
    qds = each(lambda qh, eg: (qh * eg).astype(BF16), qs, egs)
    kds = each(lambda kh, gl, gc: (kh * jnp.exp(gl - gc)).astype(BF16), ks, gls, gcs)
    egls = each(jnp.exp, gls)
    for j in range(rows // chunk):
        bi = (j * chunk) // tb
        s = (j * chunk) // rs
        r0 = j * chunk - s * rs
        for h in range(GDN_HEADS):
            i = s * GDN_HEADS + h
            sc = s_ref[bi, h]
            sb = sc.astype(BF16)
            vnew = us[i][r0:r0 + chunk] - _dot(wbs[i][r0:r0 + chunk], sb)
            os_ref[h, j * chunk:(j + 1) * chunk, :] = _dot(qds[i][r0:r0 + chunk], sb)
            vn_ref[h, j * chunk:(j + 1) * chunk, :] = vnew
            s_ref[bi, h] = sc * egls[i][r0:r0 + 1, :] + _dot_tn(kds[i][r0:r0 + chunk], vnew.astype(BF16))
    z = z_ref[...].astype(F32).reshape(rows, GDN_HEADS * GDN_DV)
    for i, (s, h) in enumerate(chains):
        r = slice(s * rs, (s + 1) * rs)
        o = os_ref[h, r, :] + _dot(qkms[i], vn_ref[h, r, :].astype(BF16))
        on = _rms(o, ng_ref[...])
        y = (on * _silu(z[r, h * GDN_DV:(h + 1) * GDN_DV])).astype(y_ref.dtype)
        if tb >= rs:
            y_ref[0, s * rs:(s + 1) * rs, h * GDN_DV:(h + 1) * GDN_DV] = y
        else:
            y_ref[s * (rs // tb):(s + 1) * (rs // tb), :, h * GDN_DV:(h + 1) * GDN_DV] = y.reshape(rs // tb, tb, GDN_DV)
    so_ref[...] = s_ref[...]


def _gated_deltanet(fat, fat16, hist8, state, cw, alog_row, dt_row, ng, *, nb, tb):
    bsz, t, _ = fat.shape
    n_t = t // tb
    chunk = min(CHUNK, t)
    hl = SUBLANES
    qkv_blk = QKV_OFF // GDN_QKV_WIDTH
    z_blk = Z_OFF // (GDN_HEADS * GDN_DV)
    small_blk = SMALL_OFF // LANES
    in_specs = [pl.BlockSpec((nb, tb, GDN_QKV_WIDTH), lambda bi, i: (bi, i, qkv_blk))]
    args = [fat]
    if n_t > 1:
        per = tb // hl
        in_specs.append(pl.BlockSpec((nb, hl, GDN_QKV_WIDTH), lambda bi, i: (bi, jnp.maximum(i * per - 1, 0), qkv_blk)))
        args.append(fat)
    in_specs += [pl.BlockSpec((nb, hl, GDN_QKV_WIDTH), lambda bi, i: (bi, 0, 0)),
                 pl.BlockSpec((nb, tb, GDN_HEADS * GDN_DV), lambda bi, i: (bi, i, z_blk)),
                 pl.BlockSpec((nb, tb, LANES), lambda bi, i: (bi, i, small_blk)),
                 pl.BlockSpec((nb, GDN_HEADS, GDN_DK, GDN_DV), lambda bi, i: (bi, 0, 0, 0)),
                 pl.BlockSpec((GDN_CONV_WIDTH, GDN_QKV_WIDTH), lambda bi, i: (0, 0)),
                 pl.BlockSpec((1, LANES), lambda bi, i: (0, 0)),
                 pl.BlockSpec((1, LANES), lambda bi, i: (0, 0)),
                 pl.BlockSpec((1, GDN_DV), lambda bi, i: (0, 0))]
    args += [hist8, fat16, fat, state, cw, alog_row, dt_row, ng]
    rows = nb * tb
    return pl.pallas_call(
        functools.partial(_gdn_kernel, n_t=n_t, chunk=chunk),
        grid=(bsz // nb, n_t),
        in_specs=in_specs,
        out_specs=[pl.BlockSpec((nb, tb, GDN_HEADS * GDN_DV), lambda bi, i: (bi, i, 0)),
                   pl.BlockSpec((nb, hl, GDN_QKV_WIDTH), lambda bi, i: (bi, 0, 0)),
                   pl.BlockSpec((nb, GDN_HEADS, GDN_DK, GDN_DV), lambda bi, i: (bi, 0, 0, 0))],
        out_shape=[jax.ShapeDtypeStruct((bsz, t, GDN_HEADS * GDN_DV), BF16),
                   jax.ShapeDtypeStruct((bsz, hl, GDN_QKV_WIDTH), F32),
                   jax.ShapeDtypeStruct((bsz, GDN_HEADS, GDN_DK, GDN_DV), F32)],
        scratch_shapes=[pltpu.VMEM((nb, hl + tb, GDN_QKV_WIDTH), F32),
                        pltpu.VMEM((nb, GDN_HEADS, GDN_DK, GDN_DV), F32),
                        pltpu.VMEM((GDN_HEADS, rows, GDN_DV), F32),
                        pltpu.VMEM((GDN_HEADS, rows, GDN_DV), F32)],
        compiler_params=_cparams(("arbitrary", "arbitrary")),
        name="gated_deltanet",
    )(*args)


def _mla_prep_kernel(*refs, emit_kv, scale):
    if emit_kv:
        (blk_ref, cs_ref, sn_ref, qg_ref, kg_ref, wq_ref, wqs_ref, wk_ref, wvt_ref,
         q_ref, lat_ref, kpe_ref, k_ref, vt_ref) = refs
    else:
        (blk_ref, cs_ref, sn_ref, qg_ref, kg_ref, wq_ref, wqs_ref,
         q_ref, lat_ref, kpe_ref) = refs
    nb, tb, _ = blk_ref.shape
    rows = nb * tb
    blk = blk_ref[...]
    cs = cs_ref[...]
    sn = sn_ref[...]
    qn = _rms(blk[..., :MLA_Q_LORA], qg_ref[...]).reshape(rows, MLA_Q_LORA).astype(BF16)
    qf = _dot(qn, wq_ref[...]).reshape(nb, tb, MLA_HEADS * QK_PAD)
    qs = _dot(qn, wqs_ref[...]).reshape(nb, tb, MLA_HEADS * LANES)
    for h in range(MLA_HEADS):
        nope = qf[..., h * QK_PAD:h * QK_PAD + MLA_NOPE_DIM]
        rope = qf[..., h * QK_PAD + MLA_NOPE_DIM:(h + 1) * QK_PAD] * cs + qs[..., h * LANES:(h + 1) * LANES] * sn
        q_ref[h, :, :, 0:MLA_NOPE_DIM] = (nope * scale).astype(q_ref.dtype)
        q_ref[h, :, :, MLA_NOPE_DIM:QK_PAD] = (rope * scale).astype(q_ref.dtype)

    lat = _rms(blk[..., MLA_Q_LORA + LANES:], kg_ref[...])
    lat_ref[...] = lat

    small = blk[..., MLA_Q_LORA:MLA_Q_LORA + LANES].reshape(rows, LANES)
    half = MLA_ROPE_DIM // 2
    lane = lax.broadcasted_iota(jnp.int32, (rows, LANES), 1)
    swapped = jnp.where(lane < half, pltpu.roll(small, LANES - half, 1), pltpu.roll(small, half, 1))
    kpe = small.reshape(nb, tb, LANES) * cs + swapped.reshape(nb, tb, LANES) * sn
    kpe_ref[...] = kpe[..., :MLA_ROPE_DIM]

    if emit_kv:
        latb = lat.reshape(rows, MLA_KV_LORA).astype(BF16)
        kn = _dot(latb, wk_ref[...]).reshape(nb, tb, MLA_HEADS * MLA_NOPE_DIM)
        ones_rows = jnp.where(lax.broadcasted_iota(jnp.int32, (VT_ROWS - MLA_V_DIM, ATTN_KB), 0) == 0,
                              1.0, 0.0).astype(vt_ref.dtype)
        for h in range(MLA_HEADS):
            k_ref[h, :, :, 0:MLA_NOPE_DIM] = kn[..., h * MLA_NOPE_DIM:(h + 1) * MLA_NOPE_DIM].astype(k_ref.dtype)
            k_ref[h, :, :, MLA_NOPE_DIM:QK_PAD] = kpe.astype(k_ref.dtype)
            vt = _dot_nt(wvt_ref[h], latb).astype(vt_ref.dtype)
            for kb in range(rows // ATTN_KB):
                vt_ref[h, 0, kb, 0:MLA_V_DIM, :] = vt[:, kb * ATTN_KB:(kb + 1) * ATTN_KB]
                vt_ref[h, 0, kb, MLA_V_DIM:VT_ROWS, :] = ones_rows


def _mla_prep(fat, cs, sn, qg, kg, wq, wqs, wk, wvt, *, nb, tb, emit_kv, scale):
    bsz, t, _ = fat.shape
    blk = QLAT_OFF // MLA_BLK_W
    hq = MLA_HEADS
    in_specs = [pl.BlockSpec((nb, tb, MLA_BLK_W), lambda bi, i: (bi, i, blk)),
                pl.BlockSpec((tb, LANES), lambda bi, i: (i, 0)),
                pl.BlockSpec((tb, LANES), lambda bi, i: (i, 0)),
                pl.BlockSpec((1, MLA_Q_LORA), lambda bi, i: (0, 0)),
                pl.BlockSpec((1, MLA_KV_LORA), lambda bi, i: (0, 0)),
                pl.BlockSpec(wq.shape, lambda bi, i: (0, 0)),
                pl.BlockSpec(wqs.shape, lambda bi, i: (0, 0))]
    args = [fat, cs, sn, qg, kg, wq, wqs]
    out_specs = [pl.BlockSpec((hq, nb, tb, QK_PAD), lambda bi, i: (0, bi, i, 0)),
                 pl.BlockSpec((nb, tb, MLA_KV_LORA), lambda bi, i: (bi, i, 0)),
                 pl.BlockSpec((nb, tb, MLA_ROPE_DIM), lambda bi, i: (bi, i, 0))]
    out_shape = [jax.ShapeDtypeStruct((hq, bsz, t, QK_PAD), BF16),
                 jax.ShapeDtypeStruct((bsz, t, MLA_KV_LORA), F32),
                 jax.ShapeDtypeStruct((bsz, t, MLA_ROPE_DIM), F32)]
    if emit_kv:
        assert nb == 1 and tb % ATTN_KB == 0
        in_specs += [pl.BlockSpec(wk.shape, lambda bi, i: (0, 0)),
                     pl.BlockSpec(wvt.shape, lambda bi, i: (0, 0, 0))]
        args += [wk, wvt]
        per = tb // ATTN_KB
        out_specs += [pl.BlockSpec((hq, nb, tb, QK_PAD), lambda bi, i: (0, bi, i, 0)),
                      pl.BlockSpec((hq, nb, per, VT_ROWS, ATTN_KB), lambda bi, i: (0, bi, i, 0, 0))]
        out_shape += [jax.ShapeDtypeStruct((hq, bsz, t, QK_PAD), BF16),
                      jax.ShapeDtypeStruct((hq, bsz, t // ATTN_KB, VT_ROWS, ATTN_KB), BF16)]
    return pl.pallas_call(
        functools.partial(_mla_prep_kernel, emit_kv=emit_kv, scale=scale),
        grid=(bsz // nb, t // tb),
        in_specs=in_specs, out_specs=out_specs, out_shape=out_shape,
        compiler_params=_cparams(("arbitrary", "arbitrary")),
        name="mla_prep",
    )(*args)


def _attn_prompt_kernel(q_ref, k_ref, vt_ref, o_ref, m_ref, acc_ref, st_ref, *, nq):
    kb = ATTN_KB
    j = pl.program_id(2)
    m_ref[...] = jnp.full(m_ref.shape, -jnp.inf, F32)
    acc_ref[...] = jnp.zeros(acc_ref.shape, F32)

    def scores(chains, ki):
        k0 = pl.multiple_of(ki * kb, kb)
        kblk = k_ref[0, 0, pl.ds(k0, kb), :]
        return [_dot_nt(kblk, q_ref[0, 0, c * kb:(c + 1) * kb, :]) for c in chains]

    def finish(chains, sts, ki, masked_chain):
        pts, alphas = [], []
        for c, st in zip(chains, sts):
            if c == masked_chain:
                key_chunk = lax.broadcasted_iota(jnp.int32, (kb, kb), 0) // CHUNK
                qry_chunk = lax.broadcasted_iota(jnp.int32, (kb, kb), 1) // CHUNK
                st = jnp.where(key_chunk <= qry_chunk, st, -jnp.inf)
            m_old = m_ref[c]
            m_new = jnp.maximum(m_old, jnp.max(st, axis=0, keepdims=True))
            pts.append(jnp.exp2(st - m_new).astype(BF16))
            alphas.append(jnp.exp2(m_old - m_new))
            m_ref[c] = m_new
        vblk = vt_ref[0, 0, ki]
        for c, pt, alpha in zip(chains, pts, alphas):
            acc_ref[c] = alpha * acc_ref[c] + _dot(vblk, pt)

    everyone = range(nq)
    for c, st in zip(everyone, scores(everyone, 0)):
        st_ref[c] = st

    def step(ki):
        cur = [st_ref[c] for c in everyone]
        nxt = scores(everyone, ki + 1)
        finish(everyone, cur, ki, None)
        for c in everyone:
            st_ref[c] = nxt[c]

    per_trip = nq

    def body(kp, carry):
        for u in range(per_trip):
            step(per_trip * kp + u)
        return carry

    lax.fori_loop(0, (nq // per_trip) * j, body, 0)
    cur = [st_ref[c] for c in everyone]
    for d in range(nq):
        nxt = scores(range(d + 1, nq), nq * j + d + 1) if d + 1 < nq else []
        finish(range(d, nq), cur, nq * j + d, d)
        cur = nxt
    for c in range(nq):
        acc = acc_ref[c]
        ot = acc[0:MLA_V_DIM] / acc[MLA_V_DIM:MLA_V_DIM + 1]
        o_ref[0, c * kb:(c + 1) * kb, :] = ot.T.astype(o_ref.dtype)


def _attn_prompt(q, k, vt, *, nq):
    hq, bsz, t, _ = q.shape
    tq = ATTN_KB * nq
    return pl.pallas_call(
        functools.partial(_attn_prompt_kernel, nq=nq),
        grid=(bsz, hq, t // tq),
        in_specs=[pl.BlockSpec((1, 1, tq, QK_PAD), lambda b, h, i: (h, b, i, 0)),
                  pl.BlockSpec((1, 1, t, QK_PAD), lambda b, h, i: (h, b, 0, 0)),
                  pl.BlockSpec((1, 1, t // ATTN_KB, VT_ROWS, ATTN_KB), lambda b, h, i: (h, b, 0, 0, 0))],
        out_specs=pl.BlockSpec((1, tq, MLA_V_DIM), lambda b, h, i: (b, i, h)),
        out_shape=jax.ShapeDtypeStruct((bsz, t, MLA_HEADS * MLA_V_DIM), BF16),
        scratch_shapes=[pltpu.VMEM((nq, 1, ATTN_KB), F32), pltpu.VMEM((nq, VT_ROWS, ATTN_KB), F32),
                        pltpu.VMEM((nq, ATTN_KB, ATTN_KB), F32)],
        compiler_params=_cparams(("arbitrary", "arbitrary", "arbitrary")),
        name="attn_prompt",
    )(q, k, vt)


def _attn_sample_kernel(q_ref, latc_ref, kpec_ref, latn_ref, kpen_ref, wuk_ref, wuv_ref, o_ref):
    tn = latn_ref.shape[1]
    past = latc_ref.shape[2]
    hq = MLA_HEADS
    qabs = jnp.concatenate(
        [_dot(q_ref[h, 0, :, 0:MLA_NOPE_DIM], wuk_ref[h]) for h in range(hq)], axis=0).astype(BF16)
    qrope = jnp.concatenate([q_ref[h, 0, :, MLA_NOPE_DIM:MLA_NOPE_DIM + MLA_ROPE_DIM] for h in range(hq)], axis=0)
    latc = latc_ref[0, 0].astype(BF16)
    latn = latn_ref[0].astype(BF16)
    sc = _dot_nt(qabs, latc) + _dot_nt(qrope, kpec_ref[0, 0].astype(BF16))
    sn = _dot_nt(qabs, latn) + _dot_nt(qrope, kpen_ref[0].astype(BF16))
    rows = hq * tn
    qpos = past + lax.broadcasted_iota(jnp.int32, (rows, 1), 0) % tn
    kc_pos = lax.broadcasted_iota(jnp.int32, (rows, past), 1)
    kn_pos = past + lax.broadcasted_iota(jnp.int32, (rows, tn), 1)
    sc = jnp.where(kc_pos // CHUNK <= qpos // CHUNK, sc, -jnp.inf)
    sn = jnp.where(kn_pos // CHUNK <= qpos // CHUNK, sn, -jnp.inf)
    m = jnp.maximum(jnp.max(sc, axis=-1, keepdims=True), jnp.max(sn, axis=-1, keepdims=True))
    pc = jnp.exp(sc - m)
    pn = jnp.exp(sn - m)
    l = jnp.sum(pc, axis=-1, keepdims=True) + jnp.sum(pn, axis=-1, keepdims=True)
    olat = (_dot(pc.astype(BF16), latc) + _dot(pn.astype(BF16), latn)) / l
    for h in range(hq):
        oh = _dot(olat[h * tn:(h + 1) * tn].astype(BF16), wuv_ref[h])
        o_ref[0, :, h * MLA_V_DIM:(h + 1) * MLA_V_DIM] = oh.astype(o_ref.dtype)


def _attn_sample(q, caches_lat, caches_kpe, layer, lat_new, kpe_new, wuk_t, wuv):
    hq, bsz, tn, _ = q.shape
    past = caches_lat.shape[2]
    return pl.pallas_call(
        _attn_sample_kernel,
        grid=(bsz,),
        in_specs=[pl.BlockSpec((hq, 1, tn, QK_PAD), lambda b: (0, b, 0, 0)),
                  pl.BlockSpec((1, 1, past, MLA_KV_LORA), lambda b: (layer, b, 0, 0)),
                  pl.BlockSpec((1, 1, past, MLA_ROPE_DIM), lambda b: (layer, b, 0, 0)),
                  pl.BlockSpec((1, tn, MLA_KV_LORA), lambda b: (b, 0, 0)),
                  pl.BlockSpec((1, tn, MLA_ROPE_DIM), lambda b: (b, 0, 0)),
                  pl.BlockSpec(wuk_t.shape, lambda b: (0, 0, 0)),
                  pl.BlockSpec(wuv.shape, lambda b: (0, 0, 0))],
        out_specs=pl.BlockSpec((1, tn, MLA_HEADS * MLA_V_DIM), lambda b: (b, 0, 0)),
        out_shape=jax.ShapeDtypeStruct((bsz, tn, MLA_HEADS * MLA_V_DIM), BF16),
        compiler_params=_cparams(("arbitrary",)),
        name="attn_sample",
    )(q, caches_lat, caches_kpe, lat_new, kpe_new, wuk_t, wuv)


def _merge_kernel(ya_ref, yb_ref, yc_ref, gate_ref, x_ref, g1_ref, g0_ref, b0_ref, wb_ref, wo_ref,
                  lg_ref, lb_ref, o_ref, *, ln0, alpha):
    nb, tb, d = x_ref.shape
    rows = nb * tb
    x = x_ref[...]
    if ln0:
        x = _ln(x, g0_ref[...], b0_ref[...])
    merged = None
    for i, y_ref in enumerate((ya_ref, yb_ref, yc_ref)):
        proj = _dot(y_ref[...].reshape(rows, BRANCH_WIDTH), wb_ref[i])
        gate = _sigmoid(gate_ref[:, :, i * d:(i + 1) * d].astype(F32)).reshape(rows, d)
        merged = gate * proj if merged is None else merged + gate * proj
    out = _dot(merged.astype(BF16), wo_ref[...]).reshape(nb, tb, d)
    o_ref[...] = _ln(alpha * x + (1.0 + g1_ref[...]) * out, lg_ref[...], lb_ref[...])


def _merge(ya, yb, yc, fat, x, g1, g0, b0, wb, wo, lg, lb, *, nb, tb, ln0, alpha):
    bsz, t, d = x.shape
    row = lambda bi, i: (bi, i, 0)
    const2 = lambda bi, i: (0, 0)
    return pl.pallas_call(
        functools.partial(_merge_kernel, ln0=ln0, alpha=alpha),
        grid=(bsz // nb, t // tb),
        in_specs=[pl.BlockSpec((nb, tb, BRANCH_WIDTH), row),
                  pl.BlockSpec((nb, tb, BRANCH_WIDTH), row),
                  pl.BlockSpec((nb, tb, BRANCH_WIDTH), row),
                  pl.BlockSpec((nb, tb, GATE_W), row),
                  pl.BlockSpec((nb, tb, d), row),
                  pl.BlockSpec((nb, 1, d), lambda bi, i: (bi, 0, 0)),
                  pl.BlockSpec((1, d), const2),
                  pl.BlockSpec((1, d), const2),
                  pl.BlockSpec(wb.shape, lambda bi, i: (0, 0, 0)),
                  pl.BlockSpec(wo.shape, const2),
                  pl.BlockSpec((1, d), const2),
                  pl.BlockSpec((1, d), const2)],
        out_specs=pl.BlockSpec((nb, tb, d), row),
        out_shape=jax.ShapeDtypeStruct((bsz, t, d), F32),
        compiler_params=_cparams(("arbitrary", "arbitrary")),
        name="merge",
    )(ya, yb, yc, fat, x, g1, g0, b0, wb, wo, lg, lb)


FFN_BLOCK = 1408


def _ffn_kernel(x_ref, sc_ref, sh_ref, g2_ref, wa_ref, wv_ref, wd_ref, cw_ref, cb_ref, lg_ref, lb_ref, hist_ref,
                o_ref, ho_ref, h_ref, acc_ref, carry_ref, ap_ref, *, alpha):
    nb, tb, d = x_ref.shape
    rows = nb * tb
    t = pl.program_id(1)
    f = pl.program_id(2)
    n_f = pl.num_programs(2)
    hl = SUBLANES
    fb = wa_ref.shape[1]

    @pl.when(f == 0)
    def _():
        h = x_ref[...] * (1.0 + sc_ref[...]) + sh_ref[...]
        h_ref[...] = h.reshape(rows, d).astype(BF16)

    @pl.when(t == 0)
    def _():
        carry_ref[f] = hist_ref[...]

    h = h_ref[...]
    a = _dot(h, wa_ref[...]).reshape(nb, tb, fb)
    v = _dot(h, wv_ref[...]).reshape(nb, tb, fb)
    ap_ref[:, 0:hl, :] = carry_ref[f]
    ap_ref[:, hl:hl + tb, :] = a
    ap = ap_ref[...]
    conv = cw_ref[0:1, :] * ap
    for k in range(1, FFN_CONV_WIDTH):
        conv = cw_ref[k:k + 1, :] * ap + pltpu.roll(conv, 1, 1)
    act = _silu(conv[:, hl:hl + tb, :] + cb_ref[...]) * v
    contrib = _dot(act.reshape(rows, fb).astype(BF16), wd_ref[...])
    last = ap_ref[:, tb:tb + hl, :]
    carry_ref[f] = last
    ho_ref[0] = last

    @pl.when(f == 0)
    def _():
        acc_ref[...] = contrib

    @pl.when(f > 0)
    def _():
        acc_ref[...] = acc_ref[...] + contrib

    @pl.when(f == n_f - 1)
    def _():
        y = acc_ref[...].reshape(nb, tb, d)
        o_ref[...] = _ln(alpha * x_ref[...] + (1.0 + g2_ref[...]) * y, lg_ref[...], lb_ref[...])


def _ffn(x, sc, sh, g2, wa, wv, wd, cw, cb, lg, lb, hist8, *, nb, tb, alpha):
    bsz, t, d = x.shape
    dff = wa.shape[1]
    fb = FFN_BLOCK
    n_f = dff // fb
    hl = SUBLANES
    rows = nb * tb
    row = lambda bi, i, f: (bi, i, 0)
    mod = lambda bi, i, f: (bi, 0, 0)
    const2 = lambda bi, i, f: (0, 0)
    return pl.pallas_call(
        functools.partial(_ffn_kernel, alpha=alpha),
        grid=(bsz // nb, t // tb, n_f),
        in_specs=[pl.BlockSpec((nb, tb, d), row),
                  pl.BlockSpec((nb, 1, d), mod),
                  pl.BlockSpec((nb, 1, d), mod),
                  pl.BlockSpec((nb, 1, d), mod),
                  pl.BlockSpec((d, fb), lambda bi, i, f: (0, f)),
                  pl.BlockSpec((d, fb), lambda bi, i, f: (0, f)),
                  pl.BlockSpec((fb, d), lambda bi, i, f: (f, 0)),
                  pl.BlockSpec((FFN_CONV_WIDTH, fb), lambda bi, i, f: (0, f)),
                  pl.BlockSpec((1, fb), lambda bi, i, f: (0, f)),
                  pl.BlockSpec((1, d), const2),
                  pl.BlockSpec((1, d), const2),
                  pl.BlockSpec((nb, hl, fb), lambda bi, i, f: (bi, 0, f))],
        out_specs=[pl.BlockSpec((nb, tb, d), row),
                   pl.BlockSpec((1, nb, hl, fb), lambda bi, i, f: (i, bi, 0, f))],
        out_shape=[jax.ShapeDtypeStruct((bsz, t, d), F32),
                   jax.ShapeDtypeStruct((t // tb, bsz, hl, dff), F32)],
        scratch_shapes=[pltpu.VMEM((rows, d), BF16),
                        pltpu.VMEM((rows, d), F32),
                        pltpu.VMEM((n_f, nb, hl, fb), F32),
                        pltpu.VMEM((nb, hl + tb, fb), F32)],
        compiler_params=_cparams(("arbitrary", "arbitrary", "arbitrary")),
        name="conv_ffn",
    )(x, sc, sh, g2, wa, wv, wd, cw, cb, lg, lb, hist8)


def _front_pad(hist, rows):
    b, r, c = hist.shape
    return jnp.concatenate([jnp.zeros((b, rows - r, c), hist.dtype), hist], axis=1)


def _rope_rows(pos):
    half = MLA_ROPE_DIM // 2
    inv_freq = ROPE_THETA ** (-jnp.arange(half, dtype=F32) / half)
    ang = pos.astype(F32)[:, None] * inv_freq[None, :]
    cos, sin = jnp.cos(ang), jnp.sin(ang)
    pad = jnp.zeros((pos.shape[0], LANES - MLA_ROPE_DIM), F32)
    return jnp.concatenate([cos, cos, pad], axis=1), jnp.concatenate([-sin, sin, pad], axis=1)


def _lane_row(vals, lane0):
    return jnp.zeros((1, LANES), F32).at[0, lane0:lane0 + vals.shape[0]].set(vals.astype(F32))


W_IN_OFFS = np.cumsum([0, 2 * CONV_A_CH, GDN_QKV_WIDTH, GDN_HEADS * GDN_DV, GDN_HEADS, GDN_HEADS,
                       MLA_Q_LORA, MLA_KV_LORA, MLA_ROPE_DIM, N_BRANCH * D_MODEL]).tolist()
W_PREP_ROWS = 128


def _w_in_prep_kernel(w_ref, o32_ref, o16_ref):
    w = w_ref[0]
    seg = lambda i: w[:, W_IN_OFFS[i]:W_IN_OFFS[i + 1]].astype(BF16)
    pad = jnp.zeros((w.shape[0], LANES - MLA_ROPE_DIM - 2 * GDN_HEADS), BF16)
    o32_ref[...] = jnp.concatenate([seg(1), seg(5), seg(7), seg(3), seg(4), pad, seg(6)], axis=1)
    o16_ref[...] = jnp.concatenate([seg(8), seg(0), seg(2)], axis=1)


def _w_in_prep(w_in, l):
    _, d, n = w_in.shape
    return pl.pallas_call(
        _w_in_prep_kernel,
        grid=(d // W_PREP_ROWS,),
        in_specs=[pl.BlockSpec((1, W_PREP_ROWS, n), lambda i: (l, i, 0))],
        out_specs=[pl.BlockSpec((W_PREP_ROWS, FAT32_W), lambda i: (i, 0)),
                   pl.BlockSpec((W_PREP_ROWS, FAT16_W), lambda i: (i, 0))],
        out_shape=[jax.ShapeDtypeStruct((d, FAT32_W), BF16), jax.ShapeDtypeStruct((d, FAT16_W), BF16)],
        compiler_params=_cparams(("arbitrary",)),
        name="w_in_prep",
    )(w_in)


def _layer_weights(l, w_in, mla_w_uq, mla_w_ukv, w_branch, w_out, w_up, w_down):
    w_fat32, w_fat16 = _w_in_prep(w_in, l)
    hd = MLA_NOPE_DIM + MLA_ROPE_DIM
    half = MLA_ROPE_DIM // 2
    wq = mla_w_uq[l]
    zq = jnp.zeros((MLA_Q_LORA, QK_PAD - hd), wq.dtype)
    zs = jnp.zeros((MLA_Q_LORA, LANES - MLA_ROPE_DIM), wq.dtype)
    wq_cols, wqs_cols = [], []
    for h in range(MLA_HEADS):
        wq_cols += [wq[:, h * hd:(h + 1) * hd], zq]
        r0 = h * hd + MLA_NOPE_DIM
        wqs_cols += [wq[:, r0 + half:r0 + 2 * half], wq[:, r0:r0 + half], zs]
    wkv = mla_w_ukv[l]
    kvw = MLA_NOPE_DIM + MLA_V_DIM
    wuk_t = jnp.stack([wkv[:, h * kvw:h * kvw + MLA_NOPE_DIM].T for h in range(MLA_HEADS)]).astype(BF16)
    wuv = jnp.stack([wkv[:, h * kvw + MLA_NOPE_DIM:(h + 1) * kvw] for h in range(MLA_HEADS)]).astype(BF16)
    return dict(
        w_fat32=w_fat32, w_fat16=w_fat16,
        wq=jnp.concatenate(wq_cols, axis=1).astype(BF16),
        wqs=jnp.concatenate(wqs_cols, axis=1).astype(BF16),
        wk=jnp.concatenate([wkv[:, h * kvw:h * kvw + MLA_NOPE_DIM] for h in range(MLA_HEADS)], axis=1).astype(BF16),
        wvt=jnp.transpose(wuv, (0, 2, 1)), wuk_t=wuk_t, wuv=wuv,
        wb=w_branch[l].astype(BF16), wo=w_out[l].astype(BF16),
        wa=w_up[l][:, :D_FF].astype(BF16), wv=w_up[l][:, D_FF:].astype(BF16), wd=w_down[l].astype(BF16),
    )


def _run_layer(x, mods, st, wts, p, *, ln0, g0, b0, alpha, cs, sn, tiles, prompt):
    bsz, t, d = x.shape
    sh1, sc1, g1, sh2, sc2, g2 = mods
    nb, tb, gdn_nb, gdn_tb = tiles
    if st is None:
        hist_a = jnp.zeros((bsz, CONV_A_WIDTH - 1, CONV_A_CH), F32)
        hist_b = jnp.zeros((bsz, GDN_CONV_WIDTH - 1, GDN_QKV_WIDTH), F32)
        state_b = jnp.zeros((bsz, GDN_HEADS, GDN_DK, GDN_DV), F32)
        hist_f = jnp.zeros((bsz, FFN_CONV_WIDTH - 1, D_FF), F32)
    else:
        caches_lat, caches_kpe, layer, hist_a, hist_b, state_b, hist_f = st
    row = lambda v: v.reshape(1, -1)

    fat = _inproj(x, sc1, sh1, g0, b0, wts['w_fat32'], F32, nb=nb, tb=tb, ln0=ln0)
    fat16 = _inproj(x, sc1, sh1, g0, b0, wts['w_fat16'], BF16, nb=nb, tb=tb, ln0=ln0)
    y_a, ha = _conv_module(fat16, _front_pad(hist_a, CONV_A_HALO), p['conv_a_w'], row(p['conv_a_b']),
                           row(p['ln_a_g']), row(p['ln_a_b']), nb=nb, tb=tb)
    y_b, hb, sb = _gated_deltanet(fat, fat16, _front_pad(hist_b, SUBLANES), state_b, p['gdn_conv_w'],
                                  _lane_row(p['gdn_a_log'], DEC_LANE), _lane_row(p['gdn_dt_bias'], DEC_LANE),
                                  row(p['gdn_norm_g']), nb=gdn_nb, tb=gdn_tb)
    if prompt:
        q, lat, kpe, k, v = _mla_prep(fat, cs, sn, row(p['mla_q_norm_g']), row(p['mla_kv_norm_g']),
                                      wts['wq'], wts['wqs'], wts['wk'], wts['wvt'], nb=nb, tb=tb, emit_kv=True,
                                      scale=ATTN_SCALE * math.log2(math.e))
        y_c = _attn_prompt(q, k, v, nq=math.gcd(ATTN_CHAINS, t // ATTN_KB))
    else:
        q, lat, kpe = _mla_prep(fat, cs, sn, row(p['mla_q_norm_g']), row(p['mla_kv_norm_g']),
                                wts['wq'], wts['wqs'], None, None, nb=nb, tb=tb, emit_kv=False, scale=ATTN_SCALE)
        y_c = _attn_sample(q, caches_lat, caches_kpe, layer, lat, kpe, wts['wuk_t'], wts['wuv'])
    x1 = _merge(y_a, y_b, y_c, fat16, x, g1, g0, b0, wts['wb'], wts['wo'], row(p['ln1_g']), row(p['ln1_b']),
                nb=nb, tb=tb, ln0=ln0, alpha=alpha)
    x2, hf = _ffn(x1, sc2, sh2, g2, wts['wa'], wts['wv'], wts['wd'], p['ffn_conv_w'], row(p['ffn_conv_b']),
                  row(p['ln2_g']), row(p['ln2_b']), _front_pad(hist_f, SUBLANES), nb=nb, tb=tb, alpha=alpha)
    states = (lat, kpe, ha[:, CONV_A_HALO - (CONV_A_WIDTH - 1):], hb[:, SUBLANES - (GDN_CONV_WIDTH - 1):],
              sb, hf[-1, :, SUBLANES - (FFN_CONV_WIDTH - 1):])
    return x2, states


def kernel(x_prompt, x_sample, cache_mla_latent, cache_mla_kpe, state_conv_a, state_gdn_conv, state_gdn, state_ffn_conv, c_prompt, c_sample, ln0_g, ln0_b, w_ada, b_ada, w_in, conv_a_w, conv_a_b, ln_a_g, ln_a_b, gdn_conv_w, gdn_a_log, gdn_dt_bias, gdn_norm_g, mla_q_norm_g, mla_kv_norm_g, mla_w_uq, mla_w_ukv, w_branch, w_out, ln1_g, ln1_b, w_up, ffn_conv_w, ffn_conv_b, w_down, ln2_g, ln2_b):
    depth = w_ada.shape[0]
    bp, seq, d = x_prompt.shape
    bs, dec_seq, _ = x_sample.shape
    past = cache_mla_latent.shape[2]
    alpha = (2 * depth) ** 0.25

    n_c = bp + bs
    c_rows = -(-n_c // SUBLANES) * SUBLANES
    c_all = jnp.concatenate([c_prompt, c_sample, jnp.zeros((c_rows - n_c, d), F32)], axis=0)
    mod = _modulation(c_all, w_ada, b_ada)

    cs_p, sn_p = _rope_rows(jnp.arange(seq, dtype=jnp.int32))
    cs_s, sn_s = _rope_rows(past + jnp.arange(dec_seq, dtype=jnp.int32))
    g0, b0 = ln0_g.reshape(1, d), ln0_b.reshape(1, d)

    tile_p = min(512, seq)
    tiles_p = (1, tile_p, 1, min(GDN_ROWS, seq))
    tiles_s = (bs, dec_seq, min(bs, GDN_ROWS // dec_seq), dec_seq)

    xp, xs = x_prompt, x_sample
    p_states, s_states = [], []
    for l in range(depth):
        p = {'conv_a_w': conv_a_w[l], 'conv_a_b': conv_a_b[l], 'ln_a_g': ln_a_g[l], 'ln_a_b': ln_a_b[l],
             'gdn_conv_w': gdn_conv_w[l], 'gdn_a_log': gdn_a_log[l], 'gdn_dt_bias': gdn_dt_bias[l],
             'gdn_norm_g': gdn_norm_g[l], 'mla_q_norm_g': mla_q_norm_g[l], 'mla_kv_norm_g': mla_kv_norm_g[l],
             'ln1_g': ln1_g[l], 'ln1_b': ln1_b[l], 'ffn_conv_w': ffn_conv_w[l], 'ffn_conv_b': ffn_conv_b[l],
             'ln2_g': ln2_g[l], 'ln2_b': ln2_b[l]}
        wts = _layer_weights(l, w_in, mla_w_uq, mla_w_ukv, w_branch, w_out, w_up, w_down)
        mods_p = [m.reshape(bp, 1, d) for m in jnp.split(mod[l, :bp], 6, axis=-1)]
        mods_s = [m.reshape(bs, 1, d) for m in jnp.split(mod[l, bp:n_c], 6, axis=-1)]
        xp, st_p = _run_layer(xp, mods_p, None, wts, p, ln0=(l == 0), g0=g0, b0=b0, alpha=alpha,
                              cs=cs_p, sn=sn_p, tiles=tiles_p, prompt=True)
        st_in = (cache_mla_latent, cache_mla_kpe, l, state_conv_a[l], state_gdn_conv[l], state_gdn[l],
                 state_ffn_conv[l])
        xs, st_s = _run_layer(xs, mods_s, st_in, wts, p, ln0=(l == 0), g0=g0, b0=b0, alpha=alpha,
                              cs=cs_s, sn=sn_s, tiles=tiles_s, prompt=False)
        p_states.append(st_p)
        s_states.append(st_s)
    p_out = [jnp.stack(z, axis=0) for z in zip(*p_states)]
    s_out = [jnp.stack(z, axis=0) for z in zip(*s_states)]
    return (xp, xs, *p_out, *s_out)
```
